```python
import jax
import jax.numpy as jnp
from jax import lax
import numpy as np

D_MODEL = 1024
BATCH = 8
SEQ = 2048
DEPTH = 4
DEC_BATCH = 32
DEC_SEQ = 8
PAST_LEN = 16384
PAGE_SIZE = 128

N_MIXERS = 2
N_MLA_LAYERS = (DEPTH + 1) // 2
N_NSA_LAYERS = DEPTH // 2
EPS = 1e-6
ROPE_THETA = 10000.0
NEG_INF = -1e30
Q_BLOCK = 128

MLA_HEADS = 16
MLA_Q_LORA = 384
MLA_KV_LORA = 256
MLA_NOPE = 64
MLA_ROPE = 32
MLA_V = 64
MLA_ROW = MLA_KV_LORA + MLA_ROPE
MLA_SCALE = (MLA_NOPE + MLA_ROPE) ** -0.5

NSA_HEADS = 16
NSA_GROUPS = 4
NSA_HPG = NSA_HEADS // NSA_GROUPS
NSA_DH = 64
CMP_BLOCK = 32
CMP_STRIDE = 16
CMP_HID = 2 * NSA_DH
SLC_BLOCK = 64
N_SELECT = 16
N_LOCAL = 2
WINDOW = 512
SLC_QBLOCK = 32
FORCE_BONUS = 1e4
NSA_SCALE = NSA_DH ** -0.5
NSA_IN = NSA_HEADS * NSA_DH + 6 * NSA_GROUPS * NSA_DH + 3 * NSA_HEADS

D_FF = 2816
CONV_W = 3

kernel_name = 'mla_nsa_convffn_hybrid_step'


def rmsnorm(x, g):
    xf = x.astype(jnp.float32)
    y = xf * lax.rsqrt(jnp.mean(xf * xf, axis=-1, keepdims=True) + EPS)
    return (y * g.astype(jnp.float32)).astype(x.dtype)


def rope(x, pos):
    d = x.shape[-1]
    inv = ROPE_THETA ** (-jnp.arange(0, d, 2, dtype=jnp.float32) / d)
    ang = pos.astype(jnp.float32)[:, None] * inv[None, :]
    ang = ang.reshape((pos.shape[0],) + (1,) * (x.ndim - 3) + (d // 2,))
    cos, sin = jnp.cos(ang), jnp.sin(ang)
    xf = x.astype(jnp.float32)
    x1, x2 = xf[..., : d // 2], xf[..., d // 2:]
    return jnp.concatenate([x1 * cos - x2 * sin, x1 * sin + x2 * cos], axis=-1).astype(x.dtype)


def masked_softmax(s, mask):
    s = jnp.where(mask, s.astype(jnp.float32), NEG_INF)
    p = jnp.exp(s - jnp.max(s, axis=-1, keepdims=True)) * mask
    den = jnp.sum(p, axis=-1, keepdims=True)
    return p / jnp.where(den > 0, den, 1.0)


def mla_project(xn, pos, w_in, q_norm, kv_norm, w_q_up, w_uk):
    B, T, _ = xn.shape
    a = xn @ w_in
    q_a, c_kv, k_pe = jnp.split(a, [MLA_Q_LORA, MLA_Q_LORA + MLA_KV_LORA], axis=-1)
    q = (rmsnorm(q_a, q_norm) @ w_q_up).reshape(B, T, MLA_HEADS, MLA_NOPE + MLA_ROPE)
    q_nope, q_pe = q[..., :MLA_NOPE], q[..., MLA_NOPE:]
    q_lat = jnp.einsum('bthn,chn->bthc', q_nope, w_uk)
    q_cat = jnp.concatenate([q_lat, rope(q_pe, pos)], axis=-1)
    kv_row = jnp.concatenate([rmsnorm(c_kv, kv_norm), rope(k_pe, pos)], axis=-1)
    return q_cat, kv_row


def latent_attend(q_cat, kv, mask):
    s = jnp.einsum('bthc,bsc->bths', q_cat, kv) * MLA_SCALE
    p = masked_softmax(s, mask[None, :, None, :])
    return jnp.einsum('bths,bsc->bthc', p.astype(kv.dtype), kv[..., :MLA_KV_LORA])


def mla_out(o_lat, w_uv, w_out):
    B, T = o_lat.shape[:2]
    o = jnp.einsum('bthc,chv->bthv', o_lat, w_uv).reshape(B, T, MLA_HEADS * MLA_V)
    return o @ w_out


def mla_prompt(xn, w_in, q_norm, kv_norm, w_q_up, w_uk, w_uv, w_out):
    B, S, _ = xn.shape
    pos = jnp.arange(S)
    q_cat, kv_row = mla_project(xn, pos, w_in, q_norm, kv_norm, w_q_up, w_uk)

    def block(i):
        qb = lax.dynamic_slice_in_dim(q_cat, i * Q_BLOCK, Q_BLOCK, axis=1)
        qp = i * Q_BLOCK + jnp.arange(Q_BLOCK)
        return latent_attend(qb, kv_row, pos[None, :] <= qp[:, None])

    o = lax.map(block, jnp.arange(S // Q_BLOCK))
    o = jnp.moveaxis(o, 0, 1).reshape(B, S, MLA_HEADS, MLA_KV_LORA)
    return mla_out(o, w_uv, w_out), kv_row


def mla_sample(xn, cache, layer, page_table, w_in, q_norm, kv_norm, w_q_up, w_uk, w_uv, w_out):
    DB, T, _ = xn.shape
    past = page_table.shape[1] * PAGE_SIZE
    pos = past + jnp.arange(T)
    q_cat, kv_row = mla_project(xn, pos, w_in, q_norm, kv_norm, w_q_up, w_uk)
    kv_past = cache[layer, page_table].reshape(DB, past, MLA_ROW)
    kv_all = jnp.concatenate([kv_past, kv_row], axis=1)
    k_pos = jnp.arange(past + T)
    o = latent_attend(q_cat, kv_all, k_pos[None, :] <= pos[:, None])
    return mla_out(o, w_uv, w_out), kv_row


def nsa_project(xn, pos, w_in, b_gate):
    B, T, _ = xn.shape
    hq = NSA_HEADS * NSA_DH
    gk = NSA_GROUPS * NSA_DH
    a = xn @ w_in
    q, kv_c, kv_s, kv_w, g = jnp.split(a, [hq, hq + 2 * gk, hq + 4 * gk, hq + 6 * gk], axis=-1)
    q = rope(q.reshape(B, T, NSA_HEADS, NSA_DH), pos).reshape(B, T, NSA_GROUPS, NSA_HPG, NSA_DH)

    def kv_rows(z):
        z = z.reshape(B, T, 2, NSA_GROUPS, NSA_DH)
        return jnp.stack([rope(z[:, :, 0], pos), z[:, :, 1]], axis=2)

    gates = jax.nn.sigmoid((g + b_gate).astype(jnp.float32)).astype(xn.dtype)
    gates = gates.reshape(B, T, NSA_GROUPS, NSA_HPG, 3)
    return q, kv_rows(kv_c), kv_rows(kv_s), kv_rows(kv_w), gates


def compress(kv, pe, w1, w2):
    B, L = kv.shape[:2]
    nc = L // CMP_STRIDE
    chunks = kv[:, : nc * CMP_STRIDE].reshape(B, nc, CMP_STRIDE, 2, NSA_GROUPS, NSA_DH)
    first = jnp.einsum('bjlcgd,cldh->bjcgh', chunks, w1[:, :CMP_STRIDE])
    second = jnp.einsum('bjlcgd,cldh->bjcgh', chunks, w1[:, CMP_STRIDE:])
    bias = jnp.einsum('cld,cldh->ch', pe, w1)
    h = jax.nn.gelu(first[:, :-1] + second[:, 1:] + bias[:, None, :])
    out = jnp.einsum('bjcgh,chd->bjcgd', h, w2)
    end = jnp.arange(nc - 1) * CMP_STRIDE + CMP_BLOCK - 1
    return out[:, :, 0], out[:, :, 1], end


def gqa_attend(q, k, v, mask):
    s = jnp.einsum('btghd,bsgd->btghs', q, k) * NSA_SCALE
    p = masked_softmax(s, mask[None, :, None, None, :])
    return jnp.einsum('btghs,bsgd->btghd', p.astype(v.dtype), v), p


def select_blocks(p_cmp, q_pos, n_slc):
    n_cmp = p_cmp.shape[-1]
    j0 = jnp.arange(n_cmp)[:, None] * CMP_STRIDE
    s0 = jnp.arange(n_slc)[None, :] * SLC_BLOCK
    cover = ((j0 < s0 + SLC_BLOCK) & (j0 + CMP_BLOCK > s0)).astype(jnp.float32)
    imp = jnp.einsum('btghn,ns->btgs', p_cmp, cover)
    blk = jnp.arange(n_slc)[None, :]
    cur = (q_pos // SLC_BLOCK)[:, None]
    forced = (blk == 0) | ((blk <= cur) & (blk > cur - N_LOCAL))
    valid = blk * SLC_BLOCK <= q_pos[:, None]
    score = jnp.where(valid[None, :, None, :], imp + FORCE_BONUS * forced[None, :, None, :], NEG_INF)
    _, idx = lax.top_k(score, min(N_SELECT, n_slc))
    return idx


def select_attend(q, kv_sel, q_pos, idx):
    B, T, G, n = idx.shape
    m = n * SLC_BLOCK
    k = kv_sel[..., 0, :].reshape(B, T, G, m, NSA_DH)
    v = kv_sel[..., 1, :].reshape(B, T, G, m, NSA_DH)
    k_pos = (idx[..., None] * SLC_BLOCK + jnp.arange(SLC_BLOCK)).reshape(B, T, G, m)
    s = jnp.einsum('btghd,btgmd->btghm', q, k) * NSA_SCALE
    mask = (k_pos <= q_pos[None, :, None, None])[:, :, :, None, :]
    p = masked_softmax(s, mask)
    return jnp.einsum('btghm,btgmd->btghd', p.astype(v.dtype), v)


def window_mask(qp, kp):
    d = qp[:, None] - kp[None, :]
    return (d >= 0) & (d < WINDOW) & (kp[None, :] >= 0)


def window_prompt(q, kv_w):
    B, S = q.shape[:2]
    kvp = jnp.pad(kv_w, ((0, 0), (WINDOW, 0), (0, 0), (0, 0), (0, 0)))
    span = WINDOW + Q_BLOCK

    def block(i):
        start = i * Q_BLOCK
        qb = lax.dynamic_slice_in_dim(q, start, Q_BLOCK, axis=1)
        kvb = lax.dynamic_slice_in_dim(kvp, start, span, axis=1)
        qp = start + jnp.arange(Q_BLOCK)
        kp = start - WINDOW + jnp.arange(span)
        return gqa_attend(qb, kvb[:, :, 0], kvb[:, :, 1], window_mask(qp, kp))[0]

    o = lax.map(block, jnp.arange(S // Q_BLOCK))
    return jnp.moveaxis(o, 0, 1).reshape(B, S, NSA_GROUPS, NSA_HPG, NSA_DH)


def pad_to_blocks(kv, n_blk):
    rows = n_blk * SLC_BLOCK
    kv = jnp.pad(kv, ((0, 0), (0, rows - kv.shape[1])) + ((0, 0),) * (kv.ndim - 2))
    return kv.reshape((kv.shape[0], n_blk, SLC_BLOCK) + kv.shape[2:])


def nsa_output(gates, o_cmp, o_slc, o_win, w_out):
    B, T = gates.shape[:2]
    o = gates[..., 0:1] * o_cmp + gates[..., 1:2] * o_slc + gates[..., 2:3] * o_win
    return o.reshape(B, T, NSA_HEADS * NSA_DH) @ w_out


def nsa_prompt(xn, w_in, b_gate, pe, w1, w2, w_out):
    B, S, _ = xn.shape
    pos = jnp.arange(S)
    q, kv_c, kv_s, kv_w, gates = nsa_project(xn, pos, w_in, b_gate)
    k_c, v_c, end = compress(kv_c, pe, w1, w2)
    o_cmp, p_cmp = gqa_attend(q, k_c, v_c, end[None, :] <= pos[:, None])
    n_slc = -(-S // SLC_BLOCK)
    idx = select_blocks(p_cmp, pos, n_slc)
    blocks = pad_to_blocks(kv_s, n_slc)
    b_i = jnp.arange(B)[:, None, None, None]
    g_i = jnp.arange(NSA_GROUPS)[None, None, :, None]

    def block(i):
        st = i * SLC_QBLOCK
        qb = lax.dynamic_slice_in_dim(q, st, SLC_QBLOCK, axis=1)
        ib = lax.dynamic_slice_in_dim(idx, st, SLC_QBLOCK, axis=1)
        kv_sel = blocks[b_i, ib, :, :, g_i, :]
        return select_attend(qb, kv_sel, st + jnp.arange(SLC_QBLOCK), ib)

    o_slc = lax.map(block, jnp.arange(S // SLC_QBLOCK))
    o_slc = jnp.moveaxis(o_slc, 0, 1).reshape(B, S, NSA_GROUPS, NSA_HPG, NSA_DH)
    o_win = window_prompt(q, kv_w)
    out = nsa_output(gates, o_cmp, o_slc, o_win, w_out)
    return out, kv_c, kv_s, kv_w[:, S - min(WINDOW, S):]


def nsa_sample(xn, cache_cmp, cache_slc, win_buf, layer, page_table, w_in, b_gate, pe, w1, w2, w_out):
    DB, T, _ = xn.shape
    past = page_table.shape[1] * PAGE_SIZE
    pos = past + jnp.arange(T)
    q, kv_c, kv_s, kv_w, gates = nsa_project(xn, pos, w_in, b_gate)
    past_c = cache_cmp[layer, page_table].reshape(DB, past, 2, NSA_GROUPS, NSA_DH)
    k_c, v_c, end = compress(jnp.concatenate([past_c, kv_c], axis=1), pe, w1, w2)
    o_cmp, p_cmp = gqa_attend(q, k_c, v_c, end[None, :] <= pos[:, None])
    n_slc = -(-(past + T) // SLC_BLOCK)
    idx = select_blocks(p_cmp, pos, n_slc)
    n_past_blk = past // SLC_BLOCK
    per_page = PAGE_SIZE // SLC_BLOCK
    tail = pad_to_blocks(kv_s, n_slc - n_past_blk)
    b_i = jnp.arange(DB)[:, None, None, None]
    g_i = jnp.arange(NSA_GROUPS)[None, None, :, None]
    sp = jnp.minimum(idx, n_past_blk - 1)
    page = page_table[b_i, sp // per_page]
    rows = (sp % per_page)[..., None] * SLC_BLOCK + jnp.arange(SLC_BLOCK)
    from_pool = cache_slc[layer, page[..., None], rows, :, g_i[..., None], :]
    st = jnp.clip(idx - n_past_blk, 0, tail.shape[1] - 1)
    from_tail = tail[b_i, st, :, :, g_i, :]
    kv_sel = jnp.where((idx < n_past_blk)[..., None, None, None], from_pool, from_tail)
    o_slc = select_attend(q, kv_sel, pos, idx)
    L_w = win_buf.shape[1]
    win_all = jnp.concatenate([win_buf, kv_w], axis=1)
    kp = past - L_w + jnp.arange(L_w + T)
    o_win = gqa_attend(q, win_all[:, :, 0], win_all[:, :, 1], window_mask(pos, kp))[0]
    out = nsa_output(gates, o_cmp, o_slc, o_win, w_out)
    new_len = min(WINDOW, L_w + T)
    return out, kv_c, kv_s, win_all[:, L_w + T - new_len:]


def conv_ffn(xn, prev, w_in, conv_w, conv_b, w_out):
    T = xn.shape[1]
    g, u = jnp.split(xn @ w_in, 2, axis=-1)
    gp = jnp.concatenate([prev, g], axis=1)
    gc = conv_b + conv_w[0] * gp[:, 0:T]
    for k in range(1, CONV_W):
        gc = gc + conv_w[k] * gp[:, k:k + T]
    y = (jax.nn.silu(gc) * u) @ w_out
    return y, gp[:, T:]


def setup_inputs(seed: int = 0) -> dict:
    key = jax.random.key(seed)
    k = jax.random.split(key, 32)

    def nrm(i, shape, scale):
        x = jax.random.normal(k[i], shape, jnp.float32)
        return x if scale == 1.0 else x * scale

    n_pages = PAST_LEN // PAGE_SIZE
    n_pool = (DEC_BATCH * n_pages * 5) // 4
    win_len = min(WINDOW, PAST_LEN)
    page_table = jax.random.permutation(k[28], n_pool)[: DEC_BATCH * n_pages]
    page_table = page_table.reshape(DEC_BATCH, n_pages).astype(jnp.int32)
    nsa_pool = (N_NSA_LAYERS, n_pool, PAGE_SIZE, 2, NSA_GROUPS, NSA_DH)
    return {
        'x_prompt': nrm(0, (BATCH, SEQ, D_MODEL), 1.0),
        'x_sample': nrm(1, (DEC_BATCH, DEC_SEQ, D_MODEL), 1.0),
        'cache_mla_kv': nrm(2, (N_MLA_LAYERS, n_pool, PAGE_SIZE, MLA_ROW), 1.0),
        'cache_nsa_cmp_kv': nrm(3, nsa_pool, 1.0),
        'cache_nsa_slc_kv': nrm(4, nsa_pool, 1.0),
        'state_nsa_win_kv': nrm(5, (N_NSA_LAYERS, DEC_BATCH, win_len, 2, NSA_GROUPS, NSA_DH), 1.0),
        'state_ffn_conv': nrm(6, (DEPTH, DEC_BATCH, CONV_W - 1, D_FF), 1.0),
        'page_table': page_table,
        'attn_norm': 1.0 + nrm(7, (DEPTH, D_MODEL), 0.02),
        'ffn_norm': 1.0 + nrm(8, (DEPTH, D_MODEL), 0.02),
        'final_norm': 1.0 + nrm(9, (D_MODEL,), 0.02),
        'mla_w_in': nrm(10, (N_MLA_LAYERS, D_MODEL, MLA_Q_LORA + MLA_KV_LORA + MLA_ROPE), D_MODEL ** -0.5),
        'mla_q_norm': 1.0 + nrm(11, (N_MLA_LAYERS, MLA_Q_LORA), 0.02),
        'mla_kv_norm': 1.0 + nrm(12, (N_MLA_LAYERS, MLA_KV_LORA), 0.02),
        'mla_w_q_up': nrm(13, (N_MLA_LAYERS, MLA_Q_LORA, MLA_HEADS * (MLA_NOPE + MLA_ROPE)), MLA_Q_LORA ** -0.5),
        'mla_w_uk': nrm(14, (N_MLA_LAYERS, MLA_KV_LORA, MLA_HEADS, MLA_NOPE), MLA_KV_LORA ** -0.5),
        'mla_w_uv': nrm(15, (N_MLA_LAYERS, MLA_KV_LORA, MLA_HEADS, MLA_V), MLA_KV_LORA ** -0.5),
        'mla_w_out': nrm(16, (N_MLA_LAYERS, MLA_HEADS * MLA_V, D_MODEL), (MLA_HEADS * MLA_V) ** -0.5),
        'nsa_w_in': nrm(17, (N_NSA_LAYERS, D_MODEL, NSA_IN), D_MODEL ** -0.5),
        'nsa_b_gate': nrm(18, (N_NSA_LAYERS, 3 * NSA_HEADS), 0.1),
        'nsa_cmp_pe': nrm(19, (N_NSA_LAYERS, 2, CMP_BLOCK, NSA_DH), 0.1),
        'nsa_cmp_w1': nrm(20, (N_NSA_LAYERS, 2, CMP_BLOCK, NSA_DH, CMP_HID), (CMP_BLOCK * NSA_DH) ** -0.5),
        'nsa_cmp_w2': nrm(21, (N_NSA_LAYERS, 2, CMP_HID, NSA_DH), CMP_HID ** -0.5),
        'nsa_w_out': nrm(22, (N_NSA_LAYERS, NSA_HEADS * NSA_DH, D_MODEL), (NSA_HEADS * NSA_DH) ** -0.5),
        'ffn_w_in': nrm(23, (DEPTH, D_MODEL, 2 * D_FF), D_MODEL ** -0.5),
        'ffn_conv_w': nrm(24, (DEPTH, CONV_W, D_FF), CONV_W ** -0.5),
        'ffn_conv_b': nrm(25, (DEPTH, D_FF), 0.02),
        'ffn_w_out': nrm(26, (DEPTH, D_FF, D_MODEL), D_FF ** -0.5),
    }


def reference(x_prompt, x_sample, cache_mla_kv, cache_nsa_cmp_kv, cache_nsa_slc_kv, state_nsa_win_kv,
              state_ffn_conv, page_table, attn_norm, ffn_norm, final_norm, mla_w_in, mla_q_norm,
              mla_kv_norm, mla_w_q_up, mla_w_uk, mla_w_uv, mla_w_out, nsa_w_in, nsa_b_gate, nsa_cmp_pe,
              nsa_cmp_w1, nsa_cmp_w2, nsa_w_out, ffn_w_in, ffn_conv_w, ffn_conv_b, ffn_w_out):
    xp, xs = x_prompt, x_sample
    mla_p, mla_s, cmp_p, cmp_s, slc_p, slc_s, win_p, win_s, conv_p, conv_s = [], [], [], [], [], [], [], [], [], []
    for i in range(DEPTH):
        l = i // N_MIXERS
        hp = rmsnorm(xp, attn_norm[i])
        hs = rmsnorm(xs, attn_norm[i])
        if i % N_MIXERS == 0:
            w = (mla_w_in[l], mla_q_norm[l], mla_kv_norm[l], mla_w_q_up[l], mla_w_uk[l], mla_w_uv[l], mla_w_out[l])
            op, rp = mla_prompt(hp, *w)
            osm, rs = mla_sample(hs, cache_mla_kv, l, page_table, *w)
            mla_p.append(rp)
            mla_s.append(rs)
        else:
            w = (nsa_w_in[l], nsa_b_gate[l], nsa_cmp_pe[l], nsa_cmp_w1[l], nsa_cmp_w2[l], nsa_w_out[l])
            op, cp, spr, wp = nsa_prompt(hp, *w)
            osm, cs, ssr, ws = nsa_sample(hs, cache_nsa_cmp_kv, cache_nsa_slc_kv, state_nsa_win_kv[l], l,
                                          page_table, *w)
            cmp_p.append(cp)
            cmp_s.append(cs)
            slc_p.append(spr)
            slc_s.append(ssr)
            win_p.append(wp)
            win_s.append(ws)
        xp = xp + op
        xs = xs + osm
        f = (ffn_w_in[i], ffn_conv_w[i], ffn_conv_b[i], ffn_w_out[i])
        hp = rmsnorm(xp, ffn_norm[i])
        hs = rmsnorm(xs, ffn_norm[i])
        zero_prev = jnp.zeros((hp.shape[0], CONV_W - 1, D_FF), hp.dtype)
        fp, cvp = conv_ffn(hp, zero_prev, *f)
        fs, cvs = conv_ffn(hs, state_ffn_conv[i], *f)
        xp = xp + fp
        xs = xs + fs
        conv_p.append(cvp)
        conv_s.append(cvs)
    y_prompt = rmsnorm(xp, final_norm)
    y_sample = rmsnorm(xs, final_norm)
    new_mla_kv_prompt = jnp.stack(mla_p)
    new_mla_kv_sample = jnp.stack(mla_s)
    new_cmp_kv_prompt = jnp.stack(cmp_p)
    new_cmp_kv_sample = jnp.stack(cmp_s)
    new_slc_kv_prompt = jnp.stack(slc_p)
    new_slc_kv_sample = jnp.stack(slc_s)
    new_win_kv_prompt = jnp.stack(win_p)
    new_win_kv_sample = jnp.stack(win_s)
    new_conv_prompt = jnp.stack(conv_p)
    new_conv_sample = jnp.stack(conv_s)
    return (y_prompt, y_sample, new_mla_kv_prompt, new_mla_kv_sample, new_cmp_kv_prompt, new_cmp_kv_sample,
            new_slc_kv_prompt, new_slc_kv_sample, new_win_kv_prompt, new_win_kv_sample,
            new_conv_prompt, new_conv_sample)
```

```python
import functools

import jax
import jax.numpy as jnp
from jax import lax
from jax.experimental import pallas as pl
from jax.experimental.pallas import tpu as pltpu

F32 = jnp.float32
BF16 = jnp.bfloat16

D_MODEL = 1024
EPS = 1e-6
ROPE_THETA = 10000.0
NEG = -1e30
PAGE = 128

MLA_HEADS = 16
MLA_Q_LORA = 384
MLA_KV_LORA = 256
MLA_NOPE = 64
MLA_ROPE = 32
MLA_V = 64
MLA_ROW = MLA_KV_LORA + MLA_ROPE
MLA_SCALE = (MLA_NOPE + MLA_ROPE) ** -0.5
MLA_QBLK = 128

NSA_HEADS = 16
NSA_GROUPS = 4
NSA_HPG = 4
NSA_DH = 64
NSA_KV = 2 * NSA_GROUPS * NSA_DH
CMP_STRIDE = 16
CMP_HID = 128
SLC_BLOCK = 64
N_SELECT = 16
WINDOW = 512
FORCE_BONUS = 1e4
NSA_SCALE = NSA_DH ** -0.5

D_FF = 2816
VMEM_LIMIT = 56 * 1024 * 1024

_NT = (((1,), (1,)), ((), ()))


def _cparams(sem):
    return pltpu.CompilerParams(dimension_semantics=sem, vmem_limit_bytes=VMEM_LIMIT)


def _rms_rows(x, g):
    ms = jnp.mean(x * x, axis=-1, keepdims=True)
    return x * lax.rsqrt(ms + EPS) * g


def _softmax_update(s, m, l):
    m_new = jnp.maximum(m, jnp.max(s, axis=-1, keepdims=True))
    alpha = jnp.exp(m - m_new)
    p = jnp.where(s > 0.5 * NEG, jnp.exp(s - m_new), 0.0)
    l_new = alpha * l + jnp.sum(p, axis=-1, keepdims=True)
    return p, alpha, m_new, l_new


def _safe_div(acc, l):
    return acc / jnp.where(l > 0, l, 1.0)


def _resid_matmul_kernel(a_ref, w_ref, r_ref, o_ref):
    o_ref[...] = r_ref[...] + jnp.dot(a_ref[...], w_ref[...], preferred_element_type=F32)


def resid_matmul(a, w, res, tm):
    n, k = a.shape
    m = w.shape[1]
    return pl.pallas_call(
        _resid_matmul_kernel,
        grid=(n // tm,),
        in_specs=[pl.BlockSpec((tm, k), lambda i: (i, 0)),
                  pl.BlockSpec((k, m), lambda i: (0, 0)),
                  pl.BlockSpec((tm, m), lambda i: (i, 0))],
        out_specs=pl.BlockSpec((tm, m), lambda i: (i, 0)),
        out_shape=jax.ShapeDtypeStruct((n, m), F32),
        compiler_params=_cparams(("parallel",)),
        name="resid_matmul",
    )(a, w, res)


def _mla_proj_kernel(x_ref, g_ref, wqa_ref, qn_ref, wqup_ref, wkvt_ref, kvn_ref, wx_ref,
                     c_ref, s1_ref, s2_ref, cost_ref, sint_ref, q_out, kvt_out, x_out, *, absorbed):
    xn = _rms_rows(x_ref[0], g_ref[...]).astype(BF16)
    qa = jnp.dot(xn, wqa_ref[...], preferred_element_type=F32)
    qn = _rms_rows(qa, qn_ref[...]).astype(BF16)
    q = jnp.dot(qn, wqup_ref[...], preferred_element_type=F32)
    cq, s1, s2 = c_ref[...], s1_ref[...], s2_ref[...]
    for h in range(MLA_HEADS):
        blk = q[:, h * MLA_QBLK:(h + 1) * MLA_QBLK]
        blk = blk * cq + pltpu.roll(blk, 16, 1) * s1 + pltpu.roll(blk, MLA_QBLK - 16, 1) * s2
        blk = blk.astype(BF16)
        q_out[0, :, h * MLA_QBLK:(h + 1) * MLA_QBLK] = blk
        if absorbed:
            q_lat = jnp.dot(blk[:, MLA_ROPE:MLA_ROPE + MLA_NOPE], wx_ref[h], preferred_element_type=F32)
            x_out[0, h, :, :MLA_KV_LORA] = q_lat.astype(BF16)
            x_out[0, h, :, MLA_KV_LORA:] = blk
    kvt = lax.dot_general(wkvt_ref[...], xn, _NT, preferred_element_type=F32)
    c = kvt[:MLA_KV_LORA]
    ms = jnp.mean(c * c, axis=0, keepdims=True)
    cn = c * lax.rsqrt(ms + EPS) * kvn_ref[...]
    k1 = kvt[MLA_KV_LORA:MLA_KV_LORA + 16]
    k2 = kvt[MLA_KV_LORA + 16:MLA_ROW]
    ct, st = cost_ref[...], sint_ref[...]
    kvt_out[0, :MLA_KV_LORA] = cn
    kvt_out[0, MLA_KV_LORA:MLA_KV_LORA + 16] = k1 * ct - k2 * st
    kvt_out[0, MLA_KV_LORA + 16:] = k1 * st + k2 * ct
    if not absorbed:
        x_out[0] = jnp.dot(wx_ref[...], cn.astype(BF16), preferred_element_type=F32).astype(BF16)


def mla_proj(x3, g, w, tabs, tm, absorbed):
    nb, L, _ = x3.shape
    nl = L // tm
    hq = MLA_HEADS * MLA_QBLK
    if absorbed:
        x_shape = jax.ShapeDtypeStruct((nb, MLA_HEADS, L, MLA_KV_LORA + MLA_QBLK), BF16)
        x_spec = pl.BlockSpec((1, MLA_HEADS, tm, MLA_KV_LORA + MLA_QBLK), lambda b, i: (b, 0, i, 0))
        wx_spec = pl.BlockSpec((MLA_HEADS, MLA_NOPE, MLA_KV_LORA), lambda b, i: (0, 0, 0))
    else:
        x_shape = jax.ShapeDtypeStruct((nb, 2 * MLA_HEADS * MLA_NOPE, L), BF16)
        x_spec = pl.BlockSpec((1, 2 * MLA_HEADS * MLA_NOPE, tm), lambda b, i: (b, 0, i))
        wx_spec = pl.BlockSpec((2 * MLA_HEADS * MLA_NOPE, MLA_KV_LORA), lambda b, i: (0, 0))
    const = lambda b, i: (0, 0)
    return pl.pallas_call(
        functools.partial(_mla_proj_kernel, absorbed=absorbed),
        grid=(nb, nl),
        in_specs=[pl.BlockSpec((1, tm, D_MODEL), lambda b, i: (b, i, 0)),
                  pl.BlockSpec((1, D_MODEL), const),
                  pl.BlockSpec((D_MODEL, MLA_Q_LORA), const),
                  pl.BlockSpec((1, MLA_Q_LORA), const),
                  pl.BlockSpec((MLA_Q_LORA, hq), const),
                  pl.BlockSpec((MLA_ROW, D_MODEL), const),
                  pl.BlockSpec((MLA_KV_LORA, 1), const),
                  wx_spec,
                  pl.BlockSpec((tm, MLA_QBLK), lambda b, i: (i, 0)),
                  pl.BlockSpec((tm, MLA_QBLK), lambda b, i: (i, 0)),
                  pl.BlockSpec((tm, MLA_QBLK), lambda b, i: (i, 0)),
                  pl.BlockSpec((16, tm), lambda b, i: (0, i)),
                  pl.BlockSpec((16, tm), lambda b, i: (0, i))],
        out_specs=[pl.BlockSpec((1, tm, hq), lambda b, i: (b, i, 0)),
                   pl.BlockSpec((1, MLA_ROW, tm), lambda b, i: (b, 0, i)),
                   x_spec],
        out_shape=[jax.ShapeDtypeStruct((nb, L, hq), BF16),
                   jax.ShapeDtypeStruct((nb, MLA_ROW, L), F32),
                   x_shape],
        compiler_params=_cparams(("parallel", "parallel")),
        name="mla_proj_abs" if absorbed else "mla_proj",
    )(x3, g, w["wqa"], w["qn"], w["wqup"], w["wkvt"], w["kvn"], w["wukT"] if absorbed else w["wukvT"],
      tabs["cq"], tabs["s1"], tabs["s2"], tabs["cosT"], tabs["sinT"])


def _mla_attn_prompt_kernel(q_ref, kn_ref, v_ref, kpe_ref, o_ref, kt_scr, *, tq, seq):
    qi = pl.program_id(2)

    @pl.when(qi == 0)
    def _():
        kpe = kpe_ref[0].astype(BF16)
        zeros = jnp.zeros((MLA_QBLK - MLA_ROPE - MLA_NOPE, seq), BF16)
        for hh in range(2):
            kt_scr[hh, 0:MLA_ROPE] = kpe
            kt_scr[hh, MLA_ROPE:MLA_ROPE + MLA_NOPE] = kn_ref[0, hh * MLA_NOPE:(hh + 1) * MLA_NOPE]
            kt_scr[hh, MLA_ROPE + MLA_NOPE:] = zeros

    row = lax.broadcasted_iota(jnp.int32, (tq, tq), 0)
    col = lax.broadcasted_iota(jnp.int32, (tq, tq), 1)
    causal = col <= row
    outs = []
    for hh in range(2):
        qh = q_ref[0, :, hh * MLA_QBLK:(hh + 1) * MLA_QBLK]

        def step(j, carry, masked, hh=hh, qh=qh):
            m, l, acc = carry
            off = pl.multiple_of(j * tq, tq)
            s = jnp.dot(qh, kt_scr[hh, :, pl.ds(off, tq)], preferred_element_type=F32)
            if masked:
                s = jnp.where(causal, s, NEG)
            p, alpha, m, l = _softmax_update(s, m, l)
            vt = v_ref[0, hh * MLA_V:(hh + 1) * MLA_V, pl.ds(off, tq)]
            acc = alpha * acc + lax.dot_general(p.astype(BF16), vt, _NT, preferred_element_type=F32)
            return m, l, acc

        init = (jnp.full((tq, 1), NEG, F32), jnp.zeros((tq, 1), F32), jnp.zeros((tq, MLA_V), F32))
        carry = lax.fori_loop(0, qi, functools.partial(step, masked=False), init)
        m, l, acc = step(qi, carry, True)
        outs.append(_safe_div(acc, l))
    o_ref[0] = jnp.concatenate(outs, axis=-1).astype(BF16)


def mla_attn_prompt(q, knv, kvt, tq):
    b, s, _ = q.shape
    hp = MLA_HEADS // 2
    return pl.pallas_call(
        functools.partial(_mla_attn_prompt_kernel, tq=tq, seq=s),
        grid=(b, hp, s // tq),
        in_specs=[pl.BlockSpec((1, tq, 2 * MLA_QBLK), lambda bi, h, i: (bi, i, h)),
                  pl.BlockSpec((1, 2 * MLA_NOPE, s), lambda bi, h, i: (bi, h, 0)),
                  pl.BlockSpec((1, 2 * MLA_V, s), lambda bi, h, i: (bi, hp + h, 0)),
                  pl.BlockSpec((1, MLA_ROPE, s), lambda bi, h, i: (bi, MLA_KV_LORA // MLA_ROPE, 0))],
        out_specs=pl.BlockSpec((1, tq, 2 * MLA_V), lambda bi, h, i: (bi, i, h)),
        out_shape=jax.ShapeDtypeStruct((b, s, MLA_HEADS * MLA_V), BF16),
        scratch_shapes=[pltpu.VMEM((2, MLA_QBLK, s), BF16)],
        compiler_params=_cparams(("parallel", "parallel", "arbitrary")),
        name="mla_attn_prompt",
    )(q, knv, knv, kvt)


def _mla_attn_sample_kernel(pt_ref, qc_ref, *refs, ppc, t_new):
    page_refs = refs[:ppc]
    kvn_ref, wuv_ref, o_ref, kt_scr, m_scr, l_scr, acc_scr = refs[ppc:]
    c = pl.program_id(1)
    rows = MLA_HEADS * t_new

    @pl.when(c == 0)
    def _():
        m_scr[...] = jnp.full((rows, 1), NEG, F32)
        l_scr[...] = jnp.zeros((rows, 1), F32)
        acc_scr[...] = jnp.zeros((rows, MLA_KV_LORA), F32)

    for i in range(ppc):
        kt_scr[:, i * PAGE:(i + 1) * PAGE] = page_refs[i][0, 0].astype(BF16)
    q = qc_ref[0][:, :MLA_ROW]
    s = jnp.dot(q, kt_scr[...], preferred_element_type=F32)
    p, alpha, m, l = _softmax_update(s, m_scr[...], l_scr[...])
    acc = alpha * acc_scr[...] + lax.dot_general(p.astype(BF16), kt_scr[:MLA_KV_LORA, :], _NT,
                                                 preferred_element_type=F32)
    m_scr[...] = m
    l_scr[...] = l
    acc_scr[...] = acc

    @pl.when(c == pl.num_programs(1) - 1)
    def _():
        kn = kvn_ref[0].astype(BF16)
        s2 = jnp.dot(q, kn, preferred_element_type=F32)
        tq = lax.broadcasted_iota(jnp.int32, s2.shape, 0) % t_new
        tk = lax.broadcasted_iota(jnp.int32, s2.shape, 1)
        s2 = jnp.where(tk <= tq, s2, NEG)
        p2, alpha2, m2, l2 = _softmax_update(s2, m, l)
        acc2 = alpha2 * acc + lax.dot_general(p2.astype(BF16), kn[:MLA_KV_LORA], _NT, preferred_element_type=F32)
        o_lat = _safe_div(acc2, l2).astype(BF16)
        outs = [jnp.dot(o_lat[h * t_new:(h + 1) * t_new], wuv_ref[h], preferred_element_type=F32)
                for h in range(MLA_HEADS)]
        o_ref[0] = jnp.concatenate(outs, axis=-1)


def mla_attn_sample(page_table, qc, cache_t, layer, kv_new, wuv, ppc):
    db, rows, _ = qc.shape
    t_new = rows // MLA_HEADS
    n_pages = page_table.shape[1]
    pt = page_table.reshape(-1)

    def page_spec(k):
        return pl.BlockSpec((1, 1, MLA_ROW, PAGE),
                            lambda b, c, pt_ref: (layer, pt_ref[b * n_pages + c * ppc + k], 0, 0))

    grid_spec = pltpu.PrefetchScalarGridSpec(
        num_scalar_prefetch=1,
        grid=(db, n_pages // ppc),
        in_specs=[pl.BlockSpec((1, rows, qc.shape[2]), lambda b, c, pt_ref: (b, 0, 0))]
                 + [page_spec(k) for k in range(ppc)]
                 + [pl.BlockSpec((1, MLA_ROW, PAGE), lambda b, c, pt_ref: (b, 0, 0)),
                    pl.BlockSpec((MLA_HEADS, MLA_KV_LORA, MLA_V), lambda b, c, pt_ref: (0, 0, 0))],
        out_specs=pl.BlockSpec((1, t_new, MLA_HEADS * MLA_V), lambda b, c, pt_ref: (b, 0, 0)),
        scratch_shapes=[pltpu.VMEM((MLA_ROW, ppc * PAGE), BF16),
                        pltpu.VMEM((rows, 1), F32), pltpu.VMEM((rows, 1), F32),
                        pltpu.VMEM((rows, MLA_KV_LORA), F32)],
    )
    return pl.pallas_call(
        functools.partial(_mla_attn_sample_kernel, ppc=ppc, t_new=t_new),
        grid_spec=grid_spec,
        out_shape=jax.ShapeDtypeStruct((db, t_new, MLA_HEADS * MLA_V), F32),
        compiler_params=_cparams(("parallel", "arbitrary")),
        name="mla_attn_sample",
    )(pt, qc, *([cache_t] * ppc), kv_new, wuv)


def _ffn_kernel(*refs, tm, tf, tiles_per_seq, sample, final):
    x_ref, g_ref, wg_ref, wu_ref, cw_ref, cb_ref, wo_ref = refs[:7]
    k = 7
    if sample:
        p1_ref, p2_ref = refs[k:k + 2]
        k += 2
    if final:
        gf_ref = refs[k]
        k += 1
    y_ref, cv_ref, xn_scr, gs_scr, acc_scr, carry_scr = refs[k:]
    i = pl.program_id(0)
    j = pl.program_id(1)

    @pl.when(j == 0)
    def _():
        xn_scr[...] = _rms_rows(x_ref[...], g_ref[...]).astype(BF16)
        acc_scr[...] = jnp.zeros_like(acc_scr)

    xn = xn_scr[...]
    g = jnp.dot(xn, wg_ref[...], preferred_element_type=F32)
    u = jnp.dot(xn, wu_ref[...], preferred_element_type=F32)
    gs_scr[8:8 + tm] = g
    if sample:
        gs_scr[0:8] = jnp.zeros((8, tf), F32)
        rin = lax.broadcasted_iota(jnp.int32, (tm, 1), 0) % 8
        g1 = jnp.where(rin < 1, p1_ref[...], gs_scr[7:7 + tm])
        g2 = jnp.where(rin < 2, p2_ref[...], gs_scr[6:6 + tm])
        cv_ref[...] = g
    else:
        @pl.when((i % tiles_per_seq) == 0)
        def _():
            carry_scr[j] = jnp.zeros((8, tf), F32)

        gs_scr[0:8] = carry_scr[j]
        g1 = gs_scr[7:7 + tm]
        g2 = gs_scr[6:6 + tm]
        carry_scr[j] = gs_scr[tm:tm + 8]
        cv_ref[0] = g[tm - 2:tm]
    gc = cb_ref[...] + cw_ref[0:1] * g2 + cw_ref[1:2] * g1 + cw_ref[2:3] * g
    act = (gc * jax.nn.sigmoid(gc) * u).astype(BF16)
    acc_scr[...] += jnp.dot(act, wo_ref[...], preferred_element_type=F32)

    @pl.when(j == pl.num_programs(1) - 1)
    def _():
        y = x_ref[...] + acc_scr[...]
        if final:
            y = _rms_rows(y, gf_ref[...])
        y_ref[...] = y


def conv_ffn(x, g, w_in, cw, cb, w_out, *, tm, tf, rows_per_seq, prev=None, final_g=None):
    n = x.shape[0]
    nj = D_FF // tf
    sample = prev is not None
    final = final_g is not None
    const = lambda i, j: (0, 0)
    in_specs = [pl.BlockSpec((tm, D_MODEL), lambda i, j: (i, 0)),
                pl.BlockSpec((1, D_MODEL), const),
                pl.BlockSpec((D_MODEL, tf), lambda i, j: (0, j)),
                pl.BlockSpec((D_MODEL, tf), lambda i, j: (0, nj + j)),
                pl.BlockSpec((3, tf), lambda i, j: (0, j)),
                pl.BlockSpec((1, tf), lambda i, j: (0, j)),
                pl.BlockSpec((tf, D_MODEL), lambda i, j: (j, 0))]
    args = [x, g, w_in, w_in, cw, cb, w_out]
    if sample:
        in_specs += [pl.BlockSpec((tm, tf), lambda i, j: (i, j))] * 2
        args += list(prev)
        cv_shape = jax.ShapeDtypeStruct((n, D_FF), F32)
        cv_spec = pl.BlockSpec((tm, tf), lambda i, j: (i, j))
        tiles_per_seq = 1
    else:
        tiles_per_seq = rows_per_seq // tm
        cv_shape = jax.ShapeDtypeStruct((n // tm, 2, D_FF), F32)
        cv_spec = pl.BlockSpec((1, 2, tf), lambda i, j: (i, 0, j))
    if final:
        in_specs.append(pl.BlockSpec((1, D_MODEL), const))
        args.append(final_g)
    return pl.pallas_call(
        functools.partial(_ffn_kernel, tm=tm, tf=tf, tiles_per_seq=tiles_per_seq, sample=sample, final=final),
        grid=(n // tm, nj),
        in_specs=in_specs,
        out_specs=[pl.BlockSpec((tm, D_MODEL), lambda i, j: (i, 0)), cv_spec],
        out_shape=[jax.ShapeDtypeStruct((n, D_MODEL), F32), cv_shape],
        scratch_shapes=[pltpu.VMEM((tm, D_MODEL), BF16),
                        pltpu.VMEM((tm + 8, tf), F32),
                        pltpu.VMEM((tm, D_MODEL), F32),
                        pltpu.VMEM((nj, 8, tf), F32)],
        compiler_params=_cparams(("arbitrary", "arbitrary")),
        name="conv_ffn_sample" if sample else "conv_ffn",
    )(*args)


def _rope_rows(kvt, base, ct, st, out, half):
    for gg in range(NSA_GROUPS):
        r0 = base + gg * NSA_DH
        x1 = kvt[r0:r0 + half]
        x2 = kvt[r0 + half:r0 + NSA_DH]
        out[0, gg * NSA_DH:gg * NSA_DH + half] = x1 * ct - x2 * st
        out[0, gg * NSA_DH + half:(gg + 1) * NSA_DH] = x1 * st + x2 * ct


def _nsa_proj_kernel(x_ref, g_ref, wq_ref, wg_ref, bg_ref, wkvt_ref, cq_ref, s1_ref, s2_ref, cost_ref, sint_ref,
                     q_out, gate_out, kc_out, ks_out, kw_out):
    half = NSA_DH // 2
    xn = _rms_rows(x_ref[0], g_ref[...]).astype(BF16)
    q = jnp.dot(xn, wq_ref[...], preferred_element_type=F32)
    cq, s1, s2 = cq_ref[...], s1_ref[...], s2_ref[...]
    for v in range(NSA_HEADS * NSA_DH // 128):
        blk = q[:, v * 128:(v + 1) * 128]
        blk = blk * cq + pltpu.roll(blk, half, 1) * s1 + pltpu.roll(blk, 128 - half, 1) * s2
        q_out[0, :, v * 128:(v + 1) * 128] = blk.astype(BF16)
    gl = jnp.dot(xn, wg_ref[...], preferred_element_type=F32) + bg_ref[...]
    gate_out[0] = jax.nn.sigmoid(gl)
    kvt = lax.dot_general(wkvt_ref[...], xn, _NT, preferred_element_type=F32)
    ct, st = cost_ref[...], sint_ref[...]
    kw = NSA_GROUPS * NSA_DH
    for br, out in enumerate((kc_out, ks_out, kw_out)):
        _rope_rows(kvt, br * NSA_KV, ct, st, out, half)
        out[0, kw:] = kvt[br * NSA_KV + kw:(br + 1) * NSA_KV]


def nsa_proj(x3, g, w, tabs, tm):
    nb, L, _ = x3.shape
    const = lambda b, i: (0, 0)
    hq = NSA_HEADS * NSA_DH
    kv_spec = pl.BlockSpec((1, NSA_KV, tm), lambda b, i: (b, 0, i))
    kv_shape = jax.ShapeDtypeStruct((nb, NSA_KV, L), F32)
    return pl.pallas_call(
        _nsa_proj_kernel,
        grid=(nb, L // tm),
        in_specs=[pl.BlockSpec((1, tm, D_MODEL), lambda b, i: (b, i, 0)),
                  pl.BlockSpec((1, D_MODEL), const),
                  pl.BlockSpec((D_MODEL, hq), const),
                  pl.BlockSpec((D_MODEL, 128), const),
                  pl.BlockSpec((1, 128), const),
                  pl.BlockSpec((3 * NSA_KV, D_MODEL), const),
                  pl.BlockSpec((tm, 128), lambda b, i: (i, 0)),
                  pl.BlockSpec((tm, 128), lambda b, i: (i, 0)),
                  pl.BlockSpec((tm, 128), lambda b, i: (i, 0)),
                  pl.BlockSpec((NSA_DH // 2, tm), lambda b, i: (0, i)),
                  pl.BlockSpec((NSA_DH // 2, tm), lambda b, i: (0, i))],
        out_specs=[pl.BlockSpec((1, tm, hq), lambda b, i: (b, i, 0)),
                   pl.BlockSpec((1, tm, 128), lambda b, i: (b, i, 0)),
                   kv_spec, kv_spec, kv_spec],
        out_shape=[jax.ShapeDtypeStruct((nb, L, hq), BF16),
                   jax.ShapeDtypeStruct((nb, L, 128), F32),
                   kv_shape, kv_shape, kv_shape],
        compiler_params=_cparams(("parallel", "parallel")),
        name="nsa_proj",
    )(x3, g, w["wq"], w["wg"], w["bg"], w["wkvt"], tabs["cq"], tabs["s1"], tabs["s2"], tabs["cosT"], tabs["sinT"])


def _cmp_bias_kernel(pe_ref, w1_ref, o_ref):
    o_ref[0] = jnp.sum(w1_ref[0] * pe_ref[0], axis=0, keepdims=True)


def cmp_bias(pe_col, w1_flat):
    n = pe_col.shape[1]
    return pl.pallas_call(
        _cmp_bias_kernel,
        grid=(2,),
        in_specs=[pl.BlockSpec((1, n, 1), lambda c: (c, 0, 0)),
                  pl.BlockSpec((1, n, CMP_HID), lambda c: (c, 0, 0))],
        out_specs=pl.BlockSpec((1, 1, CMP_HID), lambda c: (c, 0, 0)),
        out_shape=jax.ShapeDtypeStruct((2, 1, CMP_HID), F32),
        name="cmp_bias",
    )(pe_col, w1_flat)


def _compress_kernel(*refs, npg, paged):
    if paged:
        refs = refs[1:]
    page_refs = refs[:npg]
    w1_ref, bias_ref, w2_ref, out_ref, xt_scr, fs_scr = refs[npg:]
    c = pl.program_id(1)
    nchunk = npg * PAGE // CMP_STRIDE
    nv = NSA_KV // 128
    for p in range(npg):
        for v in range(nv):
            page = page_refs[p][0, 0, v * 128:(v + 1) * 128] if paged else page_refs[p][0, v * 128:(v + 1) * 128]
            xt_scr[v, p * PAGE:(p + 1) * PAGE, :] = page.T
    for v in range(nv):
        cc = v // (nv // 2)
        accs = [jnp.zeros((nchunk, 2 * CMP_HID), F32) for _ in range(2)]
        for l in range(CMP_STRIDE):
            xl = xt_scr[v, pl.ds(l, nchunk, stride=CMP_STRIDE), :].astype(BF16)
            for u in range(2):
                accs[u] = accs[u] + jnp.dot(xl[:, u * NSA_DH:(u + 1) * NSA_DH], w1_ref[cc, l],
                                            preferred_element_type=F32)
        for u in range(2):
            cg = 2 * v + u
            acc = accs[u]

            @pl.when(c == 0)
            def _():
                fs_scr[cg, 0:8] = jnp.zeros((8, CMP_HID), F32)

            fs_scr[cg, 8:8 + nchunk] = acc[:, :CMP_HID]
            h = jax.nn.gelu(fs_scr[cg, 7:7 + nchunk] + acc[:, CMP_HID:] + bias_ref[cc])
            fs_scr[cg, 0:8] = fs_scr[cg, nchunk:nchunk + 8]
            out_ref[0, cg] = jnp.dot(h.astype(BF16), w2_ref[cc], preferred_element_type=F32).astype(BF16)


def compress(kv_src, w, *, npg=16, page_table=None, layer=None):
    paged = page_table is not None
    if paged:
        nb, n_pages = page_table.shape
        pt = page_table.reshape(-1)
        specs = [pl.BlockSpec((1, 1, NSA_KV, PAGE),
                              (lambda b, c, pt_ref, k=k: (layer, pt_ref[b * n_pages + c * npg + k], 0, 0)))
                 for k in range(npg)]
        imap = lambda f: (lambda b, c, pt_ref: f(b, c))
    else:
        nb = kv_src.shape[0]
        n_pages = kv_src.shape[2] // PAGE
        specs = [pl.BlockSpec((1, NSA_KV, PAGE), (lambda b, c, k=k: (b, 0, c * npg + k))) for k in range(npg)]
        imap = lambda f: f
    nchunk = npg * PAGE // CMP_STRIDE
    in_specs = specs + [pl.BlockSpec((2, CMP_STRIDE, NSA_DH, 2 * CMP_HID), imap(lambda b, c: (0, 0, 0, 0))),
                        pl.BlockSpec((2, 1, CMP_HID), imap(lambda b, c: (0, 0, 0))),
                        pl.BlockSpec((2, CMP_HID, NSA_DH), imap(lambda b, c: (0, 0, 0)))]
    out_spec = pl.BlockSpec((1, 2 * NSA_GROUPS, nchunk, NSA_DH), imap(lambda b, c: (b, 0, c, 0)))
    scratch = [pltpu.VMEM((NSA_KV // 128, npg * PAGE, 128), F32),
               pltpu.VMEM((2 * NSA_GROUPS, nchunk + 8, CMP_HID), F32)]
    grid = (nb, n_pages // npg)
    out_shape = jax.ShapeDtypeStruct((nb, 2 * NSA_GROUPS, n_pages * PAGE // CMP_STRIDE, NSA_DH), BF16)
    kern = functools.partial(_compress_kernel, npg=npg, paged=paged)
    args = [kv_src] * npg + [w["w1cat"], w["cbias"], w["w2"]]
    if paged:
        gs = pltpu.PrefetchScalarGridSpec(num_scalar_prefetch=1, grid=grid, in_specs=in_specs,
                                          out_specs=out_spec, scratch_shapes=scratch)
        return pl.pallas_call(kern, grid_spec=gs, out_shape=out_shape,
                              compiler_params=_cparams(("parallel", "arbitrary")), name="compress_paged")(pt, *args)
    return pl.pallas_call(kern, grid=grid, in_specs=in_specs, out_specs=out_spec, out_shape=out_shape,
                          scratch_shapes=scratch, compiler_params=_cparams(("parallel", "arbitrary")),
                          name="compress")(*args)


def _masked_softmax(s, valid):
    s = jnp.where(valid, s, NEG)
    m = jnp.max(s, axis=-1, keepdims=True)
    p = jnp.where(valid, jnp.exp(s - m), 0.0)
    den = jnp.sum(p, axis=-1, keepdims=True)
    return p / jnp.where(den > 0, den, 1.0)


def _nsa_attn_prompt_kernel(q_ref, gate_ref, kc_ref, vc_ref, kst_ref, vst_ref, kwt_ref, vwt_ref, cover_ref, e_ref,
                            o_ref, ks_scr, vs_scr, kw_scr, vw_scr, *, tq, tk, seq):
    i = pl.program_id(2)
    t0 = i * tq
    n_blk = seq // SLC_BLOCK

    @pl.when(i == 0)
    def _():
        ks_scr[...] = kst_ref[0].astype(BF16)
        vs_scr[...] = vst_ref[0].astype(BF16)
        kw_scr[...] = kwt_ref[0].astype(BF16)
        vw_scr[...] = vwt_ref[0].astype(BF16)

    q4 = jnp.concatenate([q_ref[0, :, h * NSA_DH:(h + 1) * NSA_DH] for h in range(NSA_HPG)], axis=0)
    rows = NSA_HPG * tq
    t_row = t0 + lax.broadcasted_iota(jnp.int32, (tq, 1), 0)
    t_row4 = t0 + lax.broadcasted_iota(jnp.int32, (rows, 1), 0) % tq

    ci = lax.broadcasted_iota(jnp.int32, (rows, kc_ref.shape[2]), 1)
    valid_c = (ci >= 1) & (CMP_STRIDE * ci + CMP_STRIDE - 1 <= t_row4)
    s_c = lax.dot_general(q4, kc_ref[0, 0], _NT, preferred_element_type=F32)
    p_c = _masked_softmax(s_c, valid_c)
    o_cmp = jnp.dot(p_c.astype(BF16), vc_ref[0, 0], preferred_element_type=F32)

    p_sum = p_c[0:tq]
    for h in range(1, NSA_HPG):
        p_sum = p_sum + p_c[h * tq:(h + 1) * tq]
    imp = jnp.dot(p_sum, cover_ref[...], preferred_element_type=F32, precision=lax.Precision.HIGHEST)
    lane = lax.broadcasted_iota(jnp.int32, (tq, 128), 1)
    blk = lane % n_blk
    cur = t_row // SLC_BLOCK
    forced = (blk == 0) | ((blk <= cur) & (blk > cur - 2))
    score = jnp.where(blk * SLC_BLOCK <= t_row, imp + jnp.where(forced, FORCE_BONUS, 0.0), NEG)
    cnt = jnp.zeros((tq, 128), F32)
    for k in range(1, n_blk):
        other = pltpu.roll(score, k, 1)
        beats = (other > score) | ((other == score) & (blk >= k))
        cnt = cnt + jnp.where(beats, 1.0, 0.0)
    sel = jnp.where(cnt < N_SELECT, 1.0, 0.0).astype(BF16)

    def attend(kt_scr, vt_scr, bias_fn, lo, hi):
        def step(j, carry):
            m, l, acc = carry
            off = pl.multiple_of(j * tk, tk)
            s = jnp.dot(q4, kt_scr[:, pl.ds(off, tk)], preferred_element_type=F32)
            kpos = off + lax.broadcasted_iota(jnp.int32, (tq, tk), 1)
            s = (s.reshape(NSA_HPG, tq, tk) + bias_fn(off, kpos)[None]).reshape(rows, tk)
            p, alpha, m, l = _softmax_update(s, m, l)
            acc = alpha * acc + lax.dot_general(p.astype(BF16), vt_scr[:, pl.ds(off, tk)], _NT,
                                                preferred_element_type=F32)
            return m, l, acc

        init = (jnp.full((rows, 1), NEG, F32), jnp.zeros((rows, 1), F32), jnp.zeros((rows, NSA_DH), F32))
        m, l, acc = lax.fori_loop(lo, hi, step, init)
        return _safe_div(acc, l)

    def slc_bias(off, kpos):
        chosen = jnp.dot(sel, e_ref[:, pl.ds(off, tk)], preferred_element_type=F32)
        return jnp.where((chosen > 0.5) & (kpos <= t_row), 0.0, NEG)

    def win_bias(off, kpos):
        d = t_row - kpos
        return jnp.where((d >= 0) & (d < WINDOW), 0.0, NEG)

    n_hi = (t0 + tq + tk - 1) // tk
    o_slc = attend(ks_scr, vs_scr, slc_bias, 0, n_hi)
    o_win = attend(kw_scr, vw_scr, win_bias, jnp.maximum(t0 - WINDOW + 1, 0) // tk, n_hi)

    outs = []
    gate = gate_ref[0, 0]
    for h in range(NSA_HPG):
        r = slice(h * tq, (h + 1) * tq)
        outs.append(gate[:, 3 * h:3 * h + 1] * o_cmp[r] + gate[:, 3 * h + 1:3 * h + 2] * o_slc[r]
                    + gate[:, 3 * h + 2:3 * h + 3] * o_win[r])
    o_ref[0] = jnp.concatenate(outs, axis=-1).astype(BF16)


def nsa_attn_prompt(q, gates_g, cmp_kv, kst, kwt, cover, e_mat, tq=128, tk=256):
    b, s, _ = q.shape
    g = NSA_GROUPS
    gw = NSA_HPG * NSA_DH
    ncmp = cmp_kv.shape[2]
    kt_spec = lambda off: pl.BlockSpec((1, NSA_DH, s), lambda bi, gi, i: (bi, off + gi, 0))
    return pl.pallas_call(
        functools.partial(_nsa_attn_prompt_kernel, tq=tq, tk=tk, seq=s),
        grid=(b, g, s // tq),
        in_specs=[pl.BlockSpec((1, tq, gw), lambda bi, gi, i: (bi, i, gi)),
                  pl.BlockSpec((1, 1, tq, 3 * NSA_HPG), lambda bi, gi, i: (bi, gi, i, 0)),
                  pl.BlockSpec((1, 1, ncmp, NSA_DH), lambda bi, gi, i: (bi, gi, 0, 0)),
                  pl.BlockSpec((1, 1, ncmp, NSA_DH), lambda bi, gi, i: (bi, g + gi, 0, 0)),
                  kt_spec(0), kt_spec(g), kt_spec(0), kt_spec(g),
                  pl.BlockSpec((ncmp, 128), lambda bi, gi, i: (0, 0)),
                  pl.BlockSpec((128, s), lambda bi, gi, i: (0, 0))],
        out_specs=pl.BlockSpec((1, tq, gw), lambda bi, gi, i: (bi, i, gi)),
        out_shape=jax.ShapeDtypeStruct((b, s, NSA_HEADS * NSA_DH), BF16),
        scratch_shapes=[pltpu.VMEM((NSA_DH, s), BF16)] * 4,
        compiler_params=_cparams(("parallel", "parallel", "arbitrary")),
        name="nsa_attn_prompt",
    )(q, gates_g, cmp_kv, cmp_kv, kst, kst, kwt, kwt, cover, e_mat)


def _nsa_attn_sample_kernel(pt_ref, q_ref, gate_ref, ckv_ref, cover_ref, *refs, ppc, t_new, past, n_slc_pad):
    page_refs = refs[:ppc]
    (ksn_ref, win_ref, kwn_ref, o_ref,
     kv_scr, idx_scr, ocmp_scr, m_scr, l_scr, acc_scr) = refs[ppc:]
    c = pl.program_id(1)
    g = NSA_GROUPS
    rg = NSA_HPG * t_new
    rows = g * rg
    kw = g * NSA_DH
    n_past_blk = past // SLC_BLOCK
    tt_g = lax.broadcasted_iota(jnp.int32, (rg, 1), 0) % t_new
    tt8 = lax.broadcasted_iota(jnp.int32, (t_new, 1), 0)

    @pl.when(c == 0)
    def _():
        imps = []
        for gg in range(g):
            qg = q_ref[0, gg * rg:(gg + 1) * rg]
            s = lax.dot_general(qg, ckv_ref[0, gg], _NT, preferred_element_type=F32)
            ci = lax.broadcasted_iota(jnp.int32, s.shape, 1)
            valid = (ci >= 1) & (CMP_STRIDE * ci + CMP_STRIDE - 1 <= past + tt_g)
            p = _masked_softmax(s, valid)
            ocmp_scr[gg * rg:(gg + 1) * rg] = jnp.dot(p.astype(BF16), ckv_ref[0, g + gg], preferred_element_type=F32)
            p_sum = p[0:t_new]
            for h in range(1, NSA_HPG):
                p_sum = p_sum + p[h * t_new:(h + 1) * t_new]
            imps.append(jnp.dot(p_sum, cover_ref[...], preferred_element_type=F32, precision=lax.Precision.HIGHEST))
        imp = jnp.concatenate(imps, axis=0)
        blk = lax.broadcasted_iota(jnp.int32, imp.shape, 1)
        tpos = past + lax.broadcasted_iota(jnp.int32, (g * t_new, 1), 0) % t_new
        cur = tpos // SLC_BLOCK
        forced = (blk == 0) | ((blk <= cur) & (blk > cur - 2))
        score = jnp.where(blk * SLC_BLOCK <= tpos, imp + jnp.where(forced, FORCE_BONUS, 0.0), NEG)
        n_slc = (past + t_new + SLC_BLOCK - 1) // SLC_BLOCK
        gone = jnp.float32(-3e38)
        score = jnp.where(blk < n_slc, score, gone)
        lane = lax.broadcasted_iota(jnp.int32, (g * t_new, 128), 1)
        idx = jnp.full((g * t_new, 128), -1, jnp.int32)
        for k in range(N_SELECT):
            mx = jnp.max(score, axis=-1, keepdims=True)
            pick = jnp.min(jnp.where(score == mx, blk, n_slc_pad), axis=-1, keepdims=True)
            idx = jnp.where(lane == k, pick, idx)
            score = jnp.where(blk == pick, gone, score)
        idx_scr[...] = idx
        m_scr[...] = jnp.full((rows, 1), NEG, F32)
        l_scr[...] = jnp.zeros((rows, 1), F32)
        acc_scr[...] = jnp.zeros((rows, NSA_DH), F32)

    def chosen_bias(gg, key_blk, extra):
        idx = idx_scr[gg * t_new:(gg + 1) * t_new]
        hit = key_blk == idx[:, 0:1]
        for k in range(1, N_SELECT):
            hit = hit | (key_blk == idx[:, k:k + 1])
        if extra is not None:
            hit = hit & extra
        return jnp.where(hit, 0.0, NEG)

    def update(gg, s, bias, vt):
        r = slice(gg * rg, (gg + 1) * rg)
        nk = s.shape[-1]
        s = (s.reshape(NSA_HPG, t_new, nk) + bias[None]).reshape(rg, nk)
        p, alpha, m, l = _softmax_update(s, m_scr[r], l_scr[r])
        acc_scr[r] = alpha * acc_scr[r] + lax.dot_general(p.astype(BF16), vt, _NT, preferred_element_type=F32)
        m_scr[r] = m
        l_scr[r] = l

    nk = ppc * PAGE
    for i in range(ppc):
        kv_scr[:, i * PAGE:(i + 1) * PAGE] = page_refs[i][0, 0].astype(BF16)
    key_blk = (c * nk + lax.broadcasted_iota(jnp.int32, (t_new, nk), 1)) // SLC_BLOCK
    for gg in range(g):
        qg = q_ref[0, gg * rg:(gg + 1) * rg]
        s = jnp.dot(qg, kv_scr[gg * NSA_DH:(gg + 1) * NSA_DH, :], preferred_element_type=F32)
        update(gg, s, chosen_bias(gg, key_blk, None), kv_scr[kw + gg * NSA_DH:kw + (gg + 1) * NSA_DH, :])

    @pl.when(c == pl.num_programs(1) - 1)
    def _():
        ksn = ksn_ref[0].astype(BF16)
        kn = lax.broadcasted_iota(jnp.int32, (t_new, PAGE), 1)
        tail_ok = (kn <= tt8) & (kn < t_new)
        for gg in range(g):
            qg = q_ref[0, gg * rg:(gg + 1) * rg]
            s = jnp.dot(qg, ksn[gg * NSA_DH:(gg + 1) * NSA_DH], preferred_element_type=F32)
            update(gg, s, chosen_bias(gg, n_past_blk + kn // SLC_BLOCK, tail_ok), ksn[kw + gg * NSA_DH:kw + (gg + 1) * NSA_DH])
        o_slc = _safe_div(acc_scr[...], l_scr[...])

        m_scr[...] = jnp.full((rows, 1), NEG, F32)
        l_scr[...] = jnp.zeros((rows, 1), F32)
        acc_scr[...] = jnp.zeros((rows, NSA_DH), F32)
        lw = win_ref.shape[3]
        wk = win_ref[0, 0].astype(BF16)
        kwn = kwn_ref[0].astype(BF16)
        wi = lax.broadcasted_iota(jnp.int32, (t_new, lw), 1)
        d_old = tt8 + lw - wi
        bias_old = jnp.where((d_old >= 0) & (d_old < WINDOW), 0.0, NEG)
        d_new = tt8 - kn
        bias_new = jnp.where((d_new >= 0) & (d_new < WINDOW) & (kn < t_new), 0.0, NEG)
        for gg in range(g):
            qg = q_ref[0, gg * rg:(gg + 1) * rg]
            ksl = slice(gg * NSA_DH, (gg + 1) * NSA_DH)
            vsl = slice(kw + gg * NSA_DH, kw + (gg + 1) * NSA_DH)
            update(gg, jnp.dot(qg, wk[ksl], preferred_element_type=F32), bias_old, wk[vsl])
            update(gg, jnp.dot(qg, kwn[ksl], preferred_element_type=F32), bias_new, kwn[vsl])
        o_win = _safe_div(acc_scr[...], l_scr[...])
        gate = gate_ref[0]
        o_ref[0] = gate[:, 0:1] * ocmp_scr[...] + gate[:, 1:2] * o_slc + gate[:, 2:3] * o_win


def nsa_attn_sample(page_table, q, gates, cmp_kv, cover, cache_t, layer, ks_new, win_t, kw_new, ppc):
    db, rows, _ = q.shape
    t_new = rows // NSA_HEADS
    n_pages = page_table.shape[1]
    past = n_pages * PAGE
    ncmp = cmp_kv.shape[2]
    n_slc_pad = cover.shape[1]
    lw = win_t.shape[3]
    pt = page_table.reshape(-1)

    def page_spec(k):
        return pl.BlockSpec((1, 1, NSA_KV, PAGE),
                            lambda b, c, pt_ref: (layer, pt_ref[b * n_pages + c * ppc + k], 0, 0))

    per_b = lambda *blk: pl.BlockSpec((1,) + blk, lambda b, c, pt_ref: (b,) + (0,) * len(blk))
    grid_spec = pltpu.PrefetchScalarGridSpec(
        num_scalar_prefetch=1,
        grid=(db, n_pages // ppc),
        in_specs=[per_b(rows, NSA_DH), per_b(rows, 3), per_b(2 * NSA_GROUPS, ncmp, NSA_DH),
                  pl.BlockSpec((ncmp, n_slc_pad), lambda b, c, pt_ref: (0, 0))]
                 + [page_spec(k) for k in range(ppc)]
                 + [per_b(NSA_KV, PAGE),
                    pl.BlockSpec((1, 1, NSA_KV, lw), lambda b, c, pt_ref: (layer, b, 0, 0)),
                    per_b(NSA_KV, PAGE)],
        out_specs=per_b(rows, NSA_DH),
        scratch_shapes=[pltpu.VMEM((NSA_KV, ppc * PAGE), BF16),
                        pltpu.VMEM((NSA_GROUPS * t_new, 128), jnp.int32),
                        pltpu.VMEM((rows, NSA_DH), F32),
                        pltpu.VMEM((rows, 1), F32), pltpu.VMEM((rows, 1), F32),
                        pltpu.VMEM((rows, NSA_DH), F32)],
    )
    return pl.pallas_call(
        functools.partial(_nsa_attn_sample_kernel, ppc=ppc, t_new=t_new, past=past, n_slc_pad=n_slc_pad),
        grid_spec=grid_spec,
        out_shape=jax.ShapeDtypeStruct((db, rows, NSA_DH), F32),
        compiler_params=_cparams(("parallel", "arbitrary")),
        name="nsa_attn_sample",
    )(pt, q, gates, cmp_kv, cover, *([cache_t] * ppc), ks_new, win_t, kw_new)


def _rope_cos_sin(pos, d):
    inv = ROPE_THETA ** (-jnp.arange(0, d, 2, dtype=F32) / d)
    ang = pos.astype(F32)[:, None] * inv[None, :]
    return jnp.cos(ang), jnp.sin(ang)


def _mla_tables(pos):
    cos, sin = _rope_cos_sin(pos, MLA_ROPE)
    n = pos.shape[0]
    z = lambda w: jnp.zeros((n, w), F32)
    cq = jnp.concatenate([cos, cos, jnp.ones((n, MLA_NOPE), F32), z(32)], axis=1) * MLA_SCALE
    s1 = jnp.concatenate([z(16), sin, z(96)], axis=1) * MLA_SCALE
    s2 = jnp.concatenate([-sin, z(112)], axis=1) * MLA_SCALE
    return dict(cq=cq, s1=s1, s2=s2, cosT=cos.T, sinT=sin.T)


def _mla_weights(w_in, q_norm, kv_norm, w_q_up, w_uk, w_uv, w_out):
    wq = w_q_up.reshape(MLA_Q_LORA, MLA_HEADS, MLA_NOPE + MLA_ROPE)
    wq = jnp.concatenate([wq[..., MLA_NOPE:], wq[..., :MLA_NOPE],
                          jnp.zeros((MLA_Q_LORA, MLA_HEADS, MLA_QBLK - MLA_NOPE - MLA_ROPE), F32)], axis=-1)
    uk_t = jnp.transpose(w_uk, (1, 2, 0))
    uv_t = jnp.transpose(w_uv, (1, 2, 0))
    return dict(
        wqa=w_in[:, :MLA_Q_LORA].astype(BF16),
        wkvt=w_in[:, MLA_Q_LORA:].T.astype(BF16),
        qn=q_norm.reshape(1, -1),
        kvn=kv_norm.reshape(-1, 1),
        wqup=wq.reshape(MLA_Q_LORA, MLA_HEADS * MLA_QBLK).astype(BF16),
        wukvT=jnp.concatenate([uk_t.reshape(-1, MLA_KV_LORA), uv_t.reshape(-1, MLA_KV_LORA)], axis=0).astype(BF16),
        wukT=uk_t.astype(BF16),
        wuv=jnp.transpose(w_uv, (1, 0, 2)).astype(BF16),
        wout=w_out.astype(BF16),
    )


def mla_layer_prompt(xp, g, w, tabs):
    b, s, _ = xp.shape
    q, kvt, knv = mla_proj(xp, g, w, tabs, tm=256, absorbed=False)
    o = mla_attn_prompt(q, knv, kvt, tq=256)
    x = resid_matmul(o.reshape(b * s, -1), w["wout"], xp.reshape(b * s, -1), tm=512)
    return x.reshape(b, s, -1), kvt


def mla_layer_sample(xs, g, w, tabs, cache_t, layer, page_table):
    db, t, _ = xs.shape
    n = db * t
    _, kvt, qc = mla_proj(xs.reshape(1, n, -1), g, w, tabs, tm=n, absorbed=True)
    qc = qc.reshape(MLA_HEADS, db, t, -1).transpose(1, 0, 2, 3).reshape(db, MLA_HEADS * t, -1)
    kv_new = kvt.reshape(MLA_ROW, db, t).transpose(1, 0, 2)
    kv_pad = jnp.pad(kv_new, ((0, 0), (0, 0), (0, PAGE - t)))
    o = mla_attn_sample(page_table, qc, cache_t, layer, kv_pad, w["wuv"], ppc=16)
    x = resid_matmul(o.reshape(n, -1).astype(BF16), w["wout"], xs.reshape(n, -1), tm=n)
    return x.reshape(db, t, -1), kv_new.transpose(0, 2, 1)


def ffn_layer_prompt(xp, g, w_in, cw, cb, w_out, final_g=None):
    b, s, _ = xp.shape
    y, cv = conv_ffn(xp.reshape(b * s, -1), g, w_in, cw, cb, w_out, tm=512, tf=D_FF // 2,
                     rows_per_seq=s, final_g=final_g)
    tiles_per_seq = s // 512
    return y.reshape(b, s, -1), cv[tiles_per_seq - 1::tiles_per_seq]


def ffn_layer_sample(xs, g, w_in, cw, cb, w_out, state, final_g=None):
    db, t, _ = xs.shape
    z = jnp.zeros((db, t - 2, D_FF), F32)
    p1 = jnp.concatenate([state[:, 1:2], jnp.zeros((db, t - 1, D_FF), F32)], axis=1).reshape(db * t, D_FF)
    p2 = jnp.concatenate([state, z], axis=1).reshape(db * t, D_FF)
    y, gate = conv_ffn(xs.reshape(db * t, -1), g, w_in, cw, cb, w_out, tm=db * t, tf=D_FF // 2,
                       rows_per_seq=t, prev=(p1, p2), final_g=final_g)
    return y.reshape(db, t, -1), gate.reshape(db, t, D_FF)[:, t - 2:]


def _nsa_tables(pos):
    cos, sin = _rope_cos_sin(pos, NSA_DH)
    z = jnp.zeros_like(sin)
    return dict(cq=jnp.concatenate([cos] * 4, axis=1) * NSA_SCALE,
                s1=jnp.concatenate([z, sin, z, sin], axis=1) * NSA_SCALE,
                s2=jnp.concatenate([-sin, z, -sin, z], axis=1) * NSA_SCALE,
                cosT=cos.T, sinT=sin.T)


def _nsa_weights(w_in, b_gate, pe, w1, w2, w_out):
    hq = NSA_HEADS * NSA_DH
    n_gate = 3 * NSA_HEADS
    half = w1.shape[1] // 2
    w1f = w1.reshape(2, -1, CMP_HID)
    return dict(
        wq=w_in[:, :hq].astype(BF16),
        wkvt=w_in[:, hq:hq + 3 * NSA_KV].T.astype(BF16),
        wg=jnp.pad(w_in[:, hq + 3 * NSA_KV:], ((0, 0), (0, 128 - n_gate))).astype(BF16),
        bg=jnp.pad(b_gate, (0, 128 - n_gate)).reshape(1, 128),
        w1cat=jnp.concatenate([w1[:, :half], w1[:, half:]], axis=-1).astype(BF16),
        cbias=cmp_bias(pe.reshape(2, -1, 1), w1f),
        w2=w2.astype(BF16),
        wout=w_out.astype(BF16),
    )


def _cover_matrix(n_entries, n_blk, n_cols, periodic):
    i = jnp.arange(n_entries)[:, None]
    col = jnp.arange(n_cols)[None, :]
    s = col % n_blk if periodic else col
    j0 = (i - 1) * CMP_STRIDE
    hit = (i >= 1) & (j0 < s * SLC_BLOCK + SLC_BLOCK) & (j0 + 2 * CMP_STRIDE > s * SLC_BLOCK)
    if not periodic:
        hit = hit & (col < n_blk)
    return hit.astype(F32)


def nsa_layer_prompt(xp, g, w, tabs):
    b, s, _ = xp.shape
    n_blk = s // SLC_BLOCK
    q, gates, kct, kst, kwt = nsa_proj(xp, g, w, tabs, tm=256)
    cmp_kv = compress(kct, w)
    gates_g = gates[..., :3 * NSA_HEADS].reshape(b, s, NSA_GROUPS, 3 * NSA_HPG).transpose(0, 2, 1, 3)
    cover = _cover_matrix(s // CMP_STRIDE, n_blk, 128, True)
    e_mat = ((jnp.arange(128)[:, None] == jnp.arange(s)[None, :] // SLC_BLOCK)).astype(BF16)
    o = nsa_attn_prompt(q, gates_g, cmp_kv, kst, kwt, cover, e_mat)
    x = resid_matmul(o.reshape(b * s, -1), w["wout"], xp.reshape(b * s, -1), tm=512)
    return x.reshape(b, s, -1), kct, kst, kwt


def nsa_layer_sample(xs, g, w, tabs, cache_cmp_t, cache_slc_t, win_t, layer, page_table):
    db, t, _ = xs.shape
    n = db * t
    past = page_table.shape[1] * PAGE
    q, gates, kct, kst, kwt = nsa_proj(xs.reshape(1, n, -1), g, w, tabs, tm=n)
    per_seq = lambda a: a.reshape(NSA_KV, db, t).transpose(1, 0, 2)
    kc_new, ks_new, kw_new = per_seq(kct), per_seq(kst), per_seq(kwt)
    pad = lambda a: jnp.pad(a, ((0, 0), (0, 0), (0, PAGE - t)))
    cmp_kv = compress(cache_cmp_t, w, page_table=page_table, layer=layer)
    rows = lambda a, k: a.reshape(db, t, NSA_GROUPS, NSA_HPG, k).transpose(0, 2, 3, 1, 4).reshape(db, NSA_HEADS * t, k)
    q_r = rows(q.reshape(n, -1), NSA_DH)
    gates_r = rows(gates.reshape(n, -1)[:, :3 * NSA_HEADS], 3)
    n_slc = -(-(past + t) // SLC_BLOCK)
    n_slc_pad = -(-n_slc // 128) * 128
    cover = _cover_matrix(past // CMP_STRIDE, n_slc, n_slc_pad, False)
    o = nsa_attn_sample(page_table, q_r, gates_r, cmp_kv, cover, cache_slc_t, layer, pad(ks_new), win_t,
                        pad(kw_new), ppc=16)
    o = o.reshape(db, NSA_GROUPS, NSA_HPG, t, NSA_DH).transpose(0, 3, 1, 2, 4).reshape(n, -1)
    x = resid_matmul(o.astype(BF16), w["wout"], xs.reshape(n, -1), tm=n)
    return x.reshape(db, t, -1), kc_new, ks_new, kw_new


def kernel(x_prompt, x_sample, cache_mla_kv, cache_nsa_cmp_kv, cache_nsa_slc_kv, state_nsa_win_kv, state_ffn_conv,
           page_table, attn_norm, ffn_norm, final_norm, mla_w_in, mla_q_norm, mla_kv_norm, mla_w_q_up, mla_w_uk,
           mla_w_uv, mla_w_out, nsa_w_in, nsa_b_gate, nsa_cmp_pe, nsa_cmp_w1, nsa_cmp_w2, nsa_w_out, ffn_w_in,
           ffn_conv_w, ffn_conv_b, ffn_w_out):
    b, s, _ = x_prompt.shape
    db, t, _ = x_sample.shape
    depth = attn_norm.shape[0]
    past = page_table.shape[1] * PAGE
    pos_p = jnp.arange(s)
    pos_s = past + jnp.arange(db * t) % t
    mla_t = jnp.transpose(cache_mla_kv, (0, 1, 3, 2))
    to_t = lambda c: jnp.transpose(c, (0, 1, 3, 4, 5, 2)).reshape(c.shape[0], c.shape[1], NSA_KV, c.shape[2])
    cmp_t, slc_t, win_t = to_t(cache_nsa_cmp_kv), to_t(cache_nsa_slc_kv), to_t(state_nsa_win_kv)
    tabs_mla_p, tabs_mla_s = _mla_tables(pos_p), _mla_tables(pos_s)
    tabs_nsa_p, tabs_nsa_s = _nsa_tables(pos_p), _nsa_tables(pos_s)

    xp, xs = x_prompt, x_sample
    outs = {k: [] for k in ("mla_p", "mla_s", "cmp_p", "cmp_s", "slc_p", "slc_s", "win_p", "win_s", "conv_p", "conv_s")}
    rows_t = lambda a: a.transpose(0, 2, 1)
    for i in range(depth):
        l = i // 2
        g_attn = attn_norm[i].reshape(1, -1)
        if i % 2 == 0:
            w = _mla_weights(mla_w_in[l], mla_q_norm[l], mla_kv_norm[l], mla_w_q_up[l], mla_w_uk[l], mla_w_uv[l],
                             mla_w_out[l])
            xp, kvt_p = mla_layer_prompt(xp, g_attn, w, tabs_mla_p)
            xs, kv_s = mla_layer_sample(xs, g_attn, w, tabs_mla_s, mla_t, l, page_table)
            outs["mla_p"].append(kvt_p)
            outs["mla_s"].append(kv_s)
        else:
            w = _nsa_weights(nsa_w_in[l], nsa_b_gate[l], nsa_cmp_pe[l], nsa_cmp_w1[l], nsa_cmp_w2[l], nsa_w_out[l])
            xp, kct, kst, kwt = nsa_layer_prompt(xp, g_attn, w, tabs_nsa_p)
            xs, kc_n, ks_n, kw_n = nsa_layer_sample(xs, g_attn, w, tabs_nsa_s, cmp_t, slc_t, win_t, l, page_table)
            outs["cmp_p"].append(kct)
            outs["slc_p"].append(kst)
            outs["win_p"].append(kwt[:, :, s - min(WINDOW, s):])
            outs["cmp_s"].append(rows_t(kc_n))
            outs["slc_s"].append(rows_t(ks_n))
            outs["win_s"].append(jnp.concatenate([win_t[l], kw_n], axis=2)[:, :, -WINDOW:])
        fin = final_norm.reshape(1, -1) if i == depth - 1 else None
        g_ffn = ffn_norm[i].reshape(1, -1)
        f = (ffn_w_in[i].astype(BF16), ffn_conv_w[i], ffn_conv_b[i].reshape(1, -1), ffn_w_out[i].astype(BF16))
        xp, cv_p = ffn_layer_prompt(xp, g_ffn, *f, final_g=fin)
        xs, cv_s = ffn_layer_sample(xs, g_ffn, *f, state_ffn_conv[i], final_g=fin)
        outs["conv_p"].append(cv_p)
        outs["conv_s"].append(cv_s)

    kv6 = lambda a: a.reshape(a.shape[:3] + (2, NSA_GROUPS, NSA_DH))
    feat_major = lambda lst: kv6(jnp.stack(lst).transpose(0, 1, 3, 2))
    return (xp, xs,
            jnp.stack(outs["mla_p"]).transpose(0, 1, 3, 2), jnp.stack(outs["mla_s"]),
            feat_major(outs["cmp_p"]), kv6(jnp.stack(outs["cmp_s"])),
            feat_major(outs["slc_p"]), kv6(jnp.stack(outs["slc_s"])),
            feat_major(outs["win_p"]), feat_major(outs["win_s"]),
            jnp.stack(outs["conv_p"]), jnp.stack(outs["conv_s"]))
```

```python
import functools

import jax
import jax.numpy as jnp
from jax import lax
from jax.experimental import pallas as pl
from jax.experimental.pallas import tpu as pltpu

F32 = jnp.float32
BF16 = jnp.bfloat16

D_MODEL = 1024
EPS = 1e-6
ROPE_THETA = 10000.0
NEG = -1e30
PAGE = 128

MLA_HEADS = 16
MLA_Q_LORA = 384
MLA_KV_LORA = 256
MLA_NOPE = 64
MLA_ROPE = 32
MLA_V = 64
MLA_ROW = MLA_KV_LORA + MLA_ROPE
MLA_SCALE = (MLA_NOPE + MLA_ROPE) ** -0.5
MLA_QBLK = 128

NSA_HEADS = 16
NSA_GROUPS = 4
NSA_HPG = 4
NSA_DH = 64
NSA_KV = 2 * NSA_GROUPS * NSA_DH
CMP_STRIDE = 16
CMP_PITCH = 24
CMP_HID = 128
SLC_BLOCK = 64
N_SELECT = 16
WINDOW = 512
FORCE_BONUS = 1e4
NSA_SCALE = NSA_DH ** -0.5

D_FF = 2816
FFN_SLAB = 256
VMEM_LIMIT = 56 * 1024 * 1024

_NT = (((1,), (1,)), ((), ()))


def _cparams(sem):
    return pltpu.CompilerParams(dimension_semantics=sem, vmem_limit_bytes=VMEM_LIMIT)


def _rms_rows(x, g):
    ms = jnp.mean(x * x, axis=-1, keepdims=True)
    return x * lax.rsqrt(ms + EPS) * g


def _softmax_update(s, m, l):
    m_new = jnp.maximum(m, jnp.max(s, axis=-1, keepdims=True))
    alpha = jnp.exp(m - m_new)
    p = jnp.exp(s - m_new)
    l_new = alpha * l + jnp.sum(p, axis=-1, keepdims=True)
    return p, alpha, m_new, l_new


def _safe_div(acc, l):
    return acc / jnp.where(l > 0, l, 1.0)


def _resid_matmul_kernel(a_ref, w_ref, r_ref, o_ref):
    o_ref[...] = r_ref[...] + jnp.dot(a_ref[...], w_ref[...], preferred_element_type=F32)


def resid_matmul(a, w, res, tm):
    n, k = a.shape
    m = w.shape[1]
    return pl.pallas_call(
        _resid_matmul_kernel,
        grid=(n // tm,),
        in_specs=[pl.BlockSpec((tm, k), lambda i: (i, 0)),
                  pl.BlockSpec((k, m), lambda i: (0, 0)),
                  pl.BlockSpec((tm, m), lambda i: (i, 0))],
        out_specs=pl.BlockSpec((tm, m), lambda i: (i, 0)),
        out_shape=jax.ShapeDtypeStruct((n, m), F32),
        compiler_params=_cparams(("parallel",)),
        name="resid_matmul",
    )(a, w, res)


def _mla_proj_kernel(x_ref, g_ref, wqa_ref, qn_ref, wqup_ref, wkvt_ref, kvn_ref, wx_ref,
                     c_ref, s1_ref, s2_ref, cost_ref, sint_ref, q_out, kvt_out, x_out, *, absorbed):
    xn = _rms_rows(x_ref[0], g_ref[...]).astype(BF16)
    qa = jnp.dot(xn, wqa_ref[...], preferred_element_type=F32)
    qn = _rms_rows(qa, qn_ref[...]).astype(BF16)
    q = jnp.dot(qn, wqup_ref[...], preferred_element_type=F32)
    cq, s1, s2 = c_ref[...], s1_ref[...], s2_ref[...]
    for h in range(MLA_HEADS):
        blk = q[:, h * MLA_QBLK:(h + 1) * MLA_QBLK]
        blk = blk * cq + pltpu.roll(blk, 16, 1) * s1 + pltpu.roll(blk, MLA_QBLK - 16, 1) * s2
        blk = blk.astype(BF16)
        q_out[0, :, h * MLA_QBLK:(h + 1) * MLA_QBLK] = blk
        if absorbed:
            q_lat = jnp.dot(blk[:, MLA_ROPE:MLA_ROPE + MLA_NOPE], wx_ref[h], preferred_element_type=F32)
            x_out[0, h, :, :MLA_KV_LORA] = q_lat.astype(BF16)
            x_out[0, h, :, MLA_KV_LORA:] = blk
    kvt = lax.dot_general(wkvt_ref[...], xn, _NT, preferred_element_type=F32)
    c = kvt[:MLA_KV_LORA]
    ms = jnp.mean(c * c, axis=0, keepdims=True)
    cn = c * lax.rsqrt(ms + EPS) * kvn_ref[...]
    k1 = kvt[MLA_KV_LORA:MLA_KV_LORA + 16]
    k2 = kvt[MLA_KV_LORA + 16:MLA_ROW]
    ct, st = cost_ref[...], sint_ref[...]
    kvt_out[0, :MLA_KV_LORA] = cn
    kvt_out[0, MLA_KV_LORA:MLA_KV_LORA + 16] = k1 * ct - k2 * st
    kvt_out[0, MLA_KV_LORA + 16:] = k1 * st + k2 * ct
    if not absorbed:
        x_out[0] = jnp.dot(wx_ref[...], cn.astype(BF16), preferred_element_type=F32).astype(BF16)


def mla_proj(x3, g, w, tabs, tm, absorbed):
    nb, L, _ = x3.shape
    nl = L // tm
    hq = MLA_HEADS * MLA_QBLK
    if absorbed:
        x_shape = jax.ShapeDtypeStruct((nb, MLA_HEADS, L, MLA_KV_LORA + MLA_QBLK), BF16)
        x_spec = pl.BlockSpec((1, MLA_HEADS, tm, MLA_KV_LORA + MLA_QBLK), lambda b, i: (b, 0, i, 0))
        wx_spec = pl.BlockSpec((MLA_HEADS, MLA_NOPE, MLA_KV_LORA), lambda b, i: (0, 0, 0))
    else:
        x_shape = jax.ShapeDtypeStruct((nb, 2 * MLA_HEADS * MLA_NOPE, L), BF16)
        x_spec = pl.BlockSpec((1, 2 * MLA_HEADS * MLA_NOPE, tm), lambda b, i: (b, 0, i))
        wx_spec = pl.BlockSpec((2 * MLA_HEADS * MLA_NOPE, MLA_KV_LORA), lambda b, i: (0, 0))
    const = lambda b, i: (0, 0)
    return pl.pallas_call(
        functools.partial(_mla_proj_kernel, absorbed=absorbed),
        grid=(nb, nl),
        in_specs=[pl.BlockSpec((1, tm, D_MODEL), lambda b, i: (b, i, 0)),
                  pl.BlockSpec((1, D_MODEL), const),
                  pl.BlockSpec((D_MODEL, MLA_Q_LORA), const),
                  pl.BlockSpec((1, MLA_Q_LORA), const),
                  pl.BlockSpec((MLA_Q_LORA, hq), const),
                  pl.BlockSpec((MLA_ROW, D_MODEL), const),
                  pl.BlockSpec((MLA_KV_LORA, 1), const),
                  wx_spec,
                  pl.BlockSpec((tm, MLA_QBLK), lambda b, i: (i, 0)),
                  pl.BlockSpec((tm, MLA_QBLK), lambda b, i: (i, 0)),
                  pl.BlockSpec((tm, MLA_QBLK), lambda b, i: (i, 0)),
                  pl.BlockSpec((16, tm), lambda b, i: (0, i)),
                  pl.BlockSpec((16, tm), lambda b, i: (0, i))],
        out_specs=[pl.BlockSpec((1, tm, hq), lambda b, i: (b, i, 0)),
                   pl.BlockSpec((1, MLA_ROW, tm), lambda b, i: (b, 0, i)),
                   x_spec],
        out_shape=[jax.ShapeDtypeStruct((nb, L, hq), BF16),
                   jax.ShapeDtypeStruct((nb, MLA_ROW, L), F32),
                   x_shape],
        compiler_params=_cparams(("parallel", "parallel")),
        name="mla_proj_abs" if absorbed else "mla_proj",
    )(x3, g, w["wqa"], w["qn"], w["wqup"], w["wkvt"], w["kvn"], w["wukT"] if absorbed else w["wukvT"],
      tabs["cq"], tabs["s1"], tabs["s2"], tabs["cosT"], tabs["sinT"])


def _mla_attn_prompt_kernel(q_ref, kn_ref, v_ref, kpe_ref, o_ref, kt_scr, vt_scr, *, tq, seq):
    qi = pl.program_id(2)

    @pl.when(qi == 0)
    def _():
        kpe = kpe_ref[0].astype(BF16)
        zeros = jnp.zeros((MLA_QBLK - MLA_ROPE - MLA_NOPE, seq), BF16)
        ones_row = jnp.where(lax.broadcasted_iota(jnp.int32, (MLA_V, seq), 0) == 0, 1.0, 0.0).astype(BF16)
        for hh in range(2):
            kt_scr[hh, 0:MLA_ROPE] = kpe
            kt_scr[hh, MLA_ROPE:MLA_ROPE + MLA_NOPE] = kn_ref[0, hh * MLA_NOPE:(hh + 1) * MLA_NOPE]
            kt_scr[hh, MLA_ROPE + MLA_NOPE:] = zeros
            vt_scr[hh, 0:MLA_V] = v_ref[0, hh * MLA_V:(hh + 1) * MLA_V]
            vt_scr[hh, MLA_V:] = ones_row

    row = lax.broadcasted_iota(jnp.int32, (tq, tq), 0)
    col = lax.broadcasted_iota(jnp.int32, (tq, tq), 1)
    causal = col <= row
    qs = [q_ref[0, :, hh * MLA_QBLK:(hh + 1) * MLA_QBLK] for hh in range(2)]

    def step(j, carry, masked):
        off = pl.multiple_of(j * tq, tq)
        new = []
        for hh in range(2):
            m, acc = carry[hh]
            s = jnp.dot(qs[hh], kt_scr[hh, :, pl.ds(off, tq)], preferred_element_type=F32)
            if masked:
                s = jnp.where(causal, s, NEG)
            m_new = jnp.maximum(m, jnp.max(s, axis=-1, keepdims=True))
            p = jnp.exp(s - m_new).astype(BF16)
            acc = jnp.exp(m - m_new) * acc + lax.dot_general(p, vt_scr[hh, :, pl.ds(off, tq)], _NT,
                                                             preferred_element_type=F32)
            new.append((m_new, acc))
        return tuple(new)

    init = (jnp.full((tq, 1), NEG, F32), jnp.zeros((tq, 2 * MLA_V), F32))
    carry = lax.fori_loop(0, qi, functools.partial(step, masked=False), (init, init))
    carry = step(qi, carry, True)
    o_ref[0] = jnp.concatenate([acc[:, :MLA_V] / acc[:, MLA_V:MLA_V + 1] for _, acc in carry],
                               axis=-1).astype(BF16)


def mla_attn_prompt(q, knv, kvt, tq):
    b, s, _ = q.shape
    hp = MLA_HEADS // 2
    return pl.pallas_call(
        functools.partial(_mla_attn_prompt_kernel, tq=tq, seq=s),
        grid=(b, hp, s // tq),
        in_specs=[pl.BlockSpec((1, tq, 2 * MLA_QBLK), lambda bi, h, i: (bi, i, h)),
                  pl.BlockSpec((1, 2 * MLA_NOPE, s), lambda bi, h, i: (bi, h, 0)),
                  pl.BlockSpec((1, 2 * MLA_V, s), lambda bi, h, i: (bi, hp + h, 0)),
                  pl.BlockSpec((1, MLA_ROPE, s), lambda bi, h, i: (bi, MLA_KV_LORA // MLA_ROPE, 0))],
        out_specs=pl.BlockSpec((1, tq, 2 * MLA_V), lambda bi, h, i: (bi, i, h)),
        out_shape=jax.ShapeDtypeStruct((b, s, MLA_HEADS * MLA_V), BF16),
        scratch_shapes=[pltpu.VMEM((2, MLA_QBLK, s), BF16), pltpu.VMEM((2, 2 * MLA_V, s), BF16)],
        compiler_params=_cparams(("parallel", "parallel", "arbitrary")),
        name="mla_attn_prompt",
    )(q, knv, knv, kvt)


def _mla_attn_sample_kernel(pt_ref, qc_ref, *refs, ppc, t_new):
    page_refs = refs[:ppc]
    kvn_ref, wuv_ref, o_ref, kt_scr, m_scr, l_scr, acc_scr = refs[ppc:]
    c = pl.program_id(1)
    rows = MLA_HEADS * t_new

    @pl.when(c == 0)
    def _():
        m_scr[...] = jnp.full((rows, 1), NEG, F32)
        l_scr[...] = jnp.zeros((rows, 1), F32)
        acc_scr[...] = jnp.zeros((rows, MLA_KV_LORA), F32)

    for i in range(ppc):
        kt_scr[:, i * PAGE:(i + 1) * PAGE] = page_refs[i][0, 0].astype(BF16)
    q = qc_ref[0][:, :MLA_ROW]
    s = jnp.dot(q, kt_scr[...], preferred_element_type=F32)
    p, alpha, m, l = _softmax_update(s, m_scr[...], l_scr[...])
    acc = alpha * acc_scr[...] + lax.dot_general(p.astype(BF16), kt_scr[:MLA_KV_LORA, :], _NT,
                                                 preferred_element_type=F32)
    m_scr[...] = m
    l_scr[...] = l
    acc_scr[...] = acc

    @pl.when(c == pl.num_programs(1) - 1)
    def _():
        kn = kvn_ref[0].astype(BF16)
        s2 = jnp.dot(q, kn, preferred_element_type=F32)
        tq = lax.broadcasted_iota(jnp.int32, s2.shape, 0) % t_new
        tk = lax.broadcasted_iota(jnp.int32, s2.shape, 1)
        s2 = jnp.where(tk <= tq, s2, NEG)
        p2, alpha2, m2, l2 = _softmax_update(s2, m, l)
        acc2 = alpha2 * acc + lax.dot_general(p2.astype(BF16), kn[:MLA_KV_LORA], _NT, preferred_element_type=F32)
        o_lat = _safe_div(acc2, l2).astype(BF16)
        outs = [jnp.dot(o_lat[h * t_new:(h + 1) * t_new], wuv_ref[h], preferred_element_type=F32)
                for h in range(MLA_HEADS)]
        o_ref[0] = jnp.concatenate(outs, axis=-1)


def mla_attn_sample(page_table, qc, cache_t, layer, kv_new, wuv, ppc):
    db, rows, _ = qc.shape
    t_new = rows // MLA_HEADS
    n_pages = page_table.shape[1]
    pt = page_table.reshape(-1)

    def page_spec(k):
        return pl.BlockSpec((1, 1, MLA_ROW, PAGE),
                            lambda b, c, pt_ref: (layer, pt_ref[b * n_pages + c * ppc + k], 0, 0))

    grid_spec = pltpu.PrefetchScalarGridSpec(
        num_scalar_prefetch=1,
        grid=(db, n_pages // ppc),
        in_specs=[pl.BlockSpec((1, rows, qc.shape[2]), lambda b, c, pt_ref: (b, 0, 0))]
                 + [page_spec(k) for k in range(ppc)]
                 + [pl.BlockSpec((1, MLA_ROW, PAGE), lambda b, c, pt_ref: (b, 0, 0)),
                    pl.BlockSpec((MLA_HEADS, MLA_KV_LORA, MLA_V), lambda b, c, pt_ref: (0, 0, 0))],
        out_specs=pl.BlockSpec((1, t_new, MLA_HEADS * MLA_V), lambda b, c, pt_ref: (b, 0, 0)),
        scratch_shapes=[pltpu.VMEM((MLA_ROW, ppc * PAGE), BF16),
                        pltpu.VMEM((rows, 1), F32), pltpu.VMEM((rows, 1), F32),
                        pltpu.VMEM((rows, MLA_KV_LORA), F32)],
    )
    return pl.pallas_call(
        functools.partial(_mla_attn_sample_kernel, ppc=ppc, t_new=t_new),
        grid_spec=grid_spec,
        out_shape=jax.ShapeDtypeStruct((db, t_new, MLA_HEADS * MLA_V), F32),
        compiler_params=_cparams(("parallel", "arbitrary")),
        name="mla_attn_sample",
    )(pt, qc, *([cache_t] * ppc), kv_new, wuv)


def _ffn_kernel(*refs, tm, tf, tiles_per_seq, sample, final):
    x_ref, g_ref, wg_ref, wu_ref, cw_ref, cb_ref, wo_ref = refs[:7]
    k = 7
    if sample:
        p1_ref, p2_ref = refs[k:k + 2]
        k += 2
    if final:
        gf_ref = refs[k]
        k += 1
    y_ref, cv_ref, gs_scr, carry_scr = refs[k:]
    i = pl.program_id(0)
    ns = D_FF // tf

    if not sample:
        @pl.when((i % tiles_per_seq) == 0)
        def _():
            carry_scr[...] = jnp.zeros((ns, 8, tf), F32)

    x = x_ref[...]
    xn = _rms_rows(x, g_ref[...]).astype(BF16)
    y = x
    for cs in range(ns):
        cols = slice(cs * tf, (cs + 1) * tf)
        g = jnp.dot(xn, wg_ref[:, cols], preferred_element_type=F32)
        u = jnp.dot(xn, wu_ref[:, cols], preferred_element_type=F32)
        gs_scr[cs, 8:8 + tm] = g
        if sample:
            gs_scr[cs, 0:8] = jnp.zeros((8, tf), F32)
            rin = lax.broadcasted_iota(jnp.int32, (tm, 1), 0) % 8
            g1 = jnp.where(rin < 1, p1_ref[:, cols], gs_scr[cs, 7:7 + tm])
            g2 = jnp.where(rin < 2, p2_ref[:, cols], gs_scr[cs, 6:6 + tm])
            cv_ref[:, cols] = g
        else:
            gs_scr[cs, 0:8] = carry_scr[cs]
            g1 = gs_scr[cs, 7:7 + tm]
            g2 = gs_scr[cs, 6:6 + tm]
            carry_scr[cs] = gs_scr[cs, tm:tm + 8]
            cv_ref[0, :, cols] = g[tm - 2:tm]
        gc = cb_ref[:, cols] + cw_ref[0:1, cols] * g2 + cw_ref[1:2, cols] * g1 + cw_ref[2:3, cols] * g
        act = (gc * jax.nn.sigmoid(gc) * u).astype(BF16)
        y = y + jnp.dot(act, wo_ref[cols, :], preferred_element_type=F32)
    if final:
        y = _rms_rows(y, gf_ref[...])
    y_ref[...] = y


def conv_ffn(x, g, w_in, cw, cb, w_out, *, tm, tf, rows_per_seq, prev=None, final_g=None):
    n = x.shape[0]
    ns = D_FF // tf
    sample = prev is not None
    final = final_g is not None
    fixed = lambda shape, idx: pl.BlockSpec(shape, lambda i: idx, pipeline_mode=pl.Buffered(1))
    in_specs = [pl.BlockSpec((tm, D_MODEL), lambda i: (i, 0)),
                fixed((1, D_MODEL), (0, 0)),
                fixed((D_MODEL, D_FF), (0, 0)),
                fixed((D_MODEL, D_FF), (0, 1)),
                fixed((3, D_FF), (0, 0)),
                fixed((1, D_FF), (0, 0)),
                fixed((D_FF, D_MODEL), (0, 0))]
    args = [x, g, w_in, w_in, cw, cb, w_out]
    if sample:
        in_specs += [pl.BlockSpec((tm, D_FF), lambda i: (i, 0))] * 2
        args += list(prev)
        cv_shape = jax.ShapeDtypeStruct((n, D_FF), F32)
        cv_spec = pl.BlockSpec((tm, D_FF), lambda i: (i, 0))
        tiles_per_seq = 1
    else:
        tiles_per_seq = rows_per_seq // tm
        cv_shape = jax.ShapeDtypeStruct((n // tm, 2, D_FF), F32)
        cv_spec = pl.BlockSpec((1, 2, D_FF), lambda i: (i, 0, 0))
    if final:
        in_specs.append(fixed((1, D_MODEL), (0, 0)))
        args.append(final_g)
    return pl.pallas_call(
        functools.partial(_ffn_kernel, tm=tm, tf=tf, tiles_per_seq=tiles_per_seq, sample=sample, final=final),
        grid=(n // tm,),
        in_specs=in_specs,
        out_specs=[pl.BlockSpec((tm, D_MODEL), lambda i: (i, 0)), cv_spec],
        out_shape=[jax.ShapeDtypeStruct((n, D_MODEL), F32), cv_shape],
        scratch_shapes=[pltpu.VMEM((ns, tm + 8, tf), F32),
                        pltpu.VMEM((ns, 8, tf), F32)],
        compiler_params=_cparams(("arbitrary",)),
        name="conv_ffn_sample" if sample else "conv_ffn",
    )(*args)


def _rope_rows(kvt, base, ct, st, out, half):
    for gg in range(NSA_GROUPS):
        r0 = base + gg * NSA_DH
        x1 = kvt[r0:r0 + half]
        x2 = kvt[r0 + half:r0 + NSA_DH]
        out[0, gg * NSA_DH:gg * NSA_DH + half] = x1 * ct - x2 * st
        out[0, gg * NSA_DH + half:(gg + 1) * NSA_DH] = x1 * st + x2 * ct


def _nsa_proj_kernel(x_ref, g_ref, wq_ref, wg_ref, bg_ref, wkvt_ref, cq_ref, s1_ref, s2_ref, cost_ref, sint_ref,
                     q_out, gate_out, kc_out, ks_out, kw_out):
    half = NSA_DH // 2
    xn = _rms_rows(x_ref[0], g_ref[...]).astype(BF16)
    q = jnp.dot(xn, wq_ref[...], preferred_element_type=F32)
    cq, s1, s2 = cq_ref[...], s1_ref[...], s2_ref[...]
    for v in range(NSA_HEADS * NSA_DH // 128):
        blk = q[:, v * 128:(v + 1) * 128]
        blk = blk * cq + pltpu.roll(blk, half, 1) * s1 + pltpu.roll(blk, 128 - half, 1) * s2
        q_out[0, :, v * 128:(v + 1) * 128] = blk.astype(BF16)
    gl = jnp.dot(xn, wg_ref[...], preferred_element_type=F32) + bg_ref[...]
    gate_out[0] = jax.nn.sigmoid(gl)
    kvt = lax.dot_general(wkvt_ref[...], xn, _NT, preferred_element_type=F32)
    ct, st = cost_ref[...], sint_ref[...]
    kw = NSA_GROUPS * NSA_DH
    for br, out in enumerate((kc_out, ks_out, kw_out)):
        _rope_rows(kvt, br * NSA_KV, ct, st, out, half)
        out[0, kw:] = kvt[br * NSA_KV + kw:(br + 1) * NSA_KV]


def nsa_proj(x3, g, w, tabs, tm):
    nb, L, _ = x3.shape
    const = lambda b, i: (0, 0)
    hq = NSA_HEADS * NSA_DH
    kv_spec = pl.BlockSpec((1, NSA_KV, tm), lambda b, i: (b, 0, i))
    kv_shape = jax.ShapeDtypeStruct((nb, NSA_KV, L), F32)
    return pl.pallas_call(
        _nsa_proj_kernel,
        grid=(nb, L // tm),
        in_specs=[pl.BlockSpec((1, tm, D_MODEL), lambda b, i: (b, i, 0)),
                  pl.BlockSpec((1, D_MODEL), const),
                  pl.BlockSpec((D_MODEL, hq), const),
                  pl.BlockSpec((D_MODEL, 128), const),
                  pl.BlockSpec((1, 128), const),
                  pl.BlockSpec((3 * NSA_KV, D_MODEL), const),
                  pl.BlockSpec((tm, 128), lambda b, i: (i, 0)),
                  pl.BlockSpec((tm, 128), lambda b, i: (i, 0)),
                  pl.BlockSpec((tm, 128), lambda b, i: (i, 0)),
                  pl.BlockSpec((NSA_DH // 2, tm), lambda b, i: (0, i)),
                  pl.BlockSpec((NSA_DH // 2, tm), lambda b, i: (0, i))],
        out_specs=[pl.BlockSpec((1, tm, hq), lambda b, i: (b, i, 0)),
                   pl.BlockSpec((1, tm, 128), lambda b, i: (b, i, 0)),
                   kv_spec, kv_spec, kv_spec],
        out_shape=[jax.ShapeDtypeStruct((nb, L, hq), BF16),
                   jax.ShapeDtypeStruct((nb, L, 128), F32),
                   kv_shape, kv_shape, kv_shape],
        compiler_params=_cparams(("parallel", "parallel")),
        name="nsa_proj",
    )(x3, g, w["wq"], w["wg"], w["bg"], w["wkvt"], tabs["cq"], tabs["s1"], tabs["s2"], tabs["cosT"], tabs["sinT"])


def _cmp_bias_kernel(pe_ref, w1_ref, o_ref):
    o_ref[0] = jnp.sum(w1_ref[0] * pe_ref[0], axis=0, keepdims=True)


def cmp_bias(pe_col, w1_flat):
    n = pe_col.shape[1]
    return pl.pallas_call(
        _cmp_bias_kernel,
        grid=(2,),
        in_specs=[pl.BlockSpec((1, n, 1), lambda c: (c, 0, 0)),
                  pl.BlockSpec((1, n, CMP_HID), lambda c: (c, 0, 0))],
        out_specs=pl.BlockSpec((1, 1, CMP_HID), lambda c: (c, 0, 0)),
        out_shape=jax.ShapeDtypeStruct((2, 1, CMP_HID), F32),
        name="cmp_bias",
    )(pe_col, w1_flat)


def _compress_kernel(*refs, npg, paged):
    if paged:
        refs = refs[1:]
    page_refs = refs[:npg]
    nv = NSA_KV // 128
    w1_ref, bias_ref, w2_ref, out_ref = refs[npg:npg + 4]
    xt_scrs = refs[npg + 4:npg + 4 + nv]
    fs_scr = refs[npg + 4 + nv]
    c = pl.program_id(1)
    nchunk = npg * PAGE // CMP_STRIDE
    cpp = PAGE // CMP_STRIDE

    @pl.when(c == 0)
    def _():
        fs_scr[:, 0:8] = jnp.zeros((2 * NSA_GROUPS, 8, CMP_HID), F32)

    for v in range(nv):
        cc = v // (nv // 2)
        xt_scr = xt_scrs[v]
        for p in range(npg):
            page = page_refs[p][0, 0, v * 128:(v + 1) * 128] if paged else page_refs[p][0, v * 128:(v + 1) * 128]
            page_t = page.T
            for j in range(cpp):
                r0 = (p * cpp + j) * CMP_PITCH
                xt_scr[r0:r0 + CMP_STRIDE, :] = page_t[j * CMP_STRIDE:(j + 1) * CMP_STRIDE]
        both = jnp.zeros((nchunk, 4 * CMP_HID), F32)
        for lp in range(CMP_STRIDE // 2):
            x2 = jnp.concatenate([xt_scr[pl.ds(2 * lp + i, nchunk, stride=CMP_PITCH), :] for i in range(2)],
                                 axis=1).astype(BF16)
            both = both + jnp.dot(x2, w1_ref[cc, lp], preferred_element_type=F32)
        for u in range(2):
            cg = 2 * v + u
            acc = both[:, u * 2 * CMP_HID:(u + 1) * 2 * CMP_HID]
            fs_scr[cg, 8:8 + nchunk] = acc[:, :CMP_HID]
            h = jax.nn.gelu(fs_scr[cg, 7:7 + nchunk] + acc[:, CMP_HID:] + bias_ref[cc])
            fs_scr[cg, 0:8] = fs_scr[cg, nchunk:nchunk + 8]
            out_ref[0, cg] = jnp.dot(h.astype(BF16), w2_ref[cc], preferred_element_type=F32).astype(BF16)


def compress(kv_src, w, *, npg=16, page_table=None, layer=None):
    paged = page_table is not None
    if paged:
        nb, n_pages = page_table.shape
        pt = page_table.reshape(-1)
        specs = [pl.BlockSpec((1, 1, NSA_KV, PAGE),
                              (lambda b, c, pt_ref, k=k: (layer, pt_ref[b * n_pages + c * npg + k], 0, 0)))
                 for k in range(npg)]
        imap = lambda f: (lambda b, c, pt_ref: f(b, c))
    else:
        nb = kv_src.shape[0]
        n_pages = kv_src.shape[2] // PAGE
        specs = [pl.BlockSpec((1, NSA_KV, PAGE), (lambda b, c, k=k: (b, 0, c * npg + k))) for k in range(npg)]
        imap = lambda f: f
    nchunk = npg * PAGE // CMP_STRIDE
    in_specs = specs + [pl.BlockSpec((2, CMP_STRIDE // 2, 4 * NSA_DH, 4 * CMP_HID), imap(lambda b, c: (0, 0, 0, 0))),
                        pl.BlockSpec((2, 1, CMP_HID), imap(lambda b, c: (0, 0, 0))),
                        pl.BlockSpec((2, CMP_HID, NSA_DH), imap(lambda b, c: (0, 0, 0)))]
    out_spec = pl.BlockSpec((1, 2 * NSA_GROUPS, nchunk, NSA_DH), imap(lambda b, c: (b, 0, c, 0)))
    scratch = [pltpu.VMEM((nchunk * CMP_PITCH, 128), F32)] * (NSA_KV // 128) \
        + [pltpu.VMEM((2 * NSA_GROUPS, nchunk + 8, CMP_HID), F32)]
    grid = (nb, n_pages // npg)
    out_shape = jax.ShapeDtypeStruct((nb, 2 * NSA_GROUPS, n_pages * PAGE // CMP_STRIDE, NSA_DH), BF16)
    kern = functools.partial(_compress_kernel, npg=npg, paged=paged)
    args = [kv_src] * npg + [w["w1big"], w["cbias"], w["w2"]]
    if paged:
        gs = pltpu.PrefetchScalarGridSpec(num_scalar_prefetch=1, grid=grid, in_specs=in_specs,
                                          out_specs=out_spec, scratch_shapes=scratch)
        return pl.pallas_call(kern, grid_spec=gs, out_shape=out_shape,
                              compiler_params=_cparams(("parallel", "arbitrary")), name="compress_paged")(pt, *args)
    return pl.pallas_call(kern, grid=grid, in_specs=in_specs, out_specs=out_spec, out_shape=out_shape,
                          scratch_shapes=scratch, compiler_params=_cparams(("parallel", "arbitrary")),
                          name="compress")(*args)


def _masked_softmax(s, valid):
    s = jnp.where(valid, s, NEG)
    m = jnp.max(s, axis=-1, keepdims=True)
    p = jnp.where(valid, jnp.exp(s - m), 0.0)
    den = jnp.sum(p, axis=-1, keepdims=True)
    return p / jnp.where(den > 0, den, 1.0)


def _nsa_attn_prompt_kernel(q_ref, gate_ref, kc_ref, vc_ref, kst_ref, vst_ref, kwt_ref, vwt_ref, cover_ref, e_ref,
                            o_ref, ks_scr, vs_scr, kw_scr, vw_scr, *, tq, seq):
    i = pl.program_id(2)
    t0 = i * tq
    n_blk = seq // SLC_BLOCK

    @pl.when(i == 0)
    def _():
        ones_row = jnp.where(lax.broadcasted_iota(jnp.int32, (NSA_DH, seq), 0) == 0, 1.0, 0.0).astype(BF16)
        ks_scr[0:NSA_DH] = kst_ref[0].astype(BF16)
        ks_scr[NSA_DH:] = e_ref[...]
        vs_scr[0:NSA_DH] = vst_ref[0].astype(BF16)
        vs_scr[NSA_DH:] = ones_row
        kw_scr[...] = kwt_ref[0].astype(BF16)
        vw_scr[0:NSA_DH] = vwt_ref[0].astype(BF16)
        vw_scr[NSA_DH:] = ones_row

    q4 = jnp.concatenate([q_ref[0, :, h * NSA_DH:(h + 1) * NSA_DH] for h in range(NSA_HPG)], axis=0)
    rows = NSA_HPG * tq
    t_row = t0 + lax.broadcasted_iota(jnp.int32, (tq, 1), 0)
    t_row4 = t0 + lax.broadcasted_iota(jnp.int32, (rows, 1), 0) % tq

    ci = lax.broadcasted_iota(jnp.int32, (rows, kc_ref.shape[2]), 1)
    valid_c = (ci >= 1) & (CMP_STRIDE * ci + CMP_STRIDE - 1 <= t_row4)
    s_c = lax.dot_general(q4, kc_ref[0, 0], _NT, preferred_element_type=F32)
    p_c = _masked_softmax(s_c, valid_c)
    o_cmp = jnp.dot(p_c.astype(BF16), vc_ref[0, 0], preferred_element_type=F32)

    p_sum = p_c[0:tq]
    for h in range(1, NSA_HPG):
        p_sum = p_sum + p_c[h * tq:(h + 1) * tq]
    imp = jnp.dot(p_sum, cover_ref[...], preferred_element_type=F32, precision=lax.Precision.HIGHEST)
    lane = lax.broadcasted_iota(jnp.int32, (tq, 128), 1)
    blk = lane % n_blk
    cur = t_row // SLC_BLOCK
    forced = (blk == 0) | ((blk <= cur) & (blk > cur - 2))
    score = jnp.where(blk * SLC_BLOCK <= t_row, imp + jnp.where(forced, FORCE_BONUS, 0.0), NEG)
    cnt = jnp.zeros((tq, 128), F32)
    for k in range(1, n_blk):
        other = pltpu.roll(score, k, 1)
        beats = (other > score) | ((other == score) & (blk >= k))
        cnt = cnt + jnp.where(beats, 1.0, 0.0)
    sel_bias = jnp.where(cnt[:, :n_blk] < N_SELECT, 0.0, NEG).astype(BF16)
    q_aug = jnp.concatenate([q4, jnp.concatenate([sel_bias] * NSA_HPG, axis=0)], axis=1)

    def weighted(p, vt):
        return lax.dot_general(p.astype(BF16), vt, _NT, preferred_element_type=F32)

    d_iota = lax.broadcasted_iota(jnp.int32, (tq, tq), 0) - lax.broadcasted_iota(jnp.int32, (tq, tq), 1)
    causal_bias = jnp.where(d_iota >= 0, 0.0, NEG)

    def slc_step(j, carry, diagonal):
        m, acc = carry
        off = pl.multiple_of(j * tq, tq)
        s = jnp.dot(q_aug, ks_scr[:, pl.ds(off, tq)], preferred_element_type=F32)
        if diagonal:
            s = (s.reshape(NSA_HPG, tq, tq) + causal_bias[None]).reshape(rows, tq)
        m_new = jnp.maximum(m, jnp.max(s, axis=-1, keepdims=True))
        alpha = jnp.exp(m - m_new)
        acc = alpha * acc + weighted(jnp.exp(s - m_new), vs_scr[:, pl.ds(off, tq)])
        return m_new, acc

    init = (jnp.full((rows, 1), NEG, F32), jnp.zeros((rows, 2 * NSA_DH), F32))
    carry = lax.fori_loop(0, i, functools.partial(slc_step, diagonal=False), init)
    _, acc_s = slc_step(i, carry, True)
    o_slc = acc_s[:, :NSA_DH] / acc_s[:, NSA_DH:NSA_DH + 1]

    span = WINDOW + tq
    w_off = pl.multiple_of(jnp.maximum(t0 - WINDOW, 0), tq)
    s_w = jnp.dot(q4, kw_scr[:, pl.ds(w_off, span)], preferred_element_type=F32)
    d = t_row - (w_off + lax.broadcasted_iota(jnp.int32, (tq, span), 1))
    bias_w = jnp.where((d >= 0) & (d < WINDOW), 0.0, NEG)
    s_w = (s_w.reshape(NSA_HPG, tq, span) + bias_w[None]).reshape(rows, span)
    acc_w = weighted(jnp.exp(s_w - jnp.max(s_w, axis=-1, keepdims=True)), vw_scr[:, pl.ds(w_off, span)])
    o_win = acc_w[:, :NSA_DH] / acc_w[:, NSA_DH:NSA_DH + 1]

    outs = []
    gate = gate_ref[0, 0]
    for h in range(NSA_HPG):
        r = slice(h * tq, (h + 1) * tq)
        outs.append(gate[:, 3 * h:3 * h + 1] * o_cmp[r] + gate[:, 3 * h + 1:3 * h + 2] * o_slc[r]
                    + gate[:, 3 * h + 2:3 * h + 3] * o_win[r])
    o_ref[0] = jnp.concatenate(outs, axis=-1).astype(BF16)


def nsa_attn_prompt(q, gates_g, cmp_kv, kst, kwt, cover, e_mat, tq=256):
    b, s, _ = q.shape
    g = NSA_GROUPS
    gw = NSA_HPG * NSA_DH
    ncmp = cmp_kv.shape[2]
    n_blk = s // SLC_BLOCK
    kt_spec = lambda off: pl.BlockSpec((1, NSA_DH, s), lambda bi, gi, i: (bi, off + gi, 0))
    return pl.pallas_call(
        functools.partial(_nsa_attn_prompt_kernel, tq=tq, seq=s),
        grid=(b, g, s // tq),
        in_specs=[pl.BlockSpec((1, tq, gw), lambda bi, gi, i: (bi, i, gi)),
                  pl.BlockSpec((1, 1, tq, 3 * NSA_HPG), lambda bi, gi, i: (bi, gi, i, 0)),
                  pl.BlockSpec((1, 1, ncmp, NSA_DH), lambda bi, gi, i: (bi, gi, 0, 0)),
                  pl.BlockSpec((1, 1, ncmp, NSA_DH), lambda bi, gi, i: (bi, g + gi, 0, 0)),
                  kt_spec(0), kt_spec(g), kt_spec(0), kt_spec(g),
                  pl.BlockSpec((ncmp, 128), lambda bi, gi, i: (0, 0)),
                  pl.BlockSpec((n_blk, s), lambda bi, gi, i: (0, 0))],
        out_specs=pl.BlockSpec((1, tq, gw), lambda bi, gi, i: (bi, i, gi)),
        out_shape=jax.ShapeDtypeStruct((b, s, NSA_HEADS * NSA_DH), BF16),
        scratch_shapes=[pltpu.VMEM((NSA_DH + n_blk, s), BF16), pltpu.VMEM((2 * NSA_DH, s), BF16),
                        pltpu.VMEM((NSA_DH, s), BF16), pltpu.VMEM((2 * NSA_DH, s), BF16)],
        compiler_params=_cparams(("parallel", "parallel", "arbitrary")),
        name="nsa_attn_prompt",
    )(q, gates_g, cmp_kv, cmp_kv, kst, kst, kwt, kwt, cover, e_mat)


def _nsa_attn_sample_kernel(pt_ref, q_ref, qbd_ref, gate_ref, ckv_ref, cover_ref, *refs, ppc, t_new, past,
                            n_slc_pad):
    page_refs = refs[:ppc]
    (ksn_ref, win_ref, kwn_ref, o_ref,
     kv_scr, idx_scr, ocmp_scr, m_scr, l_scr, acc_scr) = refs[ppc:]
    c = pl.program_id(1)
    g = NSA_GROUPS
    rg = NSA_HPG * t_new
    rows = g * rg
    kw = g * NSA_DH
    n_past_blk = past // SLC_BLOCK
    tt_g = lax.broadcasted_iota(jnp.int32, (rg, 1), 0) % t_new
    tt8 = lax.broadcasted_iota(jnp.int32, (t_new, 1), 0)

    @pl.when(c == 0)
    def _():
        imps = []
        for gg in range(g):
            qg = q_ref[0, gg * rg:(gg + 1) * rg]
            s = lax.dot_general(qg, ckv_ref[0, gg], _NT, preferred_element_type=F32)
            ci = lax.broadcasted_iota(jnp.int32, s.shape, 1)
            valid = (ci >= 1) & (CMP_STRIDE * ci + CMP_STRIDE - 1 <= past + tt_g)
            p = _masked_softmax(s, valid)
            ocmp_scr[gg * rg:(gg + 1) * rg] = jnp.dot(p.astype(BF16), ckv_ref[0, g + gg], preferred_element_type=F32)
            p_sum = p[0:t_new]
            for h in range(1, NSA_HPG):
                p_sum = p_sum + p[h * t_new:(h + 1) * t_new]
            imps.append(jnp.dot(p_sum, cover_ref[...], preferred_element_type=F32, precision=lax.Precision.HIGHEST))
        imp = jnp.concatenate(imps, axis=0)
        blk = lax.broadcasted_iota(jnp.int32, imp.shape, 1)
        tpos = past + lax.broadcasted_iota(jnp.int32, (g * t_new, 1), 0) % t_new
        cur = tpos // SLC_BLOCK
        forced = (blk == 0) | ((blk <= cur) & (blk > cur - 2))
        score = jnp.where(blk * SLC_BLOCK <= tpos, imp + jnp.where(forced, FORCE_BONUS, 0.0), NEG)
        n_slc = (past + t_new + SLC_BLOCK - 1) // SLC_BLOCK
        gone = jnp.float32(-3e38)
        score = jnp.where(blk < n_slc, score, gone)
        lane = lax.broadcasted_iota(jnp.int32, (g * t_new, 128), 1)
        idx = jnp.full((g * t_new, 128), -1, jnp.int32)
        for k in range(N_SELECT):
            mx = jnp.max(score, axis=-1, keepdims=True)
            pick = jnp.min(jnp.where(score == mx, blk, n_slc_pad), axis=-1, keepdims=True)
            idx = jnp.where(lane == k, pick, idx)
            score = jnp.where(blk == pick, gone, score)
        idx_scr[...] = idx
        m_scr[...] = jnp.full((rows, 1), NEG, F32)
        l_scr[...] = jnp.zeros((rows, 1), F32)
        acc_scr[...] = jnp.zeros((rows, kw), F32)

    def chosen_bias(key_blk, extra):
        idx = idx_scr[...]
        hit = key_blk == idx[:, 0:1]
        for k in range(1, N_SELECT):
            hit = hit | (key_blk == idx[:, k:k + 1])
        if extra is not None:
            hit = hit & extra
        return jnp.where(hit, 0.0, NEG).reshape(g, 1, t_new, key_blk.shape[-1])

    def update(s, bias4, vt):
        nk = s.shape[-1]
        s = (s.reshape(g, NSA_HPG, t_new, nk) + bias4).reshape(rows, nk)
        p, alpha, m, l = _softmax_update(s, m_scr[...], l_scr[...])
        acc_scr[...] = alpha * acc_scr[...] + lax.dot_general(p.astype(BF16), vt, _NT, preferred_element_type=F32)
        m_scr[...] = m
        l_scr[...] = l

    def own_group(acc):
        rgrp = lax.broadcasted_iota(jnp.int32, (rows, 1), 0) // rg
        out = jnp.zeros((rows, NSA_DH), F32)
        for gg in range(g):
            out = out + jnp.where(rgrp == gg, acc[:, gg * NSA_DH:(gg + 1) * NSA_DH], 0.0)
        return out

    nk = ppc * PAGE
    for i in range(ppc):
        kv_scr[:, i * PAGE:(i + 1) * PAGE] = page_refs[i][0, 0].astype(BF16)
    qbd = qbd_ref[0]
    key_blk = (c * nk + lax.broadcasted_iota(jnp.int32, (g * t_new, nk), 1)) // SLC_BLOCK
    update(jnp.dot(qbd, kv_scr[0:kw, :], preferred_element_type=F32), chosen_bias(key_blk, None), kv_scr[kw:, :])

    @pl.when(c == pl.num_programs(1) - 1)
    def _():
        ksn = ksn_ref[0].astype(BF16)
        kn = lax.broadcasted_iota(jnp.int32, (t_new, PAGE), 1)
        kn_g = lax.broadcasted_iota(jnp.int32, (g * t_new, PAGE), 1)
        tt_gt = lax.broadcasted_iota(jnp.int32, (g * t_new, 1), 0) % t_new
        tail_ok = (kn_g <= tt_gt) & (kn_g < t_new)
        update(jnp.dot(qbd, ksn[0:kw], preferred_element_type=F32),
               chosen_bias(n_past_blk + kn_g // SLC_BLOCK, tail_ok), ksn[kw:])
        o_slc = _safe_div(own_group(acc_scr[...]), l_scr[...])

        m_scr[...] = jnp.full((rows, 1), NEG, F32)
        l_scr[...] = jnp.zeros((rows, 1), F32)
        acc_scr[...] = jnp.zeros((rows, kw), F32)
        lw = win_ref.shape[3]
        wk = win_ref[0, 0].astype(BF16)
        kwn = kwn_ref[0].astype(BF16)
        wi = lax.broadcasted_iota(jnp.int32, (t_new, lw), 1)
        d_old = tt8 + lw - wi
        bias_old = jnp.where((d_old >= 0) & (d_old < WINDOW), 0.0, NEG)
        d_new = tt8 - kn
        bias_new = jnp.where((d_new >= 0) & (d_new < WINDOW) & (kn < t_new), 0.0, NEG)
        update(jnp.dot(qbd, wk[0:kw], preferred_element_type=F32), bias_old.reshape(1, 1, t_new, lw), wk[kw:])
        update(jnp.dot(qbd, kwn[0:kw], preferred_element_type=F32), bias_new.reshape(1, 1, t_new, PAGE), kwn[kw:])
        o_win = _safe_div(own_group(acc_scr[...]), l_scr[...])
        gate = gate_ref[0]
        o_ref[0] = gate[:, 0:1] * ocmp_scr[...] + gate[:, 1:2] * o_slc + gate[:, 2:3] * o_win


def nsa_attn_sample(page_table, q, qbd, gates, cmp_kv, cover, cache_t, layer, ks_new, win_t, kw_new, ppc):
    db, rows, _ = q.shape
    t_new = rows // NSA_HEADS
    n_pages = page_table.shape[1]
    past = n_pages * PAGE
    ncmp = cmp_kv.shape[2]
    n_slc_pad = cover.shape[1]
    lw = win_t.shape[3]
    pt = page_table.reshape(-1)

    def page_spec(k):
        return pl.BlockSpec((1, 1, NSA_KV, PAGE),
                            lambda b, c, pt_ref: (layer, pt_ref[b * n_pages + c * ppc + k], 0, 0))

    per_b = lambda *blk: pl.BlockSpec((1,) + blk, lambda b, c, pt_ref: (b,) + (0,) * len(blk))
    grid_spec = pltpu.PrefetchScalarGridSpec(
        num_scalar_prefetch=1,
        grid=(db, n_pages // ppc),
        in_specs=[per_b(rows, NSA_DH), per_b(rows, NSA_GROUPS * NSA_DH), per_b(rows, 3),
                  per_b(2 * NSA_GROUPS, ncmp, NSA_DH),
                  pl.BlockSpec((ncmp, n_slc_pad), lambda b, c, pt_ref: (0, 0))]
                 + [page_spec(k) for k in range(ppc)]
                 + [per_b(NSA_KV, PAGE),
                    pl.BlockSpec((1, 1, NSA_KV, lw), lambda b, c, pt_ref: (layer, b, 0, 0)),
                    per_b(NSA_KV, PAGE)],
        out_specs=per_b(rows, NSA_DH),
        scratch_shapes=[pltpu.VMEM((NSA_KV, ppc * PAGE), BF16),
                        pltpu.VMEM((NSA_GROUPS * t_new, 128), jnp.int32),
                        pltpu.VMEM((rows, NSA_DH), F32),
                        pltpu.VMEM((rows, 1), F32), pltpu.VMEM((rows, 1), F32),
                        pltpu.VMEM((rows, NSA_GROUPS * NSA_DH), F32)],
    )
    return pl.pallas_call(
        functools.partial(_nsa_attn_sample_kernel, ppc=ppc, t_new=t_new, past=past, n_slc_pad=n_slc_pad),
        grid_spec=grid_spec,
        out_shape=jax.ShapeDtypeStruct((db, rows, NSA_DH), F32),
        compiler_params=_cparams(("parallel", "arbitrary")),
        name="nsa_attn_sample",
    )(pt, q, qbd, gates, cmp_kv, cover, *([cache_t] * ppc), ks_new, win_t, kw_new)


def _rope_cos_sin(pos, d):
    inv = ROPE_THETA ** (-jnp.arange(0, d, 2, dtype=F32) / d)
    ang = pos.astype(F32)[:, None] * inv[None, :]
    return jnp.cos(ang), jnp.sin(ang)


def _mla_tables(pos):
    cos, sin = _rope_cos_sin(pos, MLA_ROPE)
    n = pos.shape[0]
    z = lambda w: jnp.zeros((n, w), F32)
    cq = jnp.concatenate([cos, cos, jnp.ones((n, MLA_NOPE), F32), z(32)], axis=1) * MLA_SCALE
    s1 = jnp.concatenate([z(16), sin, z(96)], axis=1) * MLA_SCALE
    s2 = jnp.concatenate([-sin, z(112)], axis=1) * MLA_SCALE
    return dict(cq=cq, s1=s1, s2=s2, cosT=cos.T, sinT=sin.T)


def _mla_weights(w_in, q_norm, kv_norm, w_q_up, w_uk, w_uv, w_out):
    wq = w_q_up.reshape(MLA_Q_LORA, MLA_HEADS, MLA_NOPE + MLA_ROPE)
    wq = jnp.concatenate([wq[..., MLA_NOPE:], wq[..., :MLA_NOPE],
                          jnp.zeros((MLA_Q_LORA, MLA_HEADS, MLA_QBLK - MLA_NOPE - MLA_ROPE), F32)], axis=-1)
    uk_t = jnp.transpose(w_uk, (1, 2, 0))
    uv_t = jnp.transpose(w_uv, (1, 2, 0))
    return dict(
        wqa=w_in[:, :MLA_Q_LORA].astype(BF16),
        wkvt=w_in[:, MLA_Q_LORA:].T.astype(BF16),
        qn=q_norm.reshape(1, -1),
        kvn=kv_norm.reshape(-1, 1),
        wqup=wq.reshape(MLA_Q_LORA, MLA_HEADS * MLA_QBLK).astype(BF16),
        wukvT=jnp.concatenate([uk_t.reshape(-1, MLA_KV_LORA), uv_t.reshape(-1, MLA_KV_LORA)], axis=0).astype(BF16),
        wukT=uk_t.astype(BF16),
        wuv=jnp.transpose(w_uv, (1, 0, 2)).astype(BF16),
        wout=w_out.astype(BF16),
    )


def mla_layer_prompt(xp, g, w, tabs):
    b, s, _ = xp.shape
    q, kvt, knv = mla_proj(xp, g, w, tabs, tm=256, absorbed=False)
    o = mla_attn_prompt(q, knv, kvt, tq=512)
    x = resid_matmul(o.reshape(b * s, -1), w["wout"], xp.reshape(b * s, -1), tm=512)
    return x.reshape(b, s, -1), kvt


def mla_layer_sample(xs, g, w, tabs, cache_t, layer, page_table):
    db, t, _ = xs.shape
    n = db * t
    _, kvt, qc = mla_proj(xs.reshape(1, n, -1), g, w, tabs, tm=n, absorbed=True)
    qc = qc.reshape(MLA_HEADS, db, t, -1).transpose(1, 0, 2, 3).reshape(db, MLA_HEADS * t, -1)
    kv_new = kvt.reshape(MLA_ROW, db, t).transpose(1, 0, 2)
    kv_pad = jnp.pad(kv_new, ((0, 0), (0, 0), (0, PAGE - t)))
    o = mla_attn_sample(page_table, qc, cache_t, layer, kv_pad, w["wuv"], ppc=32)
    x = resid_matmul(o.reshape(n, -1).astype(BF16), w["wout"], xs.reshape(n, -1), tm=n)
    return x.reshape(db, t, -1), kv_new.transpose(0, 2, 1)


def ffn_layer_prompt(xp, g, w_in, cw, cb, w_out, final_g=None):
    b, s, _ = xp.shape
    y, cv = conv_ffn(xp.reshape(b * s, -1), g, w_in, cw, cb, w_out, tm=512, tf=FFN_SLAB,
                     rows_per_seq=s, final_g=final_g)
    tiles_per_seq = s // 512
    return y.reshape(b, s, -1), cv[tiles_per_seq - 1::tiles_per_seq]


def ffn_layer_sample(xs, g, w_in, cw, cb, w_out, state, final_g=None):
    db, t, _ = xs.shape
    z = jnp.zeros((db, t - 2, D_FF), F32)
    p1 = jnp.concatenate([state[:, 1:2], jnp.zeros((db, t - 1, D_FF), F32)], axis=1).reshape(db * t, D_FF)
    p2 = jnp.concatenate([state, z], axis=1).reshape(db * t, D_FF)
    y, gate = conv_ffn(xs.reshape(db * t, -1), g, w_in, cw, cb, w_out, tm=db * t, tf=FFN_SLAB,
                       rows_per_seq=t, prev=(p1, p2), final_g=final_g)
    return y.reshape(db, t, -1), gate.reshape(db, t, D_FF)[:, t - 2:]


def _nsa_tables(pos):
    cos, sin = _rope_cos_sin(pos, NSA_DH)
    z = jnp.zeros_like(sin)
    return dict(cq=jnp.concatenate([cos] * 4, axis=1) * NSA_SCALE,
                s1=jnp.concatenate([z, sin, z, sin], axis=1) * NSA_SCALE,
                s2=jnp.concatenate([-sin, z, -sin, z], axis=1) * NSA_SCALE,
                cosT=cos.T, sinT=sin.T)


def _nsa_weights(w_in, b_gate, pe, w1, w2, w_out):
    hq = NSA_HEADS * NSA_DH
    n_gate = 3 * NSA_HEADS
    half = w1.shape[1] // 2
    w1f = w1.reshape(2, -1, CMP_HID)
    w1p = jnp.concatenate([w1[:, :half], w1[:, half:]], axis=-1).reshape(2, half // 2, 2, NSA_DH, 2 * CMP_HID)
    zero = jnp.zeros_like(w1p)
    w1big = jnp.stack([jnp.concatenate([w1p, zero], axis=-1), jnp.concatenate([zero, w1p], axis=-1)], axis=3)
    return dict(
        wq=w_in[:, :hq].astype(BF16),
        wkvt=w_in[:, hq:hq + 3 * NSA_KV].T.astype(BF16),
        wg=jnp.pad(w_in[:, hq + 3 * NSA_KV:], ((0, 0), (0, 128 - n_gate))).astype(BF16),
        bg=jnp.pad(b_gate, (0, 128 - n_gate)).reshape(1, 128),
        w1big=w1big.reshape(2, half // 2, 4 * NSA_DH, 4 * CMP_HID).astype(BF16),
        cbias=cmp_bias(pe.reshape(2, -1, 1), w1f),
        w2=w2.astype(BF16),
        wout=w_out.astype(BF16),
    )


def _cover_matrix(n_entries, n_blk, n_cols, periodic):
    i = jnp.arange(n_entries)[:, None]
    col = jnp.arange(n_cols)[None, :]
    s = col % n_blk if periodic else col
    j0 = (i - 1) * CMP_STRIDE
    hit = (i >= 1) & (j0 < s * SLC_BLOCK + SLC_BLOCK) & (j0 + 2 * CMP_STRIDE > s * SLC_BLOCK)
    if not periodic:
        hit = hit & (col < n_blk)
    return hit.astype(F32)


def nsa_layer_prompt(xp, g, w, tabs):
    b, s, _ = xp.shape
    n_blk = s // SLC_BLOCK
    q, gates, kct, kst, kwt = nsa_proj(xp, g, w, tabs, tm=256)
    cmp_kv = compress(kct, w)
    gates_g = gates[..., :3 * NSA_HEADS].reshape(b, s, NSA_GROUPS, 3 * NSA_HPG).transpose(0, 2, 1, 3)
    cover = _cover_matrix(s // CMP_STRIDE, n_blk, 128, True)
    e_mat = (jnp.arange(n_blk)[:, None] == jnp.arange(s)[None, :] // SLC_BLOCK).astype(BF16)
    o = nsa_attn_prompt(q, gates_g, cmp_kv, kst, kwt, cover, e_mat)
    x = resid_matmul(o.reshape(b * s, -1), w["wout"], xp.reshape(b * s, -1), tm=512)
    return x.reshape(b, s, -1), kct, kst, kwt


def nsa_layer_sample(xs, g, w, tabs, cache_cmp_t, cache_slc_t, win_t, layer, page_table):
    db, t, _ = xs.shape
    n = db * t
    past = page_table.shape[1] * PAGE
    q, gates, kct, kst, kwt = nsa_proj(xs.reshape(1, n, -1), g, w, tabs, tm=n)
    per_seq = lambda a: a.reshape(NSA_KV, db, t).transpose(1, 0, 2)
    kc_new, ks_new, kw_new = per_seq(kct), per_seq(kst), per_seq(kwt)
    pad = lambda a: jnp.pad(a, ((0, 0), (0, 0), (0, PAGE - t)))
    cmp_kv = compress(cache_cmp_t, w, page_table=page_table, layer=layer)
    rows = lambda a, k: a.reshape(db, t, NSA_GROUPS, NSA_HPG, k).transpose(0, 2, 3, 1, 4).reshape(db, NSA_HEADS * t, k)
    q_r = rows(q.reshape(n, -1), NSA_DH)
    gates_r = rows(gates.reshape(n, -1)[:, :3 * NSA_HEADS], 3)
    n_slc = -(-(past + t) // SLC_BLOCK)
    n_slc_pad = -(-n_slc // 128) * 128
    cover = _cover_matrix(past // CMP_STRIDE, n_slc, n_slc_pad, False)
    own = jnp.arange(NSA_HEADS * t) // (NSA_HPG * t)
    q_bd = q_r[:, :, None, :] * (own[:, None] == jnp.arange(NSA_GROUPS)[None, :]).astype(BF16)[None, :, :, None]
    q_bd = q_bd.reshape(db, NSA_HEADS * t, NSA_GROUPS * NSA_DH)
    o = nsa_attn_sample(page_table, q_r, q_bd, gates_r, cmp_kv, cover, cache_slc_t, layer, pad(ks_new), win_t,
                        pad(kw_new), ppc=32)
    o = o.reshape(db, NSA_GROUPS, NSA_HPG, t, NSA_DH).transpose(0, 3, 1, 2, 4).reshape(n, -1)
    x = resid_matmul(o.astype(BF16), w["wout"], xs.reshape(n, -1), tm=n)
    return x.reshape(db, t, -1), kc_new, ks_new, kw_new


def kernel(x_prompt, x_sample, cache_mla_kv, cache_nsa_cmp_kv, cache_nsa_slc_kv, state_nsa_win_kv, state_ffn_conv,
           page_table, attn_norm, ffn_norm, final_norm, mla_w_in, mla_q_norm, mla_kv_norm, mla_w_q_up, mla_w_uk,
           mla_w_uv, mla_w_out, nsa_w_in, nsa_b_gate, nsa_cmp_pe, nsa_cmp_w1, nsa_cmp_w2, nsa_w_out, ffn_w_in,
           ffn_conv_w, ffn_conv_b, ffn_w_out):
    b, s, _ = x_prompt.shape
    db, t, _ = x_sample.shape
    depth = attn_norm.shape[0]
    past = page_table.shape[1] * PAGE
    pos_p = jnp.arange(s)
    pos_s = past + jnp.arange(db * t) % t
    mla_t = jnp.transpose(cache_mla_kv, (0, 1, 3, 2))
    to_t = lambda c: jnp.transpose(c, (0, 1, 3, 4, 5, 2)).reshape(c.shape[0], c.shape[1], NSA_KV, c.shape[2])
    cmp_t, slc_t, win_t = to_t(cache_nsa_cmp_kv), to_t(cache_nsa_slc_kv), to_t(state_nsa_win_kv)
    tabs_mla_p, tabs_mla_s = _mla_tables(pos_p), _mla_tables(pos_s)
    tabs_nsa_p, tabs_nsa_s = _nsa_tables(pos_p), _nsa_tables(pos_s)

    xp, xs = x_prompt, x_sample
    outs = {k: [] for k in ("mla_p", "mla_s", "cmp_p", "cmp_s", "slc_p", "slc_s", "win_p", "win_s", "conv_p", "conv_s")}
    rows_t = lambda a: a.transpose(0, 2, 1)
    for i in range(depth):
        l = i // 2
        g_attn = attn_norm[i].reshape(1, -1)
        if i % 2 == 0:
            w = _mla_weights(mla_w_in[l], mla_q_norm[l], mla_kv_norm[l], mla_w_q_up[l], mla_w_uk[l], mla_w_uv[l],
                             mla_w_out[l])
            xp, kvt_p = mla_layer_prompt(xp, g_attn, w, tabs_mla_p)
            xs, kv_s = mla_layer_sample(xs, g_attn, w, tabs_mla_s, mla_t, l, page_table)
            outs["mla_p"].append(kvt_p)
            outs["mla_s"].append(kv_s)
        else:
            w = _nsa_weights(nsa_w_in[l], nsa_b_gate[l], nsa_cmp_pe[l], nsa_cmp_w1[l], nsa_cmp_w2[l], nsa_w_out[l])
            xp, kct, kst, kwt = nsa_layer_prompt(xp, g_attn, w, tabs_nsa_p)
            xs, kc_n, ks_n, kw_n = nsa_layer_sample(xs, g_attn, w, tabs_nsa_s, cmp_t, slc_t, win_t, l, page_table)
            outs["cmp_p"].append(kct)
            outs["slc_p"].append(kst)
            outs["win_p"].append(kwt[:, :, s - min(WINDOW, s):])
            outs["cmp_s"].append(rows_t(kc_n))
            outs["slc_s"].append(rows_t(ks_n))
            outs["win_s"].append(jnp.concatenate([win_t[l], kw_n], axis=2)[:, :, -WINDOW:])
        fin = final_norm.reshape(1, -1) if i == depth - 1 else None
        g_ffn = ffn_norm[i].reshape(1, -1)
        f = (ffn_w_in[i].astype(BF16), ffn_conv_w[i], ffn_conv_b[i].reshape(1, -1), ffn_w_out[i].astype(BF16))
        xp, cv_p = ffn_layer_prompt(xp, g_ffn, *f, final_g=fin)
        xs, cv_s = ffn_layer_sample(xs, g_ffn, *f, state_ffn_conv[i], final_g=fin)
        outs["conv_p"].append(cv_p)
        outs["conv_s"].append(cv_s)

    kv6 = lambda a: a.reshape(a.shape[:3] + (2, NSA_GROUPS, NSA_DH))
    feat_major = lambda lst: kv6(jnp.stack(lst).transpose(0, 1, 3, 2))
    return (xp, xs,
            jnp.stack(outs["mla_p"]).transpose(0, 1, 3, 2), jnp.stack(outs["mla_s"]),
            feat_major(outs["cmp_p"]), kv6(jnp.stack(outs["cmp_s"])),
            feat_major(outs["slc_p"]), kv6(jnp.stack(outs["slc_s"])),
            feat_major(outs["win_p"]), feat_major(outs["win_s"]),
            jnp.stack(outs["conv_p"]), jnp.stack(outs["conv_s"]))
```

```python
import functools

import jax
import jax.numpy as jnp
from jax import lax
from jax.experimental import pallas as pl
from jax.experimental.pallas import tpu as pltpu

F32 = jnp.float32
BF16 = jnp.bfloat16

D_MODEL = 1024
EPS = 1e-6
ROPE_THETA = 10000.0
NEG = -1e30
PAGE = 128

MLA_HEADS = 16
MLA_Q_LORA = 384
MLA_KV_LORA = 256
MLA_NOPE = 64
MLA_ROPE = 32
MLA_V = 64
MLA_ROW = MLA_KV_LORA + MLA_ROPE
MLA_SCALE = (MLA_NOPE + MLA_ROPE) ** -0.5
MLA_QBLK = 128

NSA_HEADS = 16
NSA_GROUPS = 4
NSA_HPG = 4
NSA_DH = 64
NSA_KV = 2 * NSA_GROUPS * NSA_DH
CMP_STRIDE = 16
CMP_PITCH = 24
CMP_HID = 128
SLC_BLOCK = 64
N_SELECT = 16
WINDOW = 512
FORCE_BONUS = 1e4
NSA_SCALE = NSA_DH ** -0.5

D_FF = 2816
FFN_SLAB = 256
VMEM_LIMIT = 56 * 1024 * 1024

LOG2E = 1.4426950408889634
MLA_QSCALE = MLA_SCALE * LOG2E
NSA_QSCALE = NSA_SCALE * LOG2E

_NT = (((1,), (1,)), ((), ()))


def _cparams(sem):
    return pltpu.CompilerParams(dimension_semantics=sem, vmem_limit_bytes=VMEM_LIMIT)


def _rms_rows(x, g):
    ms = jnp.mean(x * x, axis=-1, keepdims=True)
    return x * lax.rsqrt(ms + EPS) * g


def _softmax_update(s, m, l):
    m_new = jnp.maximum(m, jnp.max(s, axis=-1, keepdims=True))
    alpha = jnp.exp2(m - m_new)
    p = jnp.exp2(s - m_new)
    l_new = alpha * l + jnp.sum(p, axis=-1, keepdims=True)
    return p, alpha, m_new, l_new


def _safe_div(acc, l):
    return acc / jnp.where(l > 0, l, 1.0)


def _resid_matmul_kernel(a_ref, w_ref, r_ref, o_ref):
    o_ref[...] = r_ref[...] + jnp.dot(a_ref[...], w_ref[...], preferred_element_type=F32)


def resid_matmul(a, w, res, tm):
    n, k = a.shape
    m = w.shape[1]
    return pl.pallas_call(
        _resid_matmul_kernel,
        grid=(n // tm,),
        in_specs=[pl.BlockSpec((tm, k), lambda i: (i, 0)),
                  pl.BlockSpec((k, m), lambda i: (0, 0)),
                  pl.BlockSpec((tm, m), lambda i: (i, 0))],
        out_specs=pl.BlockSpec((tm, m), lambda i: (i, 0)),
        out_shape=jax.ShapeDtypeStruct((n, m), F32),
        compiler_params=_cparams(("parallel",)),
        name="resid_matmul",
    )(a, w, res)


def _mla_proj_kernel(x_ref, g_ref, wqa_ref, qn_ref, wqup_ref, wkvt_ref, kvn_ref, wx_ref,
                     c_ref, s1_ref, s2_ref, cost_ref, sint_ref, q_out, kvt_out, x_out, *, absorbed):
    xn = _rms_rows(x_ref[0], g_ref[...]).astype(BF16)
    qa = jnp.dot(xn, wqa_ref[...], preferred_element_type=F32)
    qn = _rms_rows(qa, qn_ref[...]).astype(BF16)
    q = jnp.dot(qn, wqup_ref[...], preferred_element_type=F32)
    cq, s1, s2 = c_ref[...], s1_ref[...], s2_ref[...]
    for h in range(MLA_HEADS):
        blk = q[:, h * MLA_QBLK:(h + 1) * MLA_QBLK]
        blk = blk * cq + pltpu.roll(blk, 16, 1) * s1 + pltpu.roll(blk, MLA_QBLK - 16, 1) * s2
        blk = blk.astype(BF16)
        q_out[0, :, h * MLA_QBLK:(h + 1) * MLA_QBLK] = blk
        if absorbed:
            q_lat = jnp.dot(blk[:, MLA_ROPE:MLA_ROPE + MLA_NOPE], wx_ref[h], preferred_element_type=F32)
            x_out[0, h, :, :MLA_KV_LORA] = q_lat.astype(BF16)
            x_out[0, h, :, MLA_KV_LORA:] = blk
    kvt = lax.dot_general(wkvt_ref[...], xn, _NT, preferred_element_type=F32)
    c = kvt[:MLA_KV_LORA]
    ms = jnp.mean(c * c, axis=0, keepdims=True)
    cn = c * lax.rsqrt(ms + EPS) * kvn_ref[...]
    k1 = kvt[MLA_KV_LORA:MLA_KV_LORA + 16]
    k2 = kvt[MLA_KV_LORA + 16:MLA_ROW]
    ct, st = cost_ref[...], sint_ref[...]
    kvt_out[0, :MLA_KV_LORA] = cn
    kvt_out[0, MLA_KV_LORA:MLA_KV_LORA + 16] = k1 * ct - k2 * st
    kvt_out[0, MLA_KV_LORA + 16:] = k1 * st + k2 * ct
    if not absorbed:
        x_out[0] = jnp.dot(wx_ref[...], cn.astype(BF16), preferred_element_type=F32).astype(BF16)


def mla_proj(x3, g, w, tabs, tm, absorbed):
    nb, L, _ = x3.shape
    nl = L // tm
    hq = MLA_HEADS * MLA_QBLK
    if absorbed:
        x_shape = jax.ShapeDtypeStruct((nb, MLA_HEADS, L, MLA_KV_LORA + MLA_QBLK), BF16)
        x_spec = pl.BlockSpec((1, MLA_HEADS, tm, MLA_KV_LORA + MLA_QBLK), lambda b, i: (b, 0, i, 0))
        wx_spec = pl.BlockSpec((MLA_HEADS, MLA_NOPE, MLA_KV_LORA), lambda b, i: (0, 0, 0))
    else:
        x_shape = jax.ShapeDtypeStruct((nb, 2 * MLA_HEADS * MLA_NOPE, L), BF16)
        x_spec = pl.BlockSpec((1, 2 * MLA_HEADS * MLA_NOPE, tm), lambda b, i: (b, 0, i))
        wx_spec = pl.BlockSpec((2 * MLA_HEADS * MLA_NOPE, MLA_KV_LORA), lambda b, i: (0, 0))
    const = lambda b, i: (0, 0)
    return pl.pallas_call(
        functools.partial(_mla_proj_kernel, absorbed=absorbed),
        grid=(nb, nl),
        in_specs=[pl.BlockSpec((1, tm, D_MODEL), lambda b, i: (b, i, 0)),
                  pl.BlockSpec((1, D_MODEL), const),
                  pl.BlockSpec((D_MODEL, MLA_Q_LORA), const),
                  pl.BlockSpec((1, MLA_Q_LORA), const),
                  pl.BlockSpec((MLA_Q_LORA, hq), const),
                  pl.BlockSpec((MLA_ROW, D_MODEL), const),
                  pl.BlockSpec((MLA_KV_LORA, 1), const),
                  wx_spec,
                  pl.BlockSpec((tm, MLA_QBLK), lambda b, i: (i, 0)),
                  pl.BlockSpec((tm, MLA_QBLK), lambda b, i: (i, 0)),
                  pl.BlockSpec((tm, MLA_QBLK), lambda b, i: (i, 0)),
                  pl.BlockSpec((16, tm), lambda b, i: (0, i)),
                  pl.BlockSpec((16, tm), lambda b, i: (0, i))],
        out_specs=[pl.BlockSpec((1, tm, hq), lambda b, i: (b, i, 0)),
                   pl.BlockSpec((1, MLA_ROW, tm), lambda b, i: (b, 0, i)),
                   x_spec],
        out_shape=[jax.ShapeDtypeStruct((nb, L, hq), BF16),
                   jax.ShapeDtypeStruct((nb, MLA_ROW, L), F32),
                   x_shape],
        compiler_params=_cparams(("parallel", "parallel")),
        name="mla_proj_abs" if absorbed else "mla_proj",
    )(x3, g, w["wqa"], w["qn"], w["wqup"], w["wkvt"], w["kvn"], w["wukT"] if absorbed else w["wukvT"],
      tabs["cq"], tabs["s1"], tabs["s2"], tabs["cosT"], tabs["sinT"])


def _mla_attn_prompt_kernel(q_ref, kn_ref, v_ref, kpe_ref, o_ref, kt_scr, vt_scr, *, tq, seq):
    qi = pl.program_id(2)

    @pl.when(qi == 0)
    def _():
        kpe = kpe_ref[0].astype(BF16)
        zeros = jnp.zeros((MLA_QBLK - MLA_ROPE - MLA_NOPE, seq), BF16)
        ones_row = jnp.where(lax.broadcasted_iota(jnp.int32, (MLA_V, seq), 0) == 0, 1.0, 0.0).astype(BF16)
        for hh in range(2):
            kt_scr[hh, 0:MLA_ROPE] = kpe
            kt_scr[hh, MLA_ROPE:MLA_ROPE + MLA_NOPE] = kn_ref[0, hh * MLA_NOPE:(hh + 1) * MLA_NOPE]
            kt_scr[hh, MLA_ROPE + MLA_NOPE:] = zeros
            vt_scr[hh, 0:MLA_V] = v_ref[0, hh * MLA_V:(hh + 1) * MLA_V]
            vt_scr[hh, MLA_V:] = ones_row

    row = lax.broadcasted_iota(jnp.int32, (tq, tq), 0)
    col = lax.broadcasted_iota(jnp.int32, (tq, tq), 1)
    causal = col <= row
    qs = [q_ref[0, :, hh * MLA_QBLK:(hh + 1) * MLA_QBLK] for hh in range(2)]

    def step(j, carry, masked):
        off = pl.multiple_of(j * tq, tq)
        new = []
        for hh in range(2):
            m, acc = carry[hh]
            s = jnp.dot(qs[hh], kt_scr[hh, :, pl.ds(off, tq)], preferred_element_type=F32)
            if masked:
                s = jnp.where(causal, s, NEG)
            m_new = jnp.maximum(m, jnp.max(s, axis=-1, keepdims=True))
            p = jnp.exp2(s - m_new).astype(BF16)
            acc = jnp.exp2(m - m_new) * acc + lax.dot_general(p, vt_scr[hh, :, pl.ds(off, tq)], _NT,
                                                             preferred_element_type=F32)
            new.append((m_new, acc))
        return tuple(new)

    init = (jnp.full((tq, 1), NEG, F32), jnp.zeros((tq, 2 * MLA_V), F32))
    carry = lax.fori_loop(0, qi, functools.partial(step, masked=False), (init, init))
    carry = step(qi, carry, True)
    o_ref[0] = jnp.concatenate([acc[:, :MLA_V] / acc[:, MLA_V:MLA_V + 1] for _, acc in carry],
                               axis=-1).astype(BF16)


def mla_attn_prompt(q, knv, kvt, tq):
    b, s, _ = q.shape
    hp = MLA_HEADS // 2
    return pl.pallas_call(
        functools.partial(_mla_attn_prompt_kernel, tq=tq, seq=s),
        grid=(b, hp, s // tq),
        in_specs=[pl.BlockSpec((1, tq, 2 * MLA_QBLK), lambda bi, h, i: (bi, i, h)),
                  pl.BlockSpec((1, 2 * MLA_NOPE, s), lambda bi, h, i: (bi, h, 0)),
                  pl.BlockSpec((1, 2 * MLA_V, s), lambda bi, h, i: (bi, hp + h, 0)),
                  pl.BlockSpec((1, MLA_ROPE, s), lambda bi, h, i: (bi, MLA_KV_LORA // MLA_ROPE, 0))],
        out_specs=pl.BlockSpec((1, tq, 2 * MLA_V), lambda bi, h, i: (bi, i, h)),
        out_shape=jax.ShapeDtypeStruct((b, s, MLA_HEADS * MLA_V), BF16),
        scratch_shapes=[pltpu.VMEM((2, MLA_QBLK, s), BF16), pltpu.VMEM((2, 2 * MLA_V, s), BF16)],
        compiler_params=_cparams(("parallel", "parallel", "arbitrary")),
        name="mla_attn_prompt",
    )(q, knv, knv, kvt)


def _mla_attn_sample_kernel(pt_ref, qc_ref, *refs, ppc, t_new):
    page_refs = refs[:ppc]
    kvn_ref, wuv_ref, o_ref, kt_scr, m_scr, l_scr, acc_scr = refs[ppc:]
    c = pl.program_id(1)
    rows = MLA_HEADS * t_new

    @pl.when(c == 0)
    def _():
        m_scr[...] = jnp.full((rows, 1), NEG, F32)
        l_scr[...] = jnp.zeros((rows, 1), F32)
        acc_scr[...] = jnp.zeros((rows, MLA_KV_LORA), F32)

    for i in range(ppc):
        kt_scr[:, i * PAGE:(i + 1) * PAGE] = page_refs[i][0, 0].astype(BF16)
    q = qc_ref[0][:, :MLA_ROW]
    s = jnp.dot(q, kt_scr[...], preferred_element_type=F32)
    p, alpha, m, l = _softmax_update(s, m_scr[...], l_scr[...])
    acc = alpha * acc_scr[...] + lax.dot_general(p.astype(BF16), kt_scr[:MLA_KV_LORA, :], _NT,
                                                 preferred_element_type=F32)
    m_scr[...] = m
    l_scr[...] = l
    acc_scr[...] = acc

    @pl.when(c == pl.num_programs(1) - 1)
    def _():
        kn = kvn_ref[0].astype(BF16)
        s2 = jnp.dot(q, kn, preferred_element_type=F32)
        tq = lax.broadcasted_iota(jnp.int32, s2.shape, 0) % t_new
        tk = lax.broadcasted_iota(jnp.int32, s2.shape, 1)
        s2 = jnp.where(tk <= tq, s2, NEG)
        p2, alpha2, m2, l2 = _softmax_update(s2, m, l)
        acc2 = alpha2 * acc + lax.dot_general(p2.astype(BF16), kn[:MLA_KV_LORA], _NT, preferred_element_type=F32)
        o_lat = _safe_div(acc2, l2).astype(BF16)
        outs = [jnp.dot(o_lat[h * t_new:(h + 1) * t_new], wuv_ref[h], preferred_element_type=F32)
                for h in range(MLA_HEADS)]
        o_ref[0] = jnp.concatenate(outs, axis=-1)


def mla_attn_sample(page_table, qc, cache_t, layer, kv_new, wuv, ppc):
    db, rows, _ = qc.shape
    t_new = rows // MLA_HEADS
    n_pages = page_table.shape[1]
    pt = page_table.reshape(-1)

    def page_spec(k):
        return pl.BlockSpec((1, 1, MLA_ROW, PAGE),
                            lambda b, c, pt_ref: (layer, pt_ref[b * n_pages + c * ppc + k], 0, 0))

    grid_spec = pltpu.PrefetchScalarGridSpec(
        num_scalar_prefetch=1,
        grid=(db, n_pages // ppc),
        in_specs=[pl.BlockSpec((1, rows, qc.shape[2]), lambda b, c, pt_ref: (b, 0, 0))]
                 + [page_spec(k) for k in range(ppc)]
                 + [pl.BlockSpec((1, MLA_ROW, PAGE), lambda b, c, pt_ref: (b, 0, 0)),
                    pl.BlockSpec((MLA_HEADS, MLA_KV_LORA, MLA_V), lambda b, c, pt_ref: (0, 0, 0))],
        out_specs=pl.BlockSpec((1, t_new, MLA_HEADS * MLA_V), lambda b, c, pt_ref: (b, 0, 0)),
        scratch_shapes=[pltpu.VMEM((MLA_ROW, ppc * PAGE), BF16),
                        pltpu.VMEM((rows, 1), F32), pltpu.VMEM((rows, 1), F32),
                        pltpu.VMEM((rows, MLA_KV_LORA), F32)],
    )
    return pl.pallas_call(
        functools.partial(_mla_attn_sample_kernel, ppc=ppc, t_new=t_new),
        grid_spec=grid_spec,
        out_shape=jax.ShapeDtypeStruct((db, t_new, MLA_HEADS * MLA_V), F32),
        compiler_params=_cparams(("parallel", "arbitrary")),
        name="mla_attn_sample",
    )(pt, qc, *([cache_t] * ppc), kv_new, wuv)


def _ffn_kernel(*refs, tm, tf, tiles_per_seq, sample, final):
    x_ref, g_ref, wg_ref, wu_ref, cw_ref, cb_ref, wo_ref = refs[:7]
    k = 7
    if sample:
        p1_ref, p2_ref = refs[k:k + 2]
        k += 2
    if final:
        gf_ref = refs[k]
        k += 1
    y_ref, cv_ref, gs_scr, carry_scr = refs[k:]
    i = pl.program_id(0)
    ns = D_FF // tf

    if not sample:
        @pl.when((i % tiles_per_seq) == 0)
        def _():
            carry_scr[...] = jnp.zeros((ns, 8, tf), F32)

    x = x_ref[...]
    xn = _rms_rows(x, g_ref[...]).astype(BF16)
    y = x
    for cs in range(ns):
        cols = slice(cs * tf, (cs + 1) * tf)
        g = jnp.dot(xn, wg_ref[:, cols], preferred_element_type=F32)
        u = jnp.dot(xn, wu_ref[:, cols], preferred_element_type=F32)
        gs_scr[cs, 8:8 + tm] = g
        if sample:
            gs_scr[cs, 0:8] = jnp.zeros((8, tf), F32)
            rin = lax.broadcasted_iota(jnp.int32, (tm, 1), 0) % 8
            g1 = jnp.where(rin < 1, p1_ref[:, cols], gs_scr[cs, 7:7 + tm])
            g2 = jnp.where(rin < 2, p2_ref[:, cols], gs_scr[cs, 6:6 + tm])
            cv_ref[:, cols] = g
        else:
            gs_scr[cs, 0:8] = carry_scr[cs]
            g1 = gs_scr[cs, 7:7 + tm]
            g2 = gs_scr[cs, 6:6 + tm]
            carry_scr[cs] = gs_scr[cs, tm:tm + 8]
            cv_ref[0, :, cols] = g[tm - 2:tm]
        gc = cb_ref[:, cols] + cw_ref[0:1, cols] * g2 + cw_ref[1:2, cols] * g1 + cw_ref[2:3, cols] * g
        act = (gc * jax.nn.sigmoid(gc) * u).astype(BF16)
        y = y + jnp.dot(act, wo_ref[cols, :], preferred_element_type=F32)
    if final:
        y = _rms_rows(y, gf_ref[...])
    y_ref[...] = y


def conv_ffn(x, g, w_in, cw, cb, w_out, *, tm, tf, rows_per_seq, prev=None, final_g=None):
    n = x.shape[0]
    ns = D_FF // tf
    sample = prev is not None
    final = final_g is not None
    fixed = lambda shape, idx: pl.BlockSpec(shape, lambda i: idx, pipeline_mode=pl.Buffered(1))
    in_specs = [pl.BlockSpec((tm, D_MODEL), lambda i: (i, 0)),
                fixed((1, D_MODEL), (0, 0)),
                fixed((D_MODEL, D_FF), (0, 0)),
                fixed((D_MODEL, D_FF), (0, 1)),
                fixed((3, D_FF), (0, 0)),
                fixed((1, D_FF), (0, 0)),
                fixed((D_FF, D_MODEL), (0, 0))]
    args = [x, g, w_in, w_in, cw, cb, w_out]
    if sample:
        in_specs += [pl.BlockSpec((tm, D_FF), lambda i: (i, 0))] * 2
        args += list(prev)
        cv_shape = jax.ShapeDtypeStruct((n, D_FF), F32)
        cv_spec = pl.BlockSpec((tm, D_FF), lambda i: (i, 0))
        tiles_per_seq = 1
    else:
        tiles_per_seq = rows_per_seq // tm
        cv_shape = jax.ShapeDtypeStruct((n // tm, 2, D_FF), F32)
        cv_spec = pl.BlockSpec((1, 2, D_FF), lambda i: (i, 0, 0))
    if final:
        in_specs.append(fixed((1, D_MODEL), (0, 0)))
        args.append(final_g)
    return pl.pallas_call(
        functools.partial(_ffn_kernel, tm=tm, tf=tf, tiles_per_seq=tiles_per_seq, sample=sample, final=final),
        grid=(n // tm,),
        in_specs=in_specs,
        out_specs=[pl.BlockSpec((tm, D_MODEL), lambda i: (i, 0)), cv_spec],
        out_shape=[jax.ShapeDtypeStruct((n, D_MODEL), F32), cv_shape],
        scratch_shapes=[pltpu.VMEM((ns, tm + 8, tf), F32),
                        pltpu.VMEM((ns, 8, tf), F32)],
        compiler_params=_cparams(("arbitrary",)),
        name="conv_ffn_sample" if sample else "conv_ffn",
    )(*args)


def _rope_rows(kvt, base, ct, st, out, half):
    for gg in range(NSA_GROUPS):
        r0 = base + gg * NSA_DH
        x1 = kvt[r0:r0 + half]
        x2 = kvt[r0 + half:r0 + NSA_DH]
        out[0, gg * NSA_DH:gg * NSA_DH + half] = x1 * ct - x2 * st
        out[0, gg * NSA_DH + half:(gg + 1) * NSA_DH] = x1 * st + x2 * ct


def _nsa_proj_kernel(x_ref, g_ref, wq_ref, wg_ref, bg_ref, wkvt_ref, cq_ref, s1_ref, s2_ref, cost_ref, sint_ref,
                     q_out, gate_out, kc_out, ks_out, kw_out):
    half = NSA_DH // 2
    xn = _rms_rows(x_ref[0], g_ref[...]).astype(BF16)
    q = jnp.dot(xn, wq_ref[...], preferred_element_type=F32)
    cq, s1, s2 = cq_ref[...], s1_ref[...], s2_ref[...]
    for v in range(NSA_HEADS * NSA_DH // 128):
        blk = q[:, v * 128:(v + 1) * 128]
        blk = blk * cq + pltpu.roll(blk, half, 1) * s1 + pltpu.roll(blk, 128 - half, 1) * s2
        q_out[0, :, v * 128:(v + 1) * 128] = blk.astype(BF16)
    gl = jnp.dot(xn, wg_ref[...], preferred_element_type=F32) + bg_ref[...]
    gate_out[0] = jax.nn.sigmoid(gl)
    kvt = lax.dot_general(wkvt_ref[...], xn, _NT, preferred_element_type=F32)
    ct, st = cost_ref[...], sint_ref[...]
    kw = NSA_GROUPS * NSA_DH
    for br, out in enumerate((kc_out, ks_out, kw_out)):
        _rope_rows(kvt, br * NSA_KV, ct, st, out, half)
        out[0, kw:] = kvt[br * NSA_KV + kw:(br + 1) * NSA_KV]


def nsa_proj(x3, g, w, tabs, tm):
    nb, L, _ = x3.shape
    const = lambda b, i: (0, 0)
    hq = NSA_HEADS * NSA_DH
    kv_spec = pl.BlockSpec((1, NSA_KV, tm), lambda b, i: (b, 0, i))
    kv_shape = jax.ShapeDtypeStruct((nb, NSA_KV, L), F32)
    return pl.pallas_call(
        _nsa_proj_kernel,
        grid=(nb, L // tm),
        in_specs=[pl.BlockSpec((1, tm, D_MODEL), lambda b, i: (b, i, 0)),
                  pl.BlockSpec((1, D_MODEL), const),
                  pl.BlockSpec((D_MODEL, hq), const),
                  pl.BlockSpec((D_MODEL, 128), const),
                  pl.BlockSpec((1, 128), const),
                  pl.BlockSpec((3 * NSA_KV, D_MODEL), const),
                  pl.BlockSpec((tm, 128), lambda b, i: (i, 0)),
                  pl.BlockSpec((tm, 128), lambda b, i: (i, 0)),
                  pl.BlockSpec((tm, 128), lambda b, i: (i, 0)),
                  pl.BlockSpec((NSA_DH // 2, tm), lambda b, i: (0, i)),
                  pl.BlockSpec((NSA_DH // 2, tm), lambda b, i: (0, i))],
        out_specs=[pl.BlockSpec((1, tm, hq), lambda b, i: (b, i, 0)),
                   pl.BlockSpec((1, tm, 128), lambda b, i: (b, i, 0)),
                   kv_spec, kv_spec, kv_spec],
        out_shape=[jax.ShapeDtypeStruct((nb, L, hq), BF16),
                   jax.ShapeDtypeStruct((nb, L, 128), F32),
                   kv_shape, kv_shape, kv_shape],
        compiler_params=_cparams(("parallel", "parallel")),
        name="nsa_proj",
    )(x3, g, w["wq"], w["wg"], w["bg"], w["wkvt"], tabs["cq"], tabs["s1"], tabs["s2"], tabs["cosT"], tabs["sinT"])


def _cmp_bias_kernel(pe_ref, w1_ref, o_ref):
    o_ref[0] = jnp.sum(w1_ref[0] * pe_ref[0], axis=0, keepdims=True)


def cmp_bias(pe_col, w1_flat):
    n = pe_col.shape[1]
    return pl.pallas_call(
        _cmp_bias_kernel,
        grid=(2,),
        in_specs=[pl.BlockSpec((1, n, 1), lambda c: (c, 0, 0)),
                  pl.BlockSpec((1, n, CMP_HID), lambda c: (c, 0, 0))],
        out_specs=pl.BlockSpec((1, 1, CMP_HID), lambda c: (c, 0, 0)),
        out_shape=jax.ShapeDtypeStruct((2, 1, CMP_HID), F32),
        name="cmp_bias",
    )(pe_col, w1_flat)


def _compress_kernel(*refs, npg, paged):
    if paged:
        refs = refs[1:]
    page_refs = refs[:npg]
    nv = NSA_KV // 128
    w1_ref, bias_ref, w2_ref, out_ref = refs[npg:npg + 4]
    xt_scrs = refs[npg + 4:npg + 4 + nv]
    fs_scr = refs[npg + 4 + nv]
    c = pl.program_id(1)
    nchunk = npg * PAGE // CMP_STRIDE
    cpp = PAGE // CMP_STRIDE

    @pl.when(c == 0)
    def _():
        fs_scr[:, 0:8] = jnp.zeros((2 * NSA_GROUPS, 8, CMP_HID), F32)

    for v in range(nv):
        cc = v // (nv // 2)
        xt_scr = xt_scrs[v]
        for p in range(npg):
            page = page_refs[p][0, 0, v * 128:(v + 1) * 128] if paged else page_refs[p][0, v * 128:(v + 1) * 128]
            page_t = page.T
            for j in range(cpp):
                r0 = (p * cpp + j) * CMP_PITCH
                xt_scr[r0:r0 + CMP_STRIDE, :] = page_t[j * CMP_STRIDE:(j + 1) * CMP_STRIDE]
        both = jnp.zeros((nchunk, 4 * CMP_HID), F32)
        for lp in range(CMP_STRIDE // 2):
            x2 = jnp.concatenate([xt_scr[pl.ds(2 * lp + i, nchunk, stride=CMP_PITCH), :] for i in range(2)],
                                 axis=1).astype(BF16)
            both = both + jnp.dot(x2, w1_ref[cc, lp], preferred_element_type=F32)
        for u in range(2):
            cg = 2 * v + u
            acc = both[:, u * 2 * CMP_HID:(u + 1) * 2 * CMP_HID]
            fs_scr[cg, 8:8 + nchunk] = acc[:, :CMP_HID]
            h = jax.nn.gelu(fs_scr[cg, 7:7 + nchunk] + acc[:, CMP_HID:] + bias_ref[cc])
            fs_scr[cg, 0:8] = fs_scr[cg, nchunk:nchunk + 8]
            out_ref[0, cg] = jnp.dot(h.astype(BF16), w2_ref[cc], preferred_element_type=F32).astype(BF16)


def compress(kv_src, w, *, npg=16, page_table=None, layer=None):
    paged = page_table is not None
    if paged:
        nb, n_pages = page_table.shape
        pt = page_table.reshape(-1)
        specs = [pl.BlockSpec((1, 1, NSA_KV, PAGE),
                              (lambda b, c, pt_ref, k=k: (layer, pt_ref[b * n_pages + c * npg + k], 0, 0)))
                 for k in range(npg)]
        imap = lambda f: (lambda b, c, pt_ref: f(b, c))
    else:
        nb = kv_src.shape[0]
        n_pages = kv_src.shape[2] // PAGE
        specs = [pl.BlockSpec((1, NSA_KV, PAGE), (lambda b, c, k=k: (b, 0, c * npg + k))) for k in range(npg)]
        imap = lambda f: f
    nchunk = npg * PAGE // CMP_STRIDE
    in_specs = specs + [pl.BlockSpec((2, CMP_STRIDE // 2, 4 * NSA_DH, 4 * CMP_HID), imap(lambda b, c: (0, 0, 0, 0))),
                        pl.BlockSpec((2, 1, CMP_HID), imap(lambda b, c: (0, 0, 0))),
                        pl.BlockSpec((2, CMP_HID, NSA_DH), imap(lambda b, c: (0, 0, 0)))]
    out_spec = pl.BlockSpec((1, 2 * NSA_GROUPS, nchunk, NSA_DH), imap(lambda b, c: (b, 0, c, 0)))
    scratch = [pltpu.VMEM((nchunk * CMP_PITCH, 128), F32)] * (NSA_KV // 128) \
        + [pltpu.VMEM((2 * NSA_GROUPS, nchunk + 8, CMP_HID), F32)]
    grid = (nb, n_pages // npg)
    out_shape = jax.ShapeDtypeStruct((nb, 2 * NSA_GROUPS, n_pages * PAGE // CMP_STRIDE, NSA_DH), BF16)
    kern = functools.partial(_compress_kernel, npg=npg, paged=paged)
    args = [kv_src] * npg + [w["w1big"], w["cbias"], w["w2"]]
    if paged:
        gs = pltpu.PrefetchScalarGridSpec(num_scalar_prefetch=1, grid=grid, in_specs=in_specs,
                                          out_specs=out_spec, scratch_shapes=scratch)
        return pl.pallas_call(kern, grid_spec=gs, out_shape=out_shape,
                              compiler_params=_cparams(("parallel", "arbitrary")), name="compress_paged")(pt, *args)
    return pl.pallas_call(kern, grid=grid, in_specs=in_specs, out_specs=out_spec, out_shape=out_shape,
                          scratch_shapes=scratch, compiler_params=_cparams(("parallel", "arbitrary")),
                          name="compress")(*args)


def _masked_softmax(s, valid):
    s = jnp.where(valid, s, NEG)
    m = jnp.max(s, axis=-1, keepdims=True)
    p = jnp.where(valid, jnp.exp2(s - m), 0.0)
    den = jnp.sum(p, axis=-1, keepdims=True)
    return p / jnp.where(den > 0, den, 1.0)


def _nsa_attn_prompt_kernel(q_ref, gate_ref, kc_ref, vc_ref, kst_ref, vst_ref, kwt_ref, vwt_ref, cover_ref, e_ref,
                            o_ref, ks_scr, vs_scr, kw_scr, vw_scr, *, tq, seq):
    i = pl.program_id(2)
    t0 = i * tq
    n_blk = seq // SLC_BLOCK

    @pl.when(i == 0)
    def _():
        ones_row = jnp.where(lax.broadcasted_iota(jnp.int32, (NSA_DH, seq), 0) == 0, 1.0, 0.0).astype(BF16)
        ks_scr[0:NSA_DH] = kst_ref[0].astype(BF16)
        ks_scr[NSA_DH:] = e_ref[...]
        vs_scr[0:NSA_DH] = vst_ref[0].astype(BF16)
        vs_scr[NSA_DH:] = ones_row
        kw_scr[...] = kwt_ref[0].astype(BF16)
        vw_scr[0:NSA_DH] = vwt_ref[0].astype(BF16)
        vw_scr[NSA_DH:] = ones_row

    q4 = jnp.concatenate([q_ref[0, :, h * NSA_DH:(h + 1) * NSA_DH] for h in range(NSA_HPG)], axis=0)
    rows = NSA_HPG * tq
    t_row = t0 + lax.broadcasted_iota(jnp.int32, (tq, 1), 0)
    t_row4 = t0 + lax.broadcasted_iota(jnp.int32, (rows, 1), 0) % tq

    ci = lax.broadcasted_iota(jnp.int32, (rows, kc_ref.shape[2]), 1)
    valid_c = (ci >= 1) & (CMP_STRIDE * ci + CMP_STRIDE - 1 <= t_row4)
    s_c = lax.dot_general(q4, kc_ref[0, 0], _NT, preferred_element_type=F32)
    p_c = _masked_softmax(s_c, valid_c)
    o_cmp = jnp.dot(p_c.astype(BF16), vc_ref[0, 0], preferred_element_type=F32)

    p_sum = p_c[0:tq]
    for h in range(1, NSA_HPG):
        p_sum = p_sum + p_c[h * tq:(h + 1) * tq]
    nq = 128 // n_blk
    rq = tq // nq
    p_wide = jnp.concatenate([p_sum[u * rq:(u + 1) * rq] for u in range(nq)], axis=1)
    imp = jnp.dot(p_wide, cover_ref[...], preferred_element_type=F32, precision=lax.Precision.HIGHEST)
    lane = lax.broadcasted_iota(jnp.int32, (rq, 128), 1)
    blk = lane % n_blk
    t_pk = t0 + (lane // n_blk) * rq + lax.broadcasted_iota(jnp.int32, (rq, 128), 0)
    cur = t_pk // SLC_BLOCK
    forced = (blk == 0) | ((blk <= cur) & (blk > cur - 2))
    score = jnp.where(blk * SLC_BLOCK <= t_pk, imp + jnp.where(forced, FORCE_BONUS, 0.0), NEG)
    cnt = jnp.zeros((rq, 128), F32)
    for k in range(1, n_blk):
        other = jnp.where(blk >= k, pltpu.roll(score, k, 1), pltpu.roll(score, 128 - n_blk + k, 1))
        beats = (other > score) | ((other == score) & (blk >= k))
        cnt = cnt + jnp.where(beats, 1.0, 0.0)
    sel_pk = jnp.where(cnt < N_SELECT, 0.0, NEG)
    sel_bias = jnp.concatenate([sel_pk[:, u * n_blk:(u + 1) * n_blk] for u in range(nq)], axis=0).astype(BF16)
    q_aug = jnp.concatenate([q4, jnp.concatenate([sel_bias] * NSA_HPG, axis=0)], axis=1)

    def weighted(p, vt):
        return lax.dot_general(p.astype(BF16), vt, _NT, preferred_element_type=F32)

    d_iota = lax.broadcasted_iota(jnp.int32, (tq, tq), 0) - lax.broadcasted_iota(jnp.int32, (tq, tq), 1)
    causal_bias = jnp.where(d_iota >= 0, 0.0, NEG)

    def slc_step(j, carry, diagonal):
        m, acc = carry
        off = pl.multiple_of(j * tq, tq)
        s = jnp.dot(q_aug, ks_scr[:, pl.ds(off, tq)], preferred_element_type=F32)
        if diagonal:
            s = (s.reshape(NSA_HPG, tq, tq) + causal_bias[None]).reshape(rows, tq)
        m_new = jnp.maximum(m, jnp.max(s, axis=-1, keepdims=True))
        alpha = jnp.exp2(m - m_new)
        acc = alpha * acc + weighted(jnp.exp2(s - m_new), vs_scr[:, pl.ds(off, tq)])
        return m_new, acc

    init = (jnp.full((rows, 1), NEG, F32), jnp.zeros((rows, 2 * NSA_DH), F32))
    carry = lax.fori_loop(0, i, functools.partial(slc_step, diagonal=False), init)
    _, acc_s = slc_step(i, carry, True)
    o_slc = acc_s[:, :NSA_DH] / acc_s[:, NSA_DH:NSA_DH + 1]

    span = WINDOW + tq
    w_off = pl.multiple_of(jnp.maximum(t0 - WINDOW, 0), tq)
    s_w = jnp.dot(q4, kw_scr[:, pl.ds(w_off, span)], preferred_element_type=F32)
    d = t_row - (w_off + lax.broadcasted_iota(jnp.int32, (tq, span), 1))
    bias_w = jnp.where((d >= 0) & (d < WINDOW), 0.0, NEG)
    s_w = (s_w.reshape(NSA_HPG, tq, span) + bias_w[None]).reshape(rows, span)
    acc_w = weighted(jnp.exp2(s_w - jnp.max(s_w, axis=-1, keepdims=True)), vw_scr[:, pl.ds(w_off, span)])
    o_win = acc_w[:, :NSA_DH] / acc_w[:, NSA_DH:NSA_DH + 1]

    outs = []
    gate = gate_ref[0, 0]
    for h in range(NSA_HPG):
        r = slice(h * tq, (h + 1) * tq)
        outs.append(gate[:, 3 * h:3 * h + 1] * o_cmp[r] + gate[:, 3 * h + 1:3 * h + 2] * o_slc[r]
                    + gate[:, 3 * h + 2:3 * h + 3] * o_win[r])
    o_ref[0] = jnp.concatenate(outs, axis=-1).astype(BF16)


def nsa_attn_prompt(q, gates_g, cmp_kv, kst, kwt, cover, e_mat, tq=256):
    b, s, _ = q.shape
    g = NSA_GROUPS
    gw = NSA_HPG * NSA_DH
    ncmp = cmp_kv.shape[2]
    n_blk = s // SLC_BLOCK
    kt_spec = lambda off: pl.BlockSpec((1, NSA_DH, s), lambda bi, gi, i: (bi, off + gi, 0))
    return pl.pallas_call(
        functools.partial(_nsa_attn_prompt_kernel, tq=tq, seq=s),
        grid=(b, g, s // tq),
        in_specs=[pl.BlockSpec((1, tq, gw), lambda bi, gi, i: (bi, i, gi)),
                  pl.BlockSpec((1, 1, tq, 3 * NSA_HPG), lambda bi, gi, i: (bi, gi, i, 0)),
                  pl.BlockSpec((1, 1, ncmp, NSA_DH), lambda bi, gi, i: (bi, gi, 0, 0)),
                  pl.BlockSpec((1, 1, ncmp, NSA_DH), lambda bi, gi, i: (bi, g + gi, 0, 0)),
                  kt_spec(0), kt_spec(g), kt_spec(0), kt_spec(g),
                  pl.BlockSpec(cover.shape, lambda bi, gi, i: (0, 0)),
                  pl.BlockSpec((n_blk, s), lambda bi, gi, i: (0, 0))],
        out_specs=pl.BlockSpec((1, tq, gw), lambda bi, gi, i: (bi, i, gi)),
        out_shape=jax.ShapeDtypeStruct((b, s, NSA_HEADS * NSA_DH), BF16),
        scratch_shapes=[pltpu.VMEM((NSA_DH + n_blk, s), BF16), pltpu.VMEM((2 * NSA_DH, s), BF16),
                        pltpu.VMEM((NSA_DH, s), BF16), pltpu.VMEM((2 * NSA_DH, s), BF16)],
        compiler_params=_cparams(("parallel", "parallel", "arbitrary")),
        name="nsa_attn_prompt",
    )(q, gates_g, cmp_kv, cmp_kv, kst, kst, kwt, kwt, cover, e_mat)


def _nsa_attn_sample_kernel(pt_ref, q_ref, qbd_ref, gate_ref, ckv_ref, cover_ref, *refs, ppc, t_new, past,
                            n_slc_pad):
    page_refs = refs[:ppc]
    (ksn_ref, win_ref, kwn_ref, e_ref, o_ref,
     kv_scr, sel_scr, ocmp_scr, m_scr, l_scr, acc_scr) = refs[ppc:]
    c = pl.program_id(1)
    g = NSA_GROUPS
    rg = NSA_HPG * t_new
    rows = g * rg
    kw = g * NSA_DH
    nk = ppc * PAGE
    bpc = nk // SLC_BLOCK
    tt_g = lax.broadcasted_iota(jnp.int32, (rg, 1), 0) % t_new
    tt8 = lax.broadcasted_iota(jnp.int32, (t_new, 1), 0)

    @pl.when(c == 0)
    def _():
        p_sums = []
        for gg in range(g):
            qg = q_ref[0, gg * rg:(gg + 1) * rg]
            s = lax.dot_general(qg, ckv_ref[0, gg], _NT, preferred_element_type=F32)
            ci = lax.broadcasted_iota(jnp.int32, s.shape, 1)
            valid = (ci >= 1) & (CMP_STRIDE * ci + CMP_STRIDE - 1 <= past + tt_g)
            p = _masked_softmax(s, valid)
            ocmp_scr[gg * rg:(gg + 1) * rg] = jnp.dot(p.astype(BF16), ckv_ref[0, g + gg], preferred_element_type=F32)
            p_sum = p[0:t_new]
            for h in range(1, NSA_HPG):
                p_sum = p_sum + p[h * t_new:(h + 1) * t_new]
            p_sums.append(p_sum)
        imp = jnp.dot(jnp.concatenate(p_sums, axis=0), cover_ref[...], preferred_element_type=F32,
                      precision=lax.Precision.HIGHEST)
        blk = lax.broadcasted_iota(jnp.int32, imp.shape, 1)
        tpos = past + lax.broadcasted_iota(jnp.int32, (g * t_new, 1), 0) % t_new
        cur = tpos // SLC_BLOCK
        forced = (blk == 0) | ((blk <= cur) & (blk > cur - 2))
        score = jnp.where(blk * SLC_BLOCK <= tpos, imp + jnp.where(forced, FORCE_BONUS, 0.0), NEG)
        n_slc = (past + t_new + SLC_BLOCK - 1) // SLC_BLOCK
        gone = jnp.float32(-3e38)
        score = jnp.where(blk < n_slc, score, gone)
        chosen = blk < 0
        for k in range(N_SELECT):
            mx = jnp.max(score, axis=-1, keepdims=True)
            pick = jnp.min(jnp.where(score == mx, blk, n_slc_pad), axis=-1, keepdims=True)
            chosen = chosen | (blk == pick)
            score = jnp.where(blk == pick, gone, score)
        sel = jnp.where(chosen, 0.0, NEG)
        for cb in range(n_slc_pad // bpc):
            sel_scr[cb] = sel[:, cb * bpc:(cb + 1) * bpc]
        m_scr[...] = jnp.full((rows, 1), NEG, F32)
        l_scr[...] = jnp.zeros((rows, 1), F32)
        acc_scr[...] = jnp.zeros((rows, kw), F32)

    def chosen_bias(cb, nkeys):
        return jnp.dot(sel_scr[cb].astype(BF16), e_ref[:, 0:nkeys], preferred_element_type=F32)

    def update(s, bias4, vt):
        nk = s.shape[-1]
        s = (s.reshape(g, NSA_HPG, t_new, nk) + bias4).reshape(rows, nk)
        p, alpha, m, l = _softmax_update(s, m_scr[...], l_scr[...])
        acc_scr[...] = alpha * acc_scr[...] + lax.dot_general(p.astype(BF16), vt, _NT, preferred_element_type=F32)
        m_scr[...] = m
        l_scr[...] = l

    def own_group(acc):
        rgrp = lax.broadcasted_iota(jnp.int32, (rows, 1), 0) // rg
        out = jnp.zeros((rows, NSA_DH), F32)
        for gg in range(g):
            out = out + jnp.where(rgrp == gg, acc[:, gg * NSA_DH:(gg + 1) * NSA_DH], 0.0)
        return out

    for i in range(ppc):
        kv_scr[:, i * PAGE:(i + 1) * PAGE] = page_refs[i][0, 0].astype(BF16)
    qbd = qbd_ref[0]
    update(jnp.dot(qbd, kv_scr[0:kw, :], preferred_element_type=F32),
           chosen_bias(c, nk).reshape(g, 1, t_new, nk), kv_scr[kw:, :])

    @pl.when(c == pl.num_programs(1) - 1)
    def _():
        ksn = ksn_ref[0].astype(BF16)
        kn = lax.broadcasted_iota(jnp.int32, (t_new, PAGE), 1)
        kn_g = lax.broadcasted_iota(jnp.int32, (g * t_new, PAGE), 1)
        tt_gt = lax.broadcasted_iota(jnp.int32, (g * t_new, 1), 0) % t_new
        tail_ok = (kn_g <= tt_gt) & (kn_g < t_new)
        tail_bias = jnp.where(tail_ok, chosen_bias(past // nk, PAGE), NEG)
        update(jnp.dot(qbd, ksn[0:kw], preferred_element_type=F32), tail_bias.reshape(g, 1, t_new, PAGE), ksn[kw:])
        o_slc = _safe_div(own_group(acc_scr[...]), l_scr[...])

        m_scr[...] = jnp.full((rows, 1), NEG, F32)
        l_scr[...] = jnp.zeros((rows, 1), F32)
        acc_scr[...] = jnp.zeros((rows, kw), F32)
        lw = win_ref.shape[3]
        wk = win_ref[0, 0].astype(BF16)
        kwn = kwn_ref[0].astype(BF16)
        wi = lax.broadcasted_iota(jnp.int32, (t_new, lw), 1)
        d_old = tt8 + lw - wi
        bias_old = jnp.where((d_old >= 0) & (d_old < WINDOW), 0.0, NEG)
        d_new = tt8 - kn
        bias_new = jnp.where((d_new >= 0) & (d_new < WINDOW) & (kn < t_new), 0.0, NEG)
        update(jnp.dot(qbd, wk[0:kw], preferred_element_type=F32), bias_old.reshape(1, 1, t_new, lw), wk[kw:])
        update(jnp.dot(qbd, kwn[0:kw], preferred_element_type=F32), bias_new.reshape(1, 1, t_new, PAGE), kwn[kw:])
        o_win = _safe_div(own_group(acc_scr[...]), l_scr[...])
        gate = gate_ref[0]
        o_ref[0] = gate[:, 0:1] * ocmp_scr[...] + gate[:, 1:2] * o_slc + gate[:, 2:3] * o_win


def nsa_attn_sample(page_table, q, qbd, gates, cmp_kv, cover, cache_t, layer, ks_new, win_t, kw_new, ppc):
    db, rows, _ = q.shape
    t_new = rows // NSA_HEADS
    n_pages = page_table.shape[1]
    past = n_pages * PAGE
    ncmp = cmp_kv.shape[2]
    n_slc_pad = cover.shape[1]
    lw = win_t.shape[3]
    pt = page_table.reshape(-1)
    nk = ppc * PAGE
    bpc = nk // SLC_BLOCK
    e_mat = (jnp.arange(bpc)[:, None] == jnp.arange(nk)[None, :] // SLC_BLOCK).astype(BF16)

    def page_spec(k):
        return pl.BlockSpec((1, 1, NSA_KV, PAGE),
                            lambda b, c, pt_ref: (layer, pt_ref[b * n_pages + c * ppc + k], 0, 0))

    per_b = lambda *blk: pl.BlockSpec((1,) + blk, lambda b, c, pt_ref: (b,) + (0,) * len(blk))
    grid_spec = pltpu.PrefetchScalarGridSpec(
        num_scalar_prefetch=1,
        grid=(db, n_pages // ppc),
        in_specs=[per_b(rows, NSA_DH), per_b(rows, NSA_GROUPS * NSA_DH), per_b(rows, 3),
                  per_b(2 * NSA_GROUPS, ncmp, NSA_DH),
                  pl.BlockSpec((ncmp, n_slc_pad), lambda b, c, pt_ref: (0, 0))]
                 + [page_spec(k) for k in range(ppc)]
                 + [per_b(NSA_KV, PAGE),
                    pl.BlockSpec((1, 1, NSA_KV, lw), lambda b, c, pt_ref: (layer, b, 0, 0)),
                    per_b(NSA_KV, PAGE),
                    pl.BlockSpec((bpc, nk), lambda b, c, pt_ref: (0, 0))],
        out_specs=per_b(rows, NSA_DH),
        scratch_shapes=[pltpu.VMEM((NSA_KV, ppc * PAGE), BF16),
                        pltpu.VMEM((n_slc_pad // bpc, NSA_GROUPS * t_new, bpc), F32),
                        pltpu.VMEM((rows, NSA_DH), F32),
                        pltpu.VMEM((rows, 1), F32), pltpu.VMEM((rows, 1), F32),
                        pltpu.VMEM((rows, NSA_GROUPS * NSA_DH), F32)],
    )
    return pl.pallas_call(
        functools.partial(_nsa_attn_sample_kernel, ppc=ppc, t_new=t_new, past=past, n_slc_pad=n_slc_pad),
        grid_spec=grid_spec,
        out_shape=jax.ShapeDtypeStruct((db, rows, NSA_DH), F32),
        compiler_params=_cparams(("parallel", "arbitrary")),
        name="nsa_attn_sample",
    )(pt, q, qbd, gates, cmp_kv, cover, *([cache_t] * ppc), ks_new, win_t, kw_new, e_mat)


def _rope_cos_sin(pos, d):
    inv = ROPE_THETA ** (-jnp.arange(0, d, 2, dtype=F32) / d)
    ang = pos.astype(F32)[:, None] * inv[None, :]
    return jnp.cos(ang), jnp.sin(ang)


def _mla_tables(pos):
    cos, sin = _rope_cos_sin(pos, MLA_ROPE)
    n = pos.shape[0]
    z = lambda w: jnp.zeros((n, w), F32)
    cq = jnp.concatenate([cos, cos, jnp.ones((n, MLA_NOPE), F32), z(32)], axis=1) * MLA_QSCALE
    s1 = jnp.concatenate([z(16), sin, z(96)], axis=1) * MLA_QSCALE
    s2 = jnp.concatenate([-sin, z(112)], axis=1) * MLA_QSCALE
    return dict(cq=cq, s1=s1, s2=s2, cosT=cos.T, sinT=sin.T)


def _mla_weights(w_in, q_norm, kv_norm, w_q_up, w_uk, w_uv, w_out):
    wq = w_q_up.reshape(MLA_Q_LORA, MLA_HEADS, MLA_NOPE + MLA_ROPE)
    wq = jnp.concatenate([wq[..., MLA_NOPE:], wq[..., :MLA_NOPE],
                          jnp.zeros((MLA_Q_LORA, MLA_HEADS, MLA_QBLK - MLA_NOPE - MLA_ROPE), F32)], axis=-1)
    uk_t = jnp.transpose(w_uk, (1, 2, 0))
    uv_t = jnp.transpose(w_uv, (1, 2, 0))
    return dict(
        wqa=w_in[:, :MLA_Q_LORA].astype(BF16),
        wkvt=w_in[:, MLA_Q_LORA:].T.astype(BF16),
        qn=q_norm.reshape(1, -1),
        kvn=kv_norm.reshape(-1, 1),
        wqup=wq.reshape(MLA_Q_LORA, MLA_HEADS * MLA_QBLK).astype(BF16),
        wukvT=jnp.concatenate([uk_t.reshape(-1, MLA_KV_LORA), uv_t.reshape(-1, MLA_KV_LORA)], axis=0).astype(BF16),
        wukT=uk_t.astype(BF16),
        wuv=jnp.transpose(w_uv, (1, 0, 2)).astype(BF16),
        wout=w_out.astype(BF16),
    )


def mla_layer_prompt(xp, g, w, tabs):
    b, s, _ = xp.shape
    q, kvt, knv = mla_proj(xp, g, w, tabs, tm=256, absorbed=False)
    o = mla_attn_prompt(q, knv, kvt, tq=512)
    x = resid_matmul(o.reshape(b * s, -1), w["wout"], xp.reshape(b * s, -1), tm=512)
    return x.reshape(b, s, -1), kvt


def mla_layer_sample(xs, g, w, tabs, cache_t, layer, page_table):
    db, t, _ = xs.shape
    n = db * t
    _, kvt, qc = mla_proj(xs.reshape(1, n, -1), g, w, tabs, tm=n, absorbed=True)
    qc = qc.reshape(MLA_HEADS, db, t, -1).transpose(1, 0, 2, 3).reshape(db, MLA_HEADS * t, -1)
    kv_new = kvt.reshape(MLA_ROW, db, t).transpose(1, 0, 2)
    kv_pad = jnp.pad(kv_new, ((0, 0), (0, 0), (0, PAGE - t)))
    o = mla_attn_sample(page_table, qc, cache_t, layer, kv_pad, w["wuv"], ppc=32)
    x = resid_matmul(o.reshape(n, -1).astype(BF16), w["wout"], xs.reshape(n, -1), tm=n)
    return x.reshape(db, t, -1), kv_new.transpose(0, 2, 1)


def ffn_layer_prompt(xp, g, w_in, cw, cb, w_out, final_g=None):
    b, s, _ = xp.shape
    y, cv = conv_ffn(xp.reshape(b * s, -1), g, w_in, cw, cb, w_out, tm=512, tf=FFN_SLAB,
                     rows_per_seq=s, final_g=final_g)
    tiles_per_seq = s // 512
    return y.reshape(b, s, -1), cv[tiles_per_seq - 1::tiles_per_seq]


def ffn_layer_sample(xs, g, w_in, cw, cb, w_out, state, final_g=None):
    db, t, _ = xs.shape
    z = jnp.zeros((db, t - 2, D_FF), F32)
    p1 = jnp.concatenate([state[:, 1:2], jnp.zeros((db, t - 1, D_FF), F32)], axis=1).reshape(db * t, D_FF)
    p2 = jnp.concatenate([state, z], axis=1).reshape(db * t, D_FF)
    y, gate = conv_ffn(xs.reshape(db * t, -1), g, w_in, cw, cb, w_out, tm=db * t, tf=FFN_SLAB,
                       rows_per_seq=t, prev=(p1, p2), final_g=final_g)
    return y.reshape(db, t, -1), gate.reshape(db, t, D_FF)[:, t - 2:]


def _nsa_tables(pos):
    cos, sin = _rope_cos_sin(pos, NSA_DH)
    z = jnp.zeros_like(sin)
    return dict(cq=jnp.concatenate([cos] * 4, axis=1) * NSA_QSCALE,
                s1=jnp.concatenate([z, sin, z, sin], axis=1) * NSA_QSCALE,
                s2=jnp.concatenate([-sin, z, -sin, z], axis=1) * NSA_QSCALE,
                cosT=cos.T, sinT=sin.T)


def _nsa_weights(w_in, b_gate, pe, w1, w2, w_out):
    hq = NSA_HEADS * NSA_DH
    n_gate = 3 * NSA_HEADS
    half = w1.shape[1] // 2
    w1f = w1.reshape(2, -1, CMP_HID)
    w1p = jnp.concatenate([w1[:, :half], w1[:, half:]], axis=-1).reshape(2, half // 2, 2, NSA_DH, 2 * CMP_HID)
    zero = jnp.zeros_like(w1p)
    w1big = jnp.stack([jnp.concatenate([w1p, zero], axis=-1), jnp.concatenate([zero, w1p], axis=-1)], axis=3)
    return dict(
        wq=w_in[:, :hq].astype(BF16),
        wkvt=w_in[:, hq:hq + 3 * NSA_KV].T.astype(BF16),
        wg=jnp.pad(w_in[:, hq + 3 * NSA_KV:], ((0, 0), (0, 128 - n_gate))).astype(BF16),
        bg=jnp.pad(b_gate, (0, 128 - n_gate)).reshape(1, 128),
        w1big=w1big.reshape(2, half // 2, 4 * NSA_DH, 4 * CMP_HID).astype(BF16),
        cbias=cmp_bias(pe.reshape(2, -1, 1), w1f),
        w2=w2.astype(BF16),
        wout=w_out.astype(BF16),
    )


def _cover_matrix(n_entries, n_blk, n_cols, periodic):
    i = jnp.arange(n_entries)[:, None]
    col = jnp.arange(n_cols)[None, :]
    s = col % n_blk if periodic else col
    j0 = (i - 1) * CMP_STRIDE
    hit = (i >= 1) & (j0 < s * SLC_BLOCK + SLC_BLOCK) & (j0 + 2 * CMP_STRIDE > s * SLC_BLOCK)
    if not periodic:
        hit = hit & (col < n_blk)
    return hit.astype(F32)


def nsa_layer_prompt(xp, g, w, tabs):
    b, s, _ = xp.shape
    n_blk = s // SLC_BLOCK
    q, gates, kct, kst, kwt = nsa_proj(xp, g, w, tabs, tm=256)
    cmp_kv = compress(kct, w)
    gates_g = gates[..., :3 * NSA_HEADS].reshape(b, s, NSA_GROUPS, 3 * NSA_HPG).transpose(0, 2, 1, 3)
    cover = jnp.kron(jnp.eye(128 // n_blk, dtype=F32), _cover_matrix(s // CMP_STRIDE, n_blk, n_blk, False))
    e_mat = (jnp.arange(n_blk)[:, None] == jnp.arange(s)[None, :] // SLC_BLOCK).astype(BF16)
    o = nsa_attn_prompt(q, gates_g, cmp_kv, kst, kwt, cover, e_mat)
    x = resid_matmul(o.reshape(b * s, -1), w["wout"], xp.reshape(b * s, -1), tm=512)
    return x.reshape(b, s, -1), kct, kst, kwt


def nsa_layer_sample(xs, g, w, tabs, cache_cmp_t, cache_slc_t, win_t, layer, page_table):
    db, t, _ = xs.shape
    n = db * t
    past = page_table.shape[1] * PAGE
    q, gates, kct, kst, kwt = nsa_proj(xs.reshape(1, n, -1), g, w, tabs, tm=n)
    per_seq = lambda a: a.reshape(NSA_KV, db, t).transpose(1, 0, 2)
    kc_new, ks_new, kw_new = per_seq(kct), per_seq(kst), per_seq(kwt)
    pad = lambda a: jnp.pad(a, ((0, 0), (0, 0), (0, PAGE - t)))
    cmp_kv = compress(cache_cmp_t, w, page_table=page_table, layer=layer)
    rows = lambda a, k: a.reshape(db, t, NSA_GROUPS, NSA_HPG, k).transpose(0, 2, 3, 1, 4).reshape(db, NSA_HEADS * t, k)
    q_r = rows(q.reshape(n, -1), NSA_DH)
    gates_r = rows(gates.reshape(n, -1)[:, :3 * NSA_HEADS], 3)
    n_slc = -(-(past + t) // SLC_BLOCK)
    n_slc_pad = -(-n_slc // 128) * 128
    cover = _cover_matrix(past // CMP_STRIDE, n_slc, n_slc_pad, False)
    own = jnp.arange(NSA_HEADS * t) // (NSA_HPG * t)
    q_bd = q_r[:, :, None, :] * (own[:, None] == jnp.arange(NSA_GROUPS)[None, :]).astype(BF16)[None, :, :, None]
    q_bd = q_bd.reshape(db, NSA_HEADS * t, NSA_GROUPS * NSA_DH)
    o = nsa_attn_sample(page_table, q_r, q_bd, gates_r, cmp_kv, cover, cache_slc_t, layer, pad(ks_new), win_t,
                        pad(kw_new), ppc=32)
    o = o.reshape(db, NSA_GROUPS, NSA_HPG, t, NSA_DH).transpose(0, 3, 1, 2, 4).reshape(n, -1)
    x = resid_matmul(o.astype(BF16), w["wout"], xs.reshape(n, -1), tm=n)
    return x.reshape(db, t, -1), kc_new, ks_new, kw_new


def kernel(x_prompt, x_sample, cache_mla_kv, cache_nsa_cmp_kv, cache_nsa_slc_kv, state_nsa_win_kv, state_ffn_conv,
           page_table, attn_norm, ffn_norm, final_norm, mla_w_in, mla_q_norm, mla_kv_norm, mla_w_q_up, mla_w_uk,
           mla_w_uv, mla_w_out, nsa_w_in, nsa_b_gate, nsa_cmp_pe, nsa_cmp_w1, nsa_cmp_w2, nsa_w_out, ffn_w_in,
           ffn_conv_w, ffn_conv_b, ffn_w_out):
    b, s, _ = x_prompt.shape
    db, t, _ = x_sample.shape
    depth = attn_norm.shape[0]
    past = page_table.shape[1] * PAGE
    pos_p = jnp.arange(s)
    pos_s = past + jnp.arange(db * t) % t
    mla_t = jnp.transpose(cache_mla_kv, (0, 1, 3, 2))
    to_t = lambda c: jnp.transpose(c, (0, 1, 3, 4, 5, 2)).reshape(c.shape[0], c.shape[1], NSA_KV, c.shape[2])
    cmp_t, slc_t, win_t = to_t(cache_nsa_cmp_kv), to_t(cache_nsa_slc_kv), to_t(state_nsa_win_kv)
    tabs_mla_p, tabs_mla_s = _mla_tables(pos_p), _mla_tables(pos_s)
    tabs_nsa_p, tabs_nsa_s = _nsa_tables(pos_p), _nsa_tables(pos_s)

    xp, xs = x_prompt, x_sample
    outs = {k: [] for k in ("mla_p", "mla_s", "cmp_p", "cmp_s", "slc_p", "slc_s", "win_p", "win_s", "conv_p", "conv_s")}
    rows_t = lambda a: a.transpose(0, 2, 1)
    for i in range(depth):
        l = i // 2
        g_attn = attn_norm[i].reshape(1, -1)
        if i % 2 == 0:
            w = _mla_weights(mla_w_in[l], mla_q_norm[l], mla_kv_norm[l], mla_w_q_up[l], mla_w_uk[l], mla_w_uv[l],
                             mla_w_out[l])
            xp, kvt_p = mla_layer_prompt(xp, g_attn, w, tabs_mla_p)
            xs, kv_s = mla_layer_sample(xs, g_attn, w, tabs_mla_s, mla_t, l, page_table)
            outs["mla_p"].append(kvt_p)
            outs["mla_s"].append(kv_s)
        else:
            w = _nsa_weights(nsa_w_in[l], nsa_b_gate[l], nsa_cmp_pe[l], nsa_cmp_w1[l], nsa_cmp_w2[l], nsa_w_out[l])
            xp, kct, kst, kwt = nsa_layer_prompt(xp, g_attn, w, tabs_nsa_p)
            xs, kc_n, ks_n, kw_n = nsa_layer_sample(xs, g_attn, w, tabs_nsa_s, cmp_t, slc_t, win_t, l, page_table)
            outs["cmp_p"].append(kct)
            outs["slc_p"].append(kst)
            outs["win_p"].append(kwt[:, :, s - min(WINDOW, s):])
            outs["cmp_s"].append(rows_t(kc_n))
            outs["slc_s"].append(rows_t(ks_n))
            outs["win_s"].append(jnp.concatenate([win_t[l], kw_n], axis=2)[:, :, -WINDOW:])
        fin = final_norm.reshape(1, -1) if i == depth - 1 else None
        g_ffn = ffn_norm[i].reshape(1, -1)
        f = (ffn_w_in[i].astype(BF16), ffn_conv_w[i], ffn_conv_b[i].reshape(1, -1), ffn_w_out[i].astype(BF16))
        xp, cv_p = ffn_layer_prompt(xp, g_ffn, *f, final_g=fin)
        xs, cv_s = ffn_layer_sample(xs, g_ffn, *f, state_ffn_conv[i], final_g=fin)
        outs["conv_p"].append(cv_p)
        outs["conv_s"].append(cv_s)

    kv6 = lambda a: a.reshape(a.shape[:3] + (2, NSA_GROUPS, NSA_DH))
    feat_major = lambda lst: kv6(jnp.stack(lst).transpose(0, 1, 3, 2))
    return (xp, xs,
            jnp.stack(outs["mla_p"]).transpose(0, 1, 3, 2), jnp.stack(outs["mla_s"]),
            feat_major(outs["cmp_p"]), kv6(jnp.stack(outs["cmp_s"])),
            feat_major(outs["slc_p"]), kv6(jnp.stack(outs["slc_s"])),
            feat_major(outs["win_p"]), feat_major(outs["win_s"]),
            jnp.stack(outs["conv_p"]), jnp.stack(outs["conv_s"]))
```

```python
import functools

import jax
import jax.numpy as jnp
from jax import lax
from jax.experimental import pallas as pl
from jax.experimental.pallas import tpu as pltpu

F32 = jnp.float32
BF16 = jnp.bfloat16

D_MODEL = 1024
EPS = 1e-6
ROPE_THETA = 10000.0
NEG = -1e30
PAGE = 128

MLA_HEADS = 16
MLA_Q_LORA = 384
MLA_KV_LORA = 256
MLA_NOPE = 64
MLA_ROPE = 32
MLA_V = 64
MLA_ROW = MLA_KV_LORA + MLA_ROPE
MLA_SCALE = (MLA_NOPE + MLA_ROPE) ** -0.5
MLA_QBLK = 128
MLA_HPS = 4

NSA_HEADS = 16
NSA_GROUPS = 4
NSA_HPG = 4
NSA_DH = 64
NSA_KV = 2 * NSA_GROUPS * NSA_DH
CMP_STRIDE = 16
CMP_PITCH = 24
CMP_HID = 128
SLC_BLOCK = 64
N_SELECT = 16
WINDOW = 512
FORCE_BONUS = 1e4
NSA_SCALE = NSA_DH ** -0.5

D_FF = 2816
FFN_SLAB = 2816
FFN_TM = 512
VMEM_LIMIT = 56 * 1024 * 1024

LOG2E = 1.4426950408889634
MLA_QSCALE = MLA_SCALE * LOG2E
NSA_QSCALE = NSA_SCALE * LOG2E

_NT = (((1,), (1,)), ((), ()))


def _cparams(sem):
    return pltpu.CompilerParams(dimension_semantics=sem, vmem_limit_bytes=VMEM_LIMIT)


def _rms_rows(x, g):
    ms = jnp.mean(x * x, axis=-1, keepdims=True)
    return x * lax.rsqrt(ms + EPS) * g


def _softmax_update(s, m, l):
    m_new = jnp.maximum(m, jnp.max(s, axis=-1, keepdims=True))
    alpha = jnp.exp2(m - m_new)
    p = jnp.exp2(s - m_new)
    l_new = alpha * l + jnp.sum(p, axis=-1, keepdims=True)
    return p, alpha, m_new, l_new


def _safe_div(acc, l):
    return acc / jnp.where(l > 0, l, 1.0)


def _resid_matmul_kernel(a_ref, w_ref, r_ref, o_ref):
    o_ref[...] = r_ref[...] + jnp.dot(a_ref[...], w_ref[...], preferred_element_type=F32)


def resid_matmul(a, w, res, tm):
    n, k = a.shape
    m = w.shape[1]
    return pl.pallas_call(
        _resid_matmul_kernel,
        grid=(n // tm,),
        in_specs=[pl.BlockSpec((tm, k), lambda i: (i, 0)),
                  pl.BlockSpec((k, m), lambda i: (0, 0)),
                  pl.BlockSpec((tm, m), lambda i: (i, 0))],
        out_specs=pl.BlockSpec((tm, m), lambda i: (i, 0)),
        out_shape=jax.ShapeDtypeStruct((n, m), F32),
        compiler_params=_cparams(("parallel",)),
        name="resid_matmul",
    )(a, w, res)


def _mla_proj_kernel(x_ref, g_ref, wqa_ref, qn_ref, wqup_ref, wkvt_ref, kvn_ref, wx_ref,
                     c_ref, s1_ref, s2_ref, cost_ref, sint_ref, q_out, kvt_out, x_out, *, absorbed):
    xn = _rms_rows(x_ref[0], g_ref[...]).astype(BF16)
    qa = jnp.dot(xn, wqa_ref[...], preferred_element_type=F32)
    qn = _rms_rows(qa, qn_ref[...]).astype(BF16)
    q = jnp.dot(qn, wqup_ref[...], preferred_element_type=F32)
    cq, s1, s2 = c_ref[...], s1_ref[...], s2_ref[...]
    for h in range(MLA_HEADS):
        blk = q[:, h * MLA_QBLK:(h + 1) * MLA_QBLK]
        blk = blk * cq + pltpu.roll(blk, 16, 1) * s1 + pltpu.roll(blk, MLA_QBLK - 16, 1) * s2
        blk = blk.astype(BF16)
        q_out[0, :, h * MLA_QBLK:(h + 1) * MLA_QBLK] = blk
        if absorbed:
            q_lat = jnp.dot(blk[:, MLA_ROPE:MLA_ROPE + MLA_NOPE], wx_ref[h], preferred_element_type=F32)
            x_out[0, h, :, :MLA_KV_LORA] = q_lat.astype(BF16)
            x_out[0, h, :, MLA_KV_LORA:] = blk
    kvt = lax.dot_general(wkvt_ref[...], xn, _NT, preferred_element_type=F32)
    c = kvt[:MLA_KV_LORA]
    ms = jnp.mean(c * c, axis=0, keepdims=True)
    cn = c * lax.rsqrt(ms + EPS) * kvn_ref[...]
    k1 = kvt[MLA_KV_LORA:MLA_KV_LORA + 16]
    k2 = kvt[MLA_KV_LORA + 16:MLA_ROW]
    ct, st = cost_ref[...], sint_ref[...]
    kvt_out[0, :MLA_KV_LORA] = cn
    kvt_out[0, MLA_KV_LORA:MLA_KV_LORA + 16] = k1 * ct - k2 * st
    kvt_out[0, MLA_KV_LORA + 16:] = k1 * st + k2 * ct
    if not absorbed:
        x_out[0] = jnp.dot(wx_ref[...], cn.astype(BF16), preferred_element_type=F32).astype(BF16)


def mla_proj(x3, g, w, tabs, tm, absorbed):
    nb, L, _ = x3.shape
    nl = L // tm
    hq = MLA_HEADS * MLA_QBLK
    if absorbed:
        x_shape = jax.ShapeDtypeStruct((nb, MLA_HEADS, L, MLA_KV_LORA + MLA_QBLK), BF16)
        x_spec = pl.BlockSpec((1, MLA_HEADS, tm, MLA_KV_LORA + MLA_QBLK), lambda b, i: (b, 0, i, 0))
        wx_spec = pl.BlockSpec((MLA_HEADS, MLA_NOPE, MLA_KV_LORA), lambda b, i: (0, 0, 0))
    else:
        x_shape = jax.ShapeDtypeStruct((nb, 2 * MLA_HEADS * MLA_NOPE, L), BF16)
        x_spec = pl.BlockSpec((1, 2 * MLA_HEADS * MLA_NOPE, tm), lambda b, i: (b, 0, i))
        wx_spec = pl.BlockSpec((2 * MLA_HEADS * MLA_NOPE, MLA_KV_LORA), lambda b, i: (0, 0))
    const = lambda b, i: (0, 0)
    return pl.pallas_call(
        functools.partial(_mla_proj_kernel, absorbed=absorbed),
        grid=(nb, nl),
        in_specs=[pl.BlockSpec((1, tm, D_MODEL), lambda b, i: (b, i, 0)),
                  pl.BlockSpec((1, D_MODEL), const),
                  pl.BlockSpec((D_MODEL, MLA_Q_LORA), const),
                  pl.BlockSpec((1, MLA_Q_LORA), const),
                  pl.BlockSpec((MLA_Q_LORA, hq), const),
                  pl.BlockSpec((MLA_ROW, D_MODEL), const),
                  pl.BlockSpec((MLA_KV_LORA, 1), const),
                  wx_spec,
                  pl.BlockSpec((tm, MLA_QBLK), lambda b, i: (i, 0)),
                  pl.BlockSpec((tm, MLA_QBLK), lambda b, i: (i, 0)),
                  pl.BlockSpec((tm, MLA_QBLK), lambda b, i: (i, 0)),
                  pl.BlockSpec((16, tm), lambda b, i: (0, i)),
                  pl.BlockSpec((16, tm), lambda b, i: (0, i))],
        out_specs=[pl.BlockSpec((1, tm, hq), lambda b, i: (b, i, 0)),
                   pl.BlockSpec((1, MLA_ROW, tm), lambda b, i: (b, 0, i)),
                   x_spec],
        out_shape=[jax.ShapeDtypeStruct((nb, L, hq), BF16),
                   jax.ShapeDtypeStruct((nb, MLA_ROW, L), F32),
                   x_shape],
        compiler_params=_cparams(("parallel", "parallel")),
        name="mla_proj_abs" if absorbed else "mla_proj",
    )(x3, g, w["wqa"], w["qn"], w["wqup"], w["wkvt"], w["kvn"], w["wukT"] if absorbed else w["wukvT"],
      tabs["cq"], tabs["s1"], tabs["s2"], tabs["cosT"], tabs["sinT"])


def _mla_attn_prompt_kernel(q_ref, kn_ref, v_ref, kpe_ref, o_ref, kt_scr, vt_scr, *, tq, seq):
    qi = pl.program_id(2)

    @pl.when(qi == 0)
    def _():
        kpe = kpe_ref[0].astype(BF16)
        zeros = jnp.zeros((MLA_QBLK - MLA_ROPE - MLA_NOPE, seq), BF16)
        ones_row = jnp.where(lax.broadcasted_iota(jnp.int32, (MLA_V, seq), 0) == 0, 1.0, 0.0).astype(BF16)
        for hh in range(MLA_HPS):
            kt_scr[hh, 0:MLA_ROPE] = kpe
            kt_scr[hh, MLA_ROPE:MLA_ROPE + MLA_NOPE] = kn_ref[0, hh * MLA_NOPE:(hh + 1) * MLA_NOPE]
            kt_scr[hh, MLA_ROPE + MLA_NOPE:] = zeros
            vt_scr[hh, 0:MLA_V] = v_ref[0, hh * MLA_V:(hh + 1) * MLA_V]
            vt_scr[hh, MLA_V:] = ones_row

    row = lax.broadcasted_iota(jnp.int32, (tq, tq), 0)
    col = lax.broadcasted_iota(jnp.int32, (tq, tq), 1)
    causal = col <= row
    qs = [q_ref[0, :, hh * MLA_QBLK:(hh + 1) * MLA_QBLK] for hh in range(MLA_HPS)]

    def step(j, carry, masked):
        off = pl.multiple_of(j * tq, tq)
        new = []
        for hh in range(MLA_HPS):
            m, acc = carry[hh]
            s = jnp.dot(qs[hh], kt_scr[hh, :, pl.ds(off, tq)], preferred_element_type=F32)
            if masked:
                s = jnp.where(causal, s, NEG)
            m_new = jnp.maximum(m, jnp.max(s, axis=-1, keepdims=True))
            p = jnp.exp2(s - m_new).astype(BF16)
            acc = jnp.exp2(m - m_new) * acc + lax.dot_general(p, vt_scr[hh, :, pl.ds(off, tq)], _NT,
                                                             preferred_element_type=F32)
            new.append((m_new, acc))
        return tuple(new)

    init = (jnp.full((tq, 1), NEG, F32), jnp.zeros((tq, 2 * MLA_V), F32))
    carry = lax.fori_loop(0, qi, functools.partial(step, masked=False), (init,) * MLA_HPS)
    carry = step(qi, carry, True)
    o_ref[0] = jnp.concatenate([acc[:, :MLA_V] / acc[:, MLA_V:MLA_V + 1] for _, acc in carry],
                               axis=-1).astype(BF16)


def mla_attn_prompt(q, knv, kvt, tq):
    b, s, _ = q.shape
    hp = MLA_HEADS // MLA_HPS
    return pl.pallas_call(
        functools.partial(_mla_attn_prompt_kernel, tq=tq, seq=s),
        grid=(b, hp, s // tq),
        in_specs=[pl.BlockSpec((1, tq, MLA_HPS * MLA_QBLK), lambda bi, h, i: (bi, i, h)),
                  pl.BlockSpec((1, MLA_HPS * MLA_NOPE, s), lambda bi, h, i: (bi, h, 0)),
                  pl.BlockSpec((1, MLA_HPS * MLA_V, s), lambda bi, h, i: (bi, hp + h, 0)),
                  pl.BlockSpec((1, MLA_ROPE, s), lambda bi, h, i: (bi, MLA_KV_LORA // MLA_ROPE, 0))],
        out_specs=pl.BlockSpec((1, tq, MLA_HPS * MLA_V), lambda bi, h, i: (bi, i, h)),
        out_shape=jax.ShapeDtypeStruct((b, s, MLA_HEADS * MLA_V), BF16),
        scratch_shapes=[pltpu.VMEM((MLA_HPS, MLA_QBLK, s), BF16), pltpu.VMEM((MLA_HPS, 2 * MLA_V, s), BF16)],
        compiler_params=_cparams(("parallel", "parallel", "arbitrary")),
        name="mla_attn_prompt",
    )(q, knv, knv, kvt)


def _mla_attn_sample_kernel(pt_ref, qc_ref, *refs, ppc, t_new):
    page_refs = refs[:ppc]
    kvn_ref, wuv_ref, o_ref, kt_scr, m_scr, l_scr, acc_scr = refs[ppc:]
    c = pl.program_id(1)
    rows = MLA_HEADS * t_new

    @pl.when(c == 0)
    def _():
        m_scr[...] = jnp.full((rows, 1), NEG, F32)
        l_scr[...] = jnp.zeros((rows, 1), F32)
        acc_scr[...] = jnp.zeros((rows, MLA_KV_LORA), F32)

    for i in range(ppc):
        kt_scr[:, i * PAGE:(i + 1) * PAGE] = page_refs[i][0, 0].astype(BF16)
    q = qc_ref[0][:, :MLA_ROW]
    s = jnp.dot(q, kt_scr[...], preferred_element_type=F32)
    p, alpha, m, l = _softmax_update(s, m_scr[...], l_scr[...])
    acc = alpha * acc_scr[...] + lax.dot_general(p.astype(BF16), kt_scr[:MLA_KV_LORA, :], _NT,
                                                 preferred_element_type=F32)
    m_scr[...] = m
    l_scr[...] = l
    acc_scr[...] = acc

    @pl.when(c == pl.num_programs(1) - 1)
    def _():
        kn = kvn_ref[0].astype(BF16)
        s2 = jnp.dot(q, kn, preferred_element_type=F32)
        tq = lax.broadcasted_iota(jnp.int32, s2.shape, 0) % t_new
        tk = lax.broadcasted_iota(jnp.int32, s2.shape, 1)
        s2 = jnp.where(tk <= tq, s2, NEG)
        p2, alpha2, m2, l2 = _softmax_update(s2, m, l)
        acc2 = alpha2 * acc + lax.dot_general(p2.astype(BF16), kn[:MLA_KV_LORA], _NT, preferred_element_type=F32)
        o_lat = _safe_div(acc2, l2).astype(BF16)
        outs = [jnp.dot(o_lat[h * t_new:(h + 1) * t_new], wuv_ref[h], preferred_element_type=F32)
                for h in range(MLA_HEADS)]
        o_ref[0] = jnp.concatenate(outs, axis=-1)


def mla_attn_sample(page_table, qc, cache_t, layer, kv_new, wuv, ppc):
    db, rows, _ = qc.shape
    t_new = rows // MLA_HEADS
    n_pages = page_table.shape[1]
    pt = page_table.reshape(-1)

    def page_spec(k):
        return pl.BlockSpec((1, 1, MLA_ROW, PAGE),
                            lambda b, c, pt_ref: (layer, pt_ref[b * n_pages + c * ppc + k], 0, 0))

    grid_spec = pltpu.PrefetchScalarGridSpec(
        num_scalar_prefetch=1,
        grid=(db, n_pages // ppc),
        in_specs=[pl.BlockSpec((1, rows, qc.shape[2]), lambda b, c, pt_ref: (b, 0, 0))]
                 + [page_spec(k) for k in range(ppc)]
                 + [pl.BlockSpec((1, MLA_ROW, PAGE), lambda b, c, pt_ref: (b, 0, 0)),
                    pl.BlockSpec((MLA_HEADS, MLA_KV_LORA, MLA_V), lambda b, c, pt_ref: (0, 0, 0))],
        out_specs=pl.BlockSpec((1, t_new, MLA_HEADS * MLA_V), lambda b, c, pt_ref: (b, 0, 0)),
        scratch_shapes=[pltpu.VMEM((MLA_ROW, ppc * PAGE), BF16),
                        pltpu.VMEM((rows, 1), F32), pltpu.VMEM((rows, 1), F32),
                        pltpu.VMEM((rows, MLA_KV_LORA), F32)],
    )
    return pl.pallas_call(
        functools.partial(_mla_attn_sample_kernel, ppc=ppc, t_new=t_new),
        grid_spec=grid_spec,
        out_shape=jax.ShapeDtypeStruct((db, t_new, MLA_HEADS * MLA_V), F32),
        compiler_params=_cparams(("parallel", "arbitrary")),
        name="mla_attn_sample",
    )(pt, qc, *([cache_t] * ppc), kv_new, wuv)


def _ffn_kernel(*refs, tm, tf, tiles_per_seq, sample, final):
    x_ref, g_ref, wg_ref, wu_ref, cw_ref, cb_ref, wo_ref = refs[:7]
    k = 7
    if sample:
        p1_ref, p2_ref = refs[k:k + 2]
        k += 2
    if final:
        gf_ref = refs[k]
        k += 1
    y_ref, cv_ref, gs_scr, carry_scr = refs[k:]
    i = pl.program_id(0)
    ns = D_FF // tf

    if not sample:
        @pl.when((i % tiles_per_seq) == 0)
        def _():
            carry_scr[...] = jnp.zeros((ns, 8, tf), F32)

    x = x_ref[...]
    xn = _rms_rows(x, g_ref[...]).astype(BF16)
    y = x
    for cs in range(ns):
        cols = slice(cs * tf, (cs + 1) * tf)
        g = jnp.dot(xn, wg_ref[:, cols], preferred_element_type=F32)
        u = jnp.dot(xn, wu_ref[:, cols], preferred_element_type=F32)
        gs_scr[cs, 8:8 + tm] = g
        if sample:
            gs_scr[cs, 0:8] = jnp.zeros((8, tf), F32)
            rin = lax.broadcasted_iota(jnp.int32, (tm, 1), 0) % 8
            g1 = jnp.where(rin < 1, p1_ref[:, cols], gs_scr[cs, 7:7 + tm])
            g2 = jnp.where(rin < 2, p2_ref[:, cols], gs_scr[cs, 6:6 + tm])
            cv_ref[:, cols] = g
        else:
            gs_scr[cs, 0:8] = carry_scr[cs]
            g1 = gs_scr[cs, 7:7 + tm]
            g2 = gs_scr[cs, 6:6 + tm]
            carry_scr[cs] = gs_scr[cs, tm:tm + 8]
            cv_ref[0, :, cols] = g[tm - 2:tm]
        gc = cb_ref[:, cols] + cw_ref[0:1, cols] * g2 + cw_ref[1:2, cols] * g1 + cw_ref[2:3, cols] * g
        act = (gc * jax.nn.sigmoid(gc) * u).astype(BF16)
        y = y + jnp.dot(act, wo_ref[cols, :], preferred_element_type=F32)
    if final:
        y = _rms_rows(y, gf_ref[...])
    y_ref[...] = y


def conv_ffn(x, g, w_in, cw, cb, w_out, *, tm, tf, rows_per_seq, prev=None, final_g=None):
    n = x.shape[0]
    ns = D_FF // tf
    sample = prev is not None
    final = final_g is not None
    fixed = lambda shape, idx: pl.BlockSpec(shape, lambda i: idx, pipeline_mode=pl.Buffered(1))
    in_specs = [pl.BlockSpec((tm, D_MODEL), lambda i: (i, 0)),
                fixed((1, D_MODEL), (0, 0)),
                fixed((D_MODEL, D_FF), (0, 0)),
                fixed((D_MODEL, D_FF), (0, 1)),
                fixed((3, D_FF), (0, 0)),
                fixed((1, D_FF), (0, 0)),
                fixed((D_FF, D_MODEL), (0, 0))]
    args = [x, g, w_in, w_in, cw, cb, w_out]
    if sample:
        in_specs += [pl.BlockSpec((tm, D_FF), lambda i: (i, 0))] * 2
        args += list(prev)
        cv_shape = jax.ShapeDtypeStruct((n, D_FF), F32)
        cv_spec = pl.BlockSpec((tm, D_FF), lambda i: (i, 0))
        tiles_per_seq = 1
    else:
        tiles_per_seq = rows_per_seq // tm
        cv_shape = jax.ShapeDtypeStruct((n // tm, 2, D_FF), F32)
        cv_spec = pl.BlockSpec((1, 2, D_FF), lambda i: (i, 0, 0))
    if final:
        in_specs.append(fixed((1, D_MODEL), (0, 0)))
        args.append(final_g)
    return pl.pallas_call(
        functools.partial(_ffn_kernel, tm=tm, tf=tf, tiles_per_seq=tiles_per_seq, sample=sample, final=final),
        grid=(n // tm,),
        in_specs=in_specs,
        out_specs=[pl.BlockSpec((tm, D_MODEL), lambda i: (i, 0)), cv_spec],
        out_shape=[jax.ShapeDtypeStruct((n, D_MODEL), F32), cv_shape],
        scratch_shapes=[pltpu.VMEM((ns, tm + 8, tf), F32),
                        pltpu.VMEM((ns, 8, tf), F32)],
        compiler_params=_cparams(("arbitrary",)),
        name="conv_ffn_sample" if sample else "conv_ffn",
    )(*args)


def _rope_rows(kvt, base, ct, st, out, half):
    for gg in range(NSA_GROUPS):
        r0 = base + gg * NSA_DH
        x1 = kvt[r0:r0 + half]
        x2 = kvt[r0 + half:r0 + NSA_DH]
        out[0, gg * NSA_DH:gg * NSA_DH + half] = x1 * ct - x2 * st
        out[0, gg * NSA_DH + half:(gg + 1) * NSA_DH] = x1 * st + x2 * ct


def _nsa_proj_kernel(x_ref, g_ref, wq_ref, wg_ref, bg_ref, wkvt_ref, cq_ref, s1_ref, s2_ref, cost_ref, sint_ref,
                     q_out, gate_out, kc_out, ks_out, kw_out):
    half = NSA_DH // 2
    xn = _rms_rows(x_ref[0], g_ref[...]).astype(BF16)
    q = jnp.dot(xn, wq_ref[...], preferred_element_type=F32)
    cq, s1, s2 = cq_ref[...], s1_ref[...], s2_ref[...]
    for v in range(NSA_HEADS * NSA_DH // 128):
        blk = q[:, v * 128:(v + 1) * 128]
        blk = blk * cq + pltpu.roll(blk, half, 1) * s1 + pltpu.roll(blk, 128 - half, 1) * s2
        q_out[0, :, v * 128:(v + 1) * 128] = blk.astype(BF16)
    gl = jnp.dot(xn, wg_ref[...], preferred_element_type=F32) + bg_ref[...]
    gate_out[0] = jax.nn.sigmoid(gl)
    kvt = lax.dot_general(wkvt_ref[...], xn, _NT, preferred_element_type=F32)
    ct, st = cost_ref[...], sint_ref[...]
    kw = NSA_GROUPS * NSA_DH
    for br, out in enumerate((kc_out, ks_out, kw_out)):
        _rope_rows(kvt, br * NSA_KV, ct, st, out, half)
        out[0, kw:] = kvt[br * NSA_KV + kw:(br + 1) * NSA_KV]


def nsa_proj(x3, g, w, tabs, tm):
    nb, L, _ = x3.shape
    const = lambda b, i: (0, 0)
    hq = NSA_HEADS * NSA_DH
    kv_spec = pl.BlockSpec((1, NSA_KV, tm), lambda b, i: (b, 0, i))
    kv_shape = jax.ShapeDtypeStruct((nb, NSA_KV, L), F32)
    return pl.pallas_call(
        _nsa_proj_kernel,
        grid=(nb, L // tm),
        in_specs=[pl.BlockSpec((1, tm, D_MODEL), lambda b, i: (b, i, 0)),
                  pl.BlockSpec((1, D_MODEL), const),
                  pl.BlockSpec((D_MODEL, hq), const),
                  pl.BlockSpec((D_MODEL, 128), const),
                  pl.BlockSpec((1, 128), const),
                  pl.BlockSpec((3 * NSA_KV, D_MODEL), const),
                  pl.BlockSpec((tm, 128), lambda b, i: (i, 0)),
                  pl.BlockSpec((tm, 128), lambda b, i: (i, 0)),
                  pl.BlockSpec((tm, 128), lambda b, i: (i, 0)),
                  pl.BlockSpec((NSA_DH // 2, tm), lambda b, i: (0, i)),
                  pl.BlockSpec((NSA_DH // 2, tm), lambda b, i: (0, i))],
        out_specs=[pl.BlockSpec((1, tm, hq), lambda b, i: (b, i, 0)),
                   pl.BlockSpec((1, tm, 128), lambda b, i: (b, i, 0)),
                   kv_spec, kv_spec, kv_spec],
        out_shape=[jax.ShapeDtypeStruct((nb, L, hq), BF16),
                   jax.ShapeDtypeStruct((nb, L, 128), F32),
                   kv_shape, kv_shape, kv_shape],
        compiler_params=_cparams(("parallel", "parallel")),
        name="nsa_proj",
    )(x3, g, w["wq"], w["wg"], w["bg"], w["wkvt"], tabs["cq"], tabs["s1"], tabs["s2"], tabs["cosT"], tabs["sinT"])


def _cmp_bias_kernel(pe_ref, w1_ref, o_ref):
    o_ref[0] = jnp.sum(w1_ref[0] * pe_ref[0], axis=0, keepdims=True)


def cmp_bias(pe_col, w1_flat):
    n = pe_col.shape[1]
    return pl.pallas_call(
        _cmp_bias_kernel,
        grid=(2,),
        in_specs=[pl.BlockSpec((1, n, 1), lambda c: (c, 0, 0)),
                  pl.BlockSpec((1, n, CMP_HID), lambda c: (c, 0, 0))],
        out_specs=pl.BlockSpec((1, 1, CMP_HID), lambda c: (c, 0, 0)),
        out_shape=jax.ShapeDtypeStruct((2, 1, CMP_HID), F32),
        name="cmp_bias",
    )(pe_col, w1_flat)


def _compress_kernel(*refs, npg, paged):
    if paged:
        refs = refs[1:]
    page_refs = refs[:npg]
    nv = NSA_KV // 128
    w1_ref, bias_ref, w2_ref, out_ref = refs[npg:npg + 4]
    xt_scrs = refs[npg + 4:npg + 4 + nv]
    fs_scr = refs[npg + 4 + nv]
    c = pl.program_id(1)
    nchunk = npg * PAGE // CMP_STRIDE
    cpp = PAGE // CMP_STRIDE

    @pl.when(c == 0)
    def _():
        fs_scr[:, 0:8] = jnp.zeros((2 * NSA_GROUPS, 8, CMP_HID), F32)

    for v in range(nv):
        cc = v // (nv // 2)
        xt_scr = xt_scrs[v]
        for p in range(npg):
            page = page_refs[p][0, 0, v * 128:(v + 1) * 128] if paged else page_refs[p][0, v * 128:(v + 1) * 128]
            page_t = page.T
            for j in range(cpp):
                r0 = (p * cpp + j) * CMP_PITCH
                xt_scr[r0:r0 + CMP_STRIDE, :] = page_t[j * CMP_STRIDE:(j + 1) * CMP_STRIDE]
        both = jnp.zeros((nchunk, 4 * CMP_HID), F32)
        for lp in range(CMP_STRIDE // 2):
            x2 = jnp.concatenate([xt_scr[pl.ds(2 * lp + i, nchunk, stride=CMP_PITCH), :] for i in range(2)],
                                 axis=1).astype(BF16)
            both = both + jnp.dot(x2, w1_ref[cc, lp], preferred_element_type=F32)
        for u in range(2):
            cg = 2 * v + u
            acc = both[:, u * 2 * CMP_HID:(u + 1) * 2 * CMP_HID]
            fs_scr[cg, 8:8 + nchunk] = acc[:, :CMP_HID]
            h = jax.nn.gelu(fs_scr[cg, 7:7 + nchunk] + acc[:, CMP_HID:] + bias_ref[cc])
            fs_scr[cg, 0:8] = fs_scr[cg, nchunk:nchunk + 8]
            out_ref[0, cg] = jnp.dot(h.astype(BF16), w2_ref[cc], preferred_element_type=F32).astype(BF16)


def compress(kv_src, w, *, npg=16, page_table=None, layer=None):
    paged = page_table is not None
    if paged:
        nb, n_pages = page_table.shape
        pt = page_table.reshape(-1)
        specs = [pl.BlockSpec((1, 1, NSA_KV, PAGE),
                              (lambda b, c, pt_ref, k=k: (layer, pt_ref[b * n_pages + c * npg + k], 0, 0)))
                 for k in range(npg)]
        imap = lambda f: (lambda b, c, pt_ref: f(b, c))
    else:
        nb = kv_src.shape[0]
        n_pages = kv_src.shape[2] // PAGE
        specs = [pl.BlockSpec((1, NSA_KV, PAGE), (lambda b, c, k=k: (b, 0, c * npg + k))) for k in range(npg)]
        imap = lambda f: f
    nchunk = npg * PAGE // CMP_STRIDE
    in_specs = specs + [pl.BlockSpec((2, CMP_STRIDE // 2, 4 * NSA_DH, 4 * CMP_HID), imap(lambda b, c: (0, 0, 0, 0))),
                        pl.BlockSpec((2, 1, CMP_HID), imap(lambda b, c: (0, 0, 0))),
                        pl.BlockSpec((2, CMP_HID, NSA_DH), imap(lambda b, c: (0, 0, 0)))]
    out_spec = pl.BlockSpec((1, 2 * NSA_GROUPS, nchunk, NSA_DH), imap(lambda b, c: (b, 0, c, 0)))
    scratch = [pltpu.VMEM((nchunk * CMP_PITCH, 128), F32)] * (NSA_KV // 128) \
        + [pltpu.VMEM((2 * NSA_GROUPS, nchunk + 8, CMP_HID), F32)]
    grid = (nb, n_pages // npg)
    out_shape = jax.ShapeDtypeStruct((nb, 2 * NSA_GROUPS, n_pages * PAGE // CMP_STRIDE, NSA_DH), BF16)
    kern = functools.partial(_compress_kernel, npg=npg, paged=paged)
    args = [kv_src] * npg + [w["w1big"], w["cbias"], w["w2"]]
    if paged:
        gs = pltpu.PrefetchScalarGridSpec(num_scalar_prefetch=1, grid=grid, in_specs=in_specs,
                                          out_specs=out_spec, scratch_shapes=scratch)
        return pl.pallas_call(kern, grid_spec=gs, out_shape=out_shape,
                              compiler_params=_cparams(("parallel", "arbitrary")), name="compress_paged")(pt, *args)
    return pl.pallas_call(kern, grid=grid, in_specs=in_specs, out_specs=out_spec, out_shape=out_shape,
                          scratch_shapes=scratch, compiler_params=_cparams(("parallel", "arbitrary")),
                          name="compress")(*args)


def _masked_softmax(s, valid):
    s = jnp.where(valid, s, NEG)
    m = jnp.max(s, axis=-1, keepdims=True)
    p = jnp.where(valid, jnp.exp2(s - m), 0.0)
    den = jnp.sum(p, axis=-1, keepdims=True)
    return p / jnp.where(den > 0, den, 1.0)


def _nsa_attn_prompt_kernel(q_ref, gate_ref, kc_ref, vc_ref, kst_ref, vst_ref, kwt_ref, vwt_ref, cover_ref, e_ref,
                            o_ref, ks_scr, vs_scr, kw_scr, vw_scr, *, tq, seq):
    i = pl.program_id(2)
    t0 = i * tq
    n_blk = seq // SLC_BLOCK

    @pl.when(i == 0)
    def _():
        ones_row = jnp.where(lax.broadcasted_iota(jnp.int32, (NSA_DH, seq), 0) == 0, 1.0, 0.0).astype(BF16)
        ks_scr[0:NSA_DH] = kst_ref[0].astype(BF16)
        ks_scr[NSA_DH:] = e_ref[...]
        vs_scr[0:NSA_DH] = vst_ref[0].astype(BF16)
        vs_scr[NSA_DH:] = ones_row
        kw_scr[...] = kwt_ref[0].astype(BF16)
        vw_scr[0:NSA_DH] = vwt_ref[0].astype(BF16)
        vw_scr[NSA_DH:] = ones_row

    q4 = jnp.concatenate([q_ref[0, :, h * NSA_DH:(h + 1) * NSA_DH] for h in range(NSA_HPG)], axis=0)
    rows = NSA_HPG * tq
    t_row = t0 + lax.broadcasted_iota(jnp.int32, (tq, 1), 0)
    t_row4 = t0 + lax.broadcasted_iota(jnp.int32, (rows, 1), 0) % tq

    def weighted(p, vt):
        return lax.dot_general(p.astype(BF16), vt, _NT, preferred_element_type=F32)

    ci = lax.broadcasted_iota(jnp.int32, (rows, kc_ref.shape[2]), 1)
    valid_c = (ci >= 1) & (CMP_STRIDE * ci + CMP_STRIDE - 1 <= t_row4)
    s_c = lax.dot_general(q4, kc_ref[0, 0], _NT, preferred_element_type=F32)
    p_c = _masked_softmax(s_c, valid_c)
    o_cmp = jnp.dot(p_c.astype(BF16), vc_ref[0, 0], preferred_element_type=F32)

    p_sum = p_c[0:tq]
    for h in range(1, NSA_HPG):
        p_sum = p_sum + p_c[h * tq:(h + 1) * tq]
    nq = 128 // n_blk
    rq = tq // nq
    p_wide = jnp.concatenate([p_sum[u * rq:(u + 1) * rq] for u in range(nq)], axis=1)
    imp = jnp.dot(p_wide, cover_ref[...], preferred_element_type=F32, precision=lax.Precision.HIGHEST)
    lane = lax.broadcasted_iota(jnp.int32, (rq, 128), 1)
    blk = lane % n_blk
    t_pk = t0 + (lane // n_blk) * rq + lax.broadcasted_iota(jnp.int32, (rq, 128), 0)
    cur = t_pk // SLC_BLOCK
    forced = (blk == 0) | ((blk <= cur) & (blk > cur - 2))
    score = jnp.where(blk * SLC_BLOCK <= t_pk, imp + jnp.where(forced, FORCE_BONUS, 0.0), NEG)
    cnt = jnp.zeros((rq, 128), F32)
    for k in range(1, n_blk):
        other = jnp.where(blk >= k, pltpu.roll(score, k, 1), pltpu.roll(score, 128 - n_blk + k, 1))
        beats = (other > score) | ((other == score) & (blk >= k))
        cnt = cnt + jnp.where(beats, 1.0, 0.0)
    sel_pk = jnp.where(cnt < N_SELECT, 0.0, NEG)
    sel_bias = jnp.concatenate([sel_pk[:, u * n_blk:(u + 1) * n_blk] for u in range(nq)], axis=0).astype(BF16)
    q_aug = jnp.concatenate([q4, jnp.concatenate([sel_bias] * NSA_HPG, axis=0)], axis=1)

    d_iota = lax.broadcasted_iota(jnp.int32, (tq, tq), 0) - lax.broadcasted_iota(jnp.int32, (tq, tq), 1)
    causal_bias = jnp.where(d_iota >= 0, 0.0, NEG)

    def slc_step(j, carry, diagonal):
        m, acc = carry
        off = pl.multiple_of(j * tq, tq)
        s = jnp.dot(q_aug, ks_scr[:, pl.ds(off, tq)], preferred_element_type=F32)
        if diagonal:
            s = (s.reshape(NSA_HPG, tq, tq) + causal_bias[None]).reshape(rows, tq)
        m_new = jnp.maximum(m, jnp.max(s, axis=-1, keepdims=True))
        alpha = jnp.exp2(m - m_new)
        acc = alpha * acc + weighted(jnp.exp2(s - m_new), vs_scr[:, pl.ds(off, tq)])
        return m_new, acc

    init = (jnp.full((rows, 1), NEG, F32), jnp.zeros((rows, 2 * NSA_DH), F32))
    carry = lax.fori_loop(0, i, functools.partial(slc_step, diagonal=False), init)
    _, acc_s = slc_step(i, carry, True)
    o_slc = acc_s[:, :NSA_DH] / acc_s[:, NSA_DH:NSA_DH + 1]

    span = WINDOW + tq
    w_off = pl.multiple_of(jnp.maximum(t0 - WINDOW, 0), tq)
    s_w = jnp.dot(q4, kw_scr[:, pl.ds(w_off, span)], preferred_element_type=F32)
    d = t_row - (w_off + lax.broadcasted_iota(jnp.int32, (tq, span), 1))
    bias_w = jnp.where((d >= 0) & (d < WINDOW), 0.0, NEG)
    s_w = (s_w.reshape(NSA_HPG, tq, span) + bias_w[None]).reshape(rows, span)
    acc_w = weighted(jnp.exp2(s_w - jnp.max(s_w, axis=-1, keepdims=True)), vw_scr[:, pl.ds(w_off, span)])
    o_win = acc_w[:, :NSA_DH] / acc_w[:, NSA_DH:NSA_DH + 1]

    outs = []
    gate = gate_ref[0, 0]
    for h in range(NSA_HPG):
        r = slice(h * tq, (h + 1) * tq)
        outs.append(gate[:, 3 * h:3 * h + 1] * o_cmp[r] + gate[:, 3 * h + 1:3 * h + 2] * o_slc[r]
                    + gate[:, 3 * h + 2:3 * h + 3] * o_win[r])
    o_ref[0] = jnp.concatenate(outs, axis=-1).astype(BF16)


def nsa_attn_prompt(q, gates_g, cmp_kv, kst, kwt, cover, e_mat, tq=512):
    b, s, _ = q.shape
    g = NSA_GROUPS
    gw = NSA_HPG * NSA_DH
    ncmp = cmp_kv.shape[2]
    n_blk = s // SLC_BLOCK
    kt_spec = lambda off: pl.BlockSpec((1, NSA_DH, s), lambda bi, gi, i: (bi, off + gi, 0))
    return pl.pallas_call(
        functools.partial(_nsa_attn_prompt_kernel, tq=tq, seq=s),
        grid=(b, g, s // tq),
        in_specs=[pl.BlockSpec((1, tq, gw), lambda bi, gi, i: (bi, i, gi)),
                  pl.BlockSpec((1, 1, tq, 3 * NSA_HPG), lambda bi, gi, i: (bi, gi, i, 0)),
                  pl.BlockSpec((1, 1, ncmp, NSA_DH), lambda bi, gi, i: (bi, gi, 0, 0)),
                  pl.BlockSpec((1, 1, ncmp, NSA_DH), lambda bi, gi, i: (bi, g + gi, 0, 0)),
                  kt_spec(0), kt_spec(g), kt_spec(0), kt_spec(g),
                  pl.BlockSpec(cover.shape, lambda bi, gi, i: (0, 0)),
                  pl.BlockSpec((n_blk, s), lambda bi, gi, i: (0, 0))],
        out_specs=pl.BlockSpec((1, tq, gw), lambda bi, gi, i: (bi, i, gi)),
        out_shape=jax.ShapeDtypeStruct((b, s, NSA_HEADS * NSA_DH), BF16),
        scratch_shapes=[pltpu.VMEM((NSA_DH + n_blk, s), BF16), pltpu.VMEM((2 * NSA_DH, s), BF16),
                        pltpu.VMEM((NSA_DH, s), BF16), pltpu.VMEM((2 * NSA_DH, s), BF16)],
        compiler_params=_cparams(("parallel", "parallel", "arbitrary")),
        name="nsa_attn_prompt",
    )(q, gates_g, cmp_kv, cmp_kv, kst, kst, kwt, kwt, cover, e_mat)


def _nsa_attn_sample_kernel(pt_ref, q_ref, qbd_ref, gate_ref, ckv_ref, cover_ref, *refs, ppc, t_new, past,
                            n_slc_pad):
    page_refs = refs[:ppc]
    (ksn_ref, win_ref, kwn_ref, e_ref, o_ref,
     kv_scr, sel_scr, ocmp_scr, m_scr, l_scr, acc_scr) = refs[ppc:]
    c = pl.program_id(1)
    g = NSA_GROUPS
    rg = NSA_HPG * t_new
    rows = g * rg
    kw = g * NSA_DH
    nk = ppc * PAGE
    bpc = nk // SLC_BLOCK
    tt_g = lax.broadcasted_iota(jnp.int32, (rg, 1), 0) % t_new
    tt8 = lax.broadcasted_iota(jnp.int32, (t_new, 1), 0)

    @pl.when(c == 0)
    def _():
        p_sums = []
        for gg in range(g):
            qg = q_ref[0, gg * rg:(gg + 1) * rg]
            s = lax.dot_general(qg, ckv_ref[0, gg], _NT, preferred_element_type=F32)
            ci = lax.broadcasted_iota(jnp.int32, s.shape, 1)
            valid = (ci >= 1) & (CMP_STRIDE * ci + CMP_STRIDE - 1 <= past + tt_g)
            p = _masked_softmax(s, valid)
            ocmp_scr[gg * rg:(gg + 1) * rg] = jnp.dot(p.astype(BF16), ckv_ref[0, g + gg], preferred_element_type=F32)
            p_sum = p[0:t_new]
            for h in range(1, NSA_HPG):
                p_sum = p_sum + p[h * t_new:(h + 1) * t_new]
            p_sums.append(p_sum)
        imp = jnp.dot(jnp.concatenate(p_sums, axis=0), cover_ref[...], preferred_element_type=F32,
                      precision=lax.Precision.HIGHEST)
        blk = lax.broadcasted_iota(jnp.int32, imp.shape, 1)
        tpos = past + lax.broadcasted_iota(jnp.int32, (g * t_new, 1), 0) % t_new
        cur = tpos // SLC_BLOCK
        forced = (blk == 0) | ((blk <= cur) & (blk > cur - 2))
        score = jnp.where(blk * SLC_BLOCK <= tpos, imp + jnp.where(forced, FORCE_BONUS, 0.0), NEG)
        n_slc = (past + t_new + SLC_BLOCK - 1) // SLC_BLOCK
        gone = jnp.float32(-3e38)
        score = jnp.where(blk < n_slc, score, gone)
        chosen = blk < 0
        for k in range(N_SELECT):
            mx = jnp.max(score, axis=-1, keepdims=True)
            pick = jnp.min(jnp.where(score == mx, blk, n_slc_pad), axis=-1, keepdims=True)
            chosen = chosen | (blk == pick)
            score = jnp.where(blk == pick, gone, score)
        sel = jnp.where(chosen, 0.0, NEG)
        for cb in range(n_slc_pad // bpc):
            sel_scr[cb] = sel[:, cb * bpc:(cb + 1) * bpc]
        m_scr[...] = jnp.full((rows, 1), NEG, F32)
        l_scr[...] = jnp.zeros((rows, 1), F32)
        acc_scr[...] = jnp.zeros((rows, kw), F32)

    def chosen_bias(cb, lo, hi):
        return jnp.dot(sel_scr[cb].astype(BF16), e_ref[:, lo:hi], preferred_element_type=F32)

    def update(s, bias4, vt):
        nk = s.shape[-1]
        s = (s.reshape(g, NSA_HPG, t_new, nk) + bias4).reshape(rows, nk)
        p, alpha, m, l = _softmax_update(s, m_scr[...], l_scr[...])
        acc_scr[...] = alpha * acc_scr[...] + lax.dot_general(p.astype(BF16), vt, _NT, preferred_element_type=F32)
        m_scr[...] = m
        l_scr[...] = l

    def own_group(acc):
        rgrp = lax.broadcasted_iota(jnp.int32, (rows, 1), 0) // rg
        out = jnp.zeros((rows, NSA_DH), F32)
        for gg in range(g):
            out = out + jnp.where(rgrp == gg, acc[:, gg * NSA_DH:(gg + 1) * NSA_DH], 0.0)
        return out

    for i in range(ppc):
        kv_scr[:, i * PAGE:(i + 1) * PAGE] = page_refs[i][0, 0].astype(BF16)
    qbd = qbd_ref[0]
    update(jnp.dot(qbd, kv_scr[0:kw, :], preferred_element_type=F32),
           chosen_bias(c, 0, nk).reshape(g, 1, t_new, nk), kv_scr[kw:, :])

    @pl.when(c == pl.num_programs(1) - 1)
    def _():
        ksn = ksn_ref[0].astype(BF16)
        kn = lax.broadcasted_iota(jnp.int32, (t_new, PAGE), 1)
        kn_g = lax.broadcasted_iota(jnp.int32, (g * t_new, PAGE), 1)
        tt_gt = lax.broadcasted_iota(jnp.int32, (g * t_new, 1), 0) % t_new
        tail_ok = (kn_g <= tt_gt) & (kn_g < t_new)
        tail_bias = jnp.where(tail_ok, chosen_bias(past // nk, 0, PAGE), NEG)
        update(jnp.dot(qbd, ksn[0:kw], preferred_element_type=F32), tail_bias.reshape(g, 1, t_new, PAGE), ksn[kw:])
        o_slc = _safe_div(own_group(acc_scr[...]), l_scr[...])

        m_scr[...] = jnp.full((rows, 1), NEG, F32)
        l_scr[...] = jnp.zeros((rows, 1), F32)
        acc_scr[...] = jnp.zeros((rows, kw), F32)
        lw = win_ref.shape[3]
        wk = win_ref[0, 0].astype(BF16)
        kwn = kwn_ref[0].astype(BF16)
        wi = lax.broadcasted_iota(jnp.int32, (t_new, lw), 1)
        d_old = tt8 + lw - wi
        bias_old = jnp.where((d_old >= 0) & (d_old < WINDOW), 0.0, NEG)
        d_new = tt8 - kn
        bias_new = jnp.where((d_new >= 0) & (d_new < WINDOW) & (kn < t_new), 0.0, NEG)
        update(jnp.dot(qbd, wk[0:kw], preferred_element_type=F32), bias_old.reshape(1, 1, t_new, lw), wk[kw:])
        update(jnp.dot(qbd, kwn[0:kw], preferred_element_type=F32), bias_new.reshape(1, 1, t_new, PAGE), kwn[kw:])
        o_win = _safe_div(own_group(acc_scr[...]), l_scr[...])
        gate = gate_ref[0]
        o_ref[0] = gate[:, 0:1] * ocmp_scr[...] + gate[:, 1:2] * o_slc + gate[:, 2:3] * o_win


def nsa_attn_sample(page_table, q, qbd, gates, cmp_kv, cover, cache_t, layer, ks_new, win_t, kw_new, ppc):
    db, rows, _ = q.shape
    t_new = rows // NSA_HEADS
    n_pages = page_table.shape[1]
    past = n_pages * PAGE
    ncmp = cmp_kv.shape[2]
    n_slc_pad = cover.shape[1]
    lw = win_t.shape[3]
    pt = page_table.reshape(-1)
    nk = ppc * PAGE
    bpc = nk // SLC_BLOCK
    e_mat = (jnp.arange(bpc)[:, None] == jnp.arange(nk)[None, :] // SLC_BLOCK).astype(BF16)

    def page_spec(k):
        return pl.BlockSpec((1, 1, NSA_KV, PAGE),
                            lambda b, c, pt_ref: (layer, pt_ref[b * n_pages + c * ppc + k], 0, 0))

    per_b = lambda *blk: pl.BlockSpec((1,) + blk, lambda b, c, pt_ref: (b,) + (0,) * len(blk))
    grid_spec = pltpu.PrefetchScalarGridSpec(
        num_scalar_prefetch=1,
        grid=(db, n_pages // ppc),
        in_specs=[per_b(rows, NSA_DH), per_b(rows, NSA_GROUPS * NSA_DH), per_b(rows, 3),
                  per_b(2 * NSA_GROUPS, ncmp, NSA_DH),
                  pl.BlockSpec((ncmp, n_slc_pad), lambda b, c, pt_ref: (0, 0))]
                 + [page_spec(k) for k in range(ppc)]
                 + [per_b(NSA_KV, PAGE),
                    pl.BlockSpec((1, 1, NSA_KV, lw), lambda b, c, pt_ref: (layer, b, 0, 0)),
                    per_b(NSA_KV, PAGE),
                    pl.BlockSpec((bpc, nk), lambda b, c, pt_ref: (0, 0))],
        out_specs=per_b(rows, NSA_DH),
        scratch_shapes=[pltpu.VMEM((NSA_KV, ppc * PAGE), BF16),
                        pltpu.VMEM((n_slc_pad // bpc, NSA_GROUPS * t_new, bpc), F32),
                        pltpu.VMEM((rows, NSA_DH), F32),
                        pltpu.VMEM((rows, 1), F32), pltpu.VMEM((rows, 1), F32),
                        pltpu.VMEM((rows, NSA_GROUPS * NSA_DH), F32)],
    )
    return pl.pallas_call(
        functools.partial(_nsa_attn_sample_kernel, ppc=ppc, t_new=t_new, past=past, n_slc_pad=n_slc_pad),
        grid_spec=grid_spec,
        out_shape=jax.ShapeDtypeStruct((db, rows, NSA_DH), F32),
        compiler_params=_cparams(("parallel", "arbitrary")),
        name="nsa_attn_sample",
    )(pt, q, qbd, gates, cmp_kv, cover, *([cache_t] * ppc), ks_new, win_t, kw_new, e_mat)


def _rope_cos_sin(pos, d):
    inv = ROPE_THETA ** (-jnp.arange(0, d, 2, dtype=F32) / d)
    ang = pos.astype(F32)[:, None] * inv[None, :]
    return jnp.cos(ang), jnp.sin(ang)


def _mla_tables(pos):
    cos, sin = _rope_cos_sin(pos, MLA_ROPE)
    n = pos.shape[0]
    z = lambda w: jnp.zeros((n, w), F32)
    cq = jnp.concatenate([cos, cos, jnp.ones((n, MLA_NOPE), F32), z(32)], axis=1) * MLA_QSCALE
    s1 = jnp.concatenate([z(16), sin, z(96)], axis=1) * MLA_QSCALE
    s2 = jnp.concatenate([-sin, z(112)], axis=1) * MLA_QSCALE
    return dict(cq=cq, s1=s1, s2=s2, cosT=cos.T, sinT=sin.T)


def _mla_weights(w_in, q_norm, kv_norm, w_q_up, w_uk, w_uv, w_out):
    wq = w_q_up.reshape(MLA_Q_LORA, MLA_HEADS, MLA_NOPE + MLA_ROPE)
    wq = jnp.concatenate([wq[..., MLA_NOPE:], wq[..., :MLA_NOPE],
                          jnp.zeros((MLA_Q_LORA, MLA_HEADS, MLA_QBLK - MLA_NOPE - MLA_ROPE), F32)], axis=-1)
    uk_t = jnp.transpose(w_uk, (1, 2, 0))
    uv_t = jnp.transpose(w_uv, (1, 2, 0))
    return dict(
        wqa=w_in[:, :MLA_Q_LORA].astype(BF16),
        wkvt=w_in[:, MLA_Q_LORA:].T.astype(BF16),
        qn=q_norm.reshape(1, -1),
        kvn=kv_norm.reshape(-1, 1),
        wqup=wq.reshape(MLA_Q_LORA, MLA_HEADS * MLA_QBLK).astype(BF16),
        wukvT=jnp.concatenate([uk_t.reshape(-1, MLA_KV_LORA), uv_t.reshape(-1, MLA_KV_LORA)], axis=0).astype(BF16),
        wukT=uk_t.astype(BF16),
        wuv=jnp.transpose(w_uv, (1, 0, 2)).astype(BF16),
        wout=w_out.astype(BF16),
    )


def mla_layer_prompt(xp, g, w, tabs):
    b, s, _ = xp.shape
    q, kvt, knv = mla_proj(xp, g, w, tabs, tm=1024, absorbed=False)
    o = mla_attn_prompt(q, knv, kvt, tq=1024)
    x = resid_matmul(o.reshape(b * s, -1), w["wout"], xp.reshape(b * s, -1), tm=512)
    return x.reshape(b, s, -1), kvt


def mla_layer_sample(xs, g, w, tabs, cache_t, layer, page_table):
    db, t, _ = xs.shape
    n = db * t
    _, kvt, qc = mla_proj(xs.reshape(1, n, -1), g, w, tabs, tm=n, absorbed=True)
    qc = qc.reshape(MLA_HEADS, db, t, -1).transpose(1, 0, 2, 3).reshape(db, MLA_HEADS * t, -1)
    kv_new = kvt.reshape(MLA_ROW, db, t).transpose(1, 0, 2)
    kv_pad = jnp.pad(kv_new, ((0, 0), (0, 0), (0, PAGE - t)))
    o = mla_attn_sample(page_table, qc, cache_t, layer, kv_pad, w["wuv"], ppc=64)
    x = resid_matmul(o.reshape(n, -1).astype(BF16), w["wout"], xs.reshape(n, -1), tm=n)
    return x.reshape(db, t, -1), kv_new.transpose(0, 2, 1)


def ffn_layer_prompt(xp, g, w_in, cw, cb, w_out, final_g=None):
    b, s, _ = xp.shape
    y, cv = conv_ffn(xp.reshape(b * s, -1), g, w_in, cw, cb, w_out, tm=FFN_TM, tf=FFN_SLAB,
                     rows_per_seq=s, final_g=final_g)
    tiles_per_seq = s // FFN_TM
    return y.reshape(b, s, -1), cv[tiles_per_seq - 1::tiles_per_seq]


def ffn_layer_sample(xs, g, w_in, cw, cb, w_out, state, final_g=None):
    db, t, _ = xs.shape
    z = jnp.zeros((db, t - 2, D_FF), F32)
    p1 = jnp.concatenate([state[:, 1:2], jnp.zeros((db, t - 1, D_FF), F32)], axis=1).reshape(db * t, D_FF)
    p2 = jnp.concatenate([state, z], axis=1).reshape(db * t, D_FF)
    y, gate = conv_ffn(xs.reshape(db * t, -1), g, w_in, cw, cb, w_out, tm=db * t, tf=FFN_SLAB,
                       rows_per_seq=t, prev=(p1, p2), final_g=final_g)
    return y.reshape(db, t, -1), gate.reshape(db, t, D_FF)[:, t - 2:]


def _nsa_tables(pos):
    cos, sin = _rope_cos_sin(pos, NSA_DH)
    z = jnp.zeros_like(sin)
    return dict(cq=jnp.concatenate([cos] * 4, axis=1) * NSA_QSCALE,
                s1=jnp.concatenate([z, sin, z, sin], axis=1) * NSA_QSCALE,
                s2=jnp.concatenate([-sin, z, -sin, z], axis=1) * NSA_QSCALE,
                cosT=cos.T, sinT=sin.T)


def _nsa_weights(w_in, b_gate, pe, w1, w2, w_out):
    hq = NSA_HEADS * NSA_DH
    n_gate = 3 * NSA_HEADS
    half = w1.shape[1] // 2
    w1f = w1.reshape(2, -1, CMP_HID)
    w1p = jnp.concatenate([w1[:, :half], w1[:, half:]], axis=-1).reshape(2, half // 2, 2, NSA_DH, 2 * CMP_HID)
    zero = jnp.zeros_like(w1p)
    w1big = jnp.stack([jnp.concatenate([w1p, zero], axis=-1), jnp.concatenate([zero, w1p], axis=-1)], axis=3)
    return dict(
        wq=w_in[:, :hq].astype(BF16),
        wkvt=w_in[:, hq:hq + 3 * NSA_KV].T.astype(BF16),
        wg=jnp.pad(w_in[:, hq + 3 * NSA_KV:], ((0, 0), (0, 128 - n_gate))).astype(BF16),
        bg=jnp.pad(b_gate, (0, 128 - n_gate)).reshape(1, 128),
        w1big=w1big.reshape(2, half // 2, 4 * NSA_DH, 4 * CMP_HID).astype(BF16),
        cbias=cmp_bias(pe.reshape(2, -1, 1), w1f),
        w2=w2.astype(BF16),
        wout=w_out.astype(BF16),
    )


def _cover_matrix(n_entries, n_blk, n_cols, periodic):
    i = jnp.arange(n_entries)[:, None]
    col = jnp.arange(n_cols)[None, :]
    s = col % n_blk if periodic else col
    j0 = (i - 1) * CMP_STRIDE
    hit = (i >= 1) & (j0 < s * SLC_BLOCK + SLC_BLOCK) & (j0 + 2 * CMP_STRIDE > s * SLC_BLOCK)
    if not periodic:
        hit = hit & (col < n_blk)
    return hit.astype(F32)


def nsa_layer_prompt(xp, g, w, tabs):
    b, s, _ = xp.shape
    n_blk = s // SLC_BLOCK
    q, gates, kct, kst, kwt = nsa_proj(xp, g, w, tabs, tm=512)
    cmp_kv = compress(kct, w)
    gates_g = gates[..., :3 * NSA_HEADS].reshape(b, s, NSA_GROUPS, 3 * NSA_HPG).transpose(0, 2, 1, 3)
    cover = jnp.kron(jnp.eye(128 // n_blk, dtype=F32), _cover_matrix(s // CMP_STRIDE, n_blk, n_blk, False))
    e_mat = (jnp.arange(n_blk)[:, None] == jnp.arange(s)[None, :] // SLC_BLOCK).astype(BF16)
    o = nsa_attn_prompt(q, gates_g, cmp_kv, kst, kwt, cover, e_mat)
    x = resid_matmul(o.reshape(b * s, -1), w["wout"], xp.reshape(b * s, -1), tm=512)
    return x.reshape(b, s, -1), kct, kst, kwt


def nsa_layer_sample(xs, g, w, tabs, cache_cmp_t, cache_slc_t, win_t, layer, page_table):
    db, t, _ = xs.shape
    n = db * t
    past = page_table.shape[1] * PAGE
    q, gates, kct, kst, kwt = nsa_proj(xs.reshape(1, n, -1), g, w, tabs, tm=n)
    per_seq = lambda a: a.reshape(NSA_KV, db, t).transpose(1, 0, 2)
    kc_new, ks_new, kw_new = per_seq(kct), per_seq(kst), per_seq(kwt)
    pad = lambda a: jnp.pad(a, ((0, 0), (0, 0), (0, PAGE - t)))
    cmp_kv = compress(cache_cmp_t, w, npg=32, page_table=page_table, layer=layer)
    rows = lambda a, k: a.reshape(db, t, NSA_GROUPS, NSA_HPG, k).transpose(0, 2, 3, 1, 4).reshape(db, NSA_HEADS * t, k)
    q_r = rows(q.reshape(n, -1), NSA_DH)
    gates_r = rows(gates.reshape(n, -1)[:, :3 * NSA_HEADS], 3)
    n_slc = -(-(past + t) // SLC_BLOCK)
    n_slc_pad = -(-n_slc // 128) * 128
    cover = _cover_matrix(past // CMP_STRIDE, n_slc, n_slc_pad, False)
    own = jnp.arange(NSA_HEADS * t) // (NSA_HPG * t)
    q_bd = q_r[:, :, None, :] * (own[:, None] == jnp.arange(NSA_GROUPS)[None, :]).astype(BF16)[None, :, :, None]
    q_bd = q_bd.reshape(db, NSA_HEADS * t, NSA_GROUPS * NSA_DH)
    o = nsa_attn_sample(page_table, q_r, q_bd, gates_r, cmp_kv, cover, cache_slc_t, layer, pad(ks_new), win_t,
                        pad(kw_new), ppc=32)
    o = o.reshape(db, NSA_GROUPS, NSA_HPG, t, NSA_DH).transpose(0, 3, 1, 2, 4).reshape(n, -1)
    x = resid_matmul(o.astype(BF16), w["wout"], xs.reshape(n, -1), tm=n)
    return x.reshape(db, t, -1), kc_new, ks_new, kw_new


def kernel(x_prompt, x_sample, cache_mla_kv, cache_nsa_cmp_kv, cache_nsa_slc_kv, state_nsa_win_kv, state_ffn_conv,
           page_table, attn_norm, ffn_norm, final_norm, mla_w_in, mla_q_norm, mla_kv_norm, mla_w_q_up, mla_w_uk,
           mla_w_uv, mla_w_out, nsa_w_in, nsa_b_gate, nsa_cmp_pe, nsa_cmp_w1, nsa_cmp_w2, nsa_w_out, ffn_w_in,
           ffn_conv_w, ffn_conv_b, ffn_w_out):
    b, s, _ = x_prompt.shape
    db, t, _ = x_sample.shape
    depth = attn_norm.shape[0]
    past = page_table.shape[1] * PAGE
    pos_p = jnp.arange(s)
    pos_s = past + jnp.arange(db * t) % t
    mla_t = jnp.transpose(cache_mla_kv, (0, 1, 3, 2))
    to_t = lambda c: jnp.transpose(c, (0, 1, 3, 4, 5, 2)).reshape(c.shape[0], c.shape[1], NSA_KV, c.shape[2])
    cmp_t, slc_t, win_t = to_t(cache_nsa_cmp_kv), to_t(cache_nsa_slc_kv), to_t(state_nsa_win_kv)
    tabs_mla_p, tabs_mla_s = _mla_tables(pos_p), _mla_tables(pos_s)
    tabs_nsa_p, tabs_nsa_s = _nsa_tables(pos_p), _nsa_tables(pos_s)

    xp, xs = x_prompt, x_sample
    outs = {k: [] for k in ("mla_p", "mla_s", "cmp_p", "cmp_s", "slc_p", "slc_s", "win_p", "win_s", "conv_p", "conv_s")}
    rows_t = lambda a: a.transpose(0, 2, 1)
    for i in range(depth):
        l = i // 2
        g_attn = attn_norm[i].reshape(1, -1)
        if i % 2 == 0:
            w = _mla_weights(mla_w_in[l], mla_q_norm[l], mla_kv_norm[l], mla_w_q_up[l], mla_w_uk[l], mla_w_uv[l],
                             mla_w_out[l])
            xp, kvt_p = mla_layer_prompt(xp, g_attn, w, tabs_mla_p)
            xs, kv_s = mla_layer_sample(xs, g_attn, w, tabs_mla_s, mla_t, l, page_table)
            outs["mla_p"].append(kvt_p)
            outs["mla_s"].append(kv_s)
        else:
            w = _nsa_weights(nsa_w_in[l], nsa_b_gate[l], nsa_cmp_pe[l], nsa_cmp_w1[l], nsa_cmp_w2[l], nsa_w_out[l])
            xp, kct, kst, kwt = nsa_layer_prompt(xp, g_attn, w, tabs_nsa_p)
            xs, kc_n, ks_n, kw_n = nsa_layer_sample(xs, g_attn, w, tabs_nsa_s, cmp_t, slc_t, win_t, l, page_table)
            outs["cmp_p"].append(kct)
            outs["slc_p"].append(kst)
            outs["win_p"].append(kwt[:, :, s - min(WINDOW, s):])
            outs["cmp_s"].append(rows_t(kc_n))
            outs["slc_s"].append(rows_t(ks_n))
            outs["win_s"].append(jnp.concatenate([win_t[l], kw_n], axis=2)[:, :, -WINDOW:])
        fin = final_norm.reshape(1, -1) if i == depth - 1 else None
        g_ffn = ffn_norm[i].reshape(1, -1)
        f = (ffn_w_in[i].astype(BF16), ffn_conv_w[i], ffn_conv_b[i].reshape(1, -1), ffn_w_out[i].astype(BF16))
        xp, cv_p = ffn_layer_prompt(xp, g_ffn, *f, final_g=fin)
        xs, cv_s = ffn_layer_sample(xs, g_ffn, *f, state_ffn_conv[i], final_g=fin)
        outs["conv_p"].append(cv_p)
        outs["conv_s"].append(cv_s)

    kv6 = lambda a: a.reshape(a.shape[:3] + (2, NSA_GROUPS, NSA_DH))
    feat_major = lambda lst: kv6(jnp.stack(lst).transpose(0, 1, 3, 2))
    return (xp, xs,
            jnp.stack(outs["mla_p"]).transpose(0, 1, 3, 2), jnp.stack(outs["mla_s"]),
            feat_major(outs["cmp_p"]), kv6(jnp.stack(outs["cmp_s"])),
            feat_major(outs["slc_p"]), kv6(jnp.stack(outs["slc_s"])),
            feat_major(outs["win_p"]), feat_major(outs["win_s"]),
            jnp.stack(outs["conv_p"]), jnp.stack(outs["conv_s"]))
```

```python
import functools

import jax
import jax.numpy as jnp
from jax import lax
from jax.experimental import pallas as pl
from jax.experimental.pallas import tpu as pltpu

F32 = jnp.float32
BF16 = jnp.bfloat16

D_MODEL = 1024
EPS = 1e-6
ROPE_THETA = 10000.0
NEG = -1e30
PAGE = 128

MLA_HEADS = 16
MLA_Q_LORA = 384
MLA_KV_LORA = 256
MLA_NOPE = 64
MLA_ROPE = 32
MLA_V = 64
MLA_ROW = MLA_KV_LORA + MLA_ROPE
MLA_SCALE = (MLA_NOPE + MLA_ROPE) ** -0.5
MLA_QBLK = 128
MLA_HPS = 4

NSA_HEADS = 16
NSA_GROUPS = 4
NSA_HPG = 4
NSA_DH = 64
NSA_KV = 2 * NSA_GROUPS * NSA_DH
CMP_STRIDE = 16
CMP_PITCH = 24
CMP_HID = 128
SLC_BLOCK = 64
N_SELECT = 16
WINDOW = 512
FORCE_BONUS = 1e4
NSA_SCALE = NSA_DH ** -0.5

D_FF = 2816
FFN_SLAB = 2816
FFN_TM = 512
VMEM_LIMIT = 56 * 1024 * 1024

LOG2E = 1.4426950408889634
MLA_QSCALE = MLA_SCALE * LOG2E
NSA_QSCALE = NSA_SCALE * LOG2E

_NT = (((1,), (1,)), ((), ()))


def _cparams(sem):
    return pltpu.CompilerParams(dimension_semantics=sem, vmem_limit_bytes=VMEM_LIMIT)


def _rms_rows(x, g):
    ms = jnp.mean(x * x, axis=-1, keepdims=True)
    return x * lax.rsqrt(ms + EPS) * g


def _softmax_update(s, m, l):
    m_new = jnp.maximum(m, jnp.max(s, axis=-1, keepdims=True))
    alpha = jnp.exp2(m - m_new)
    p = jnp.exp2(s - m_new)
    l_new = alpha * l + jnp.sum(p, axis=-1, keepdims=True)
    return p, alpha, m_new, l_new


def _safe_div(acc, l):
    return acc / jnp.where(l > 0, l, 1.0)


def _resid_matmul_kernel(a_ref, w_ref, r_ref, o_ref):
    o_ref[...] = r_ref[...] + jnp.dot(a_ref[...], w_ref[...], preferred_element_type=F32)


def resid_matmul(a, w, res, tm):
    n, k = a.shape
    m = w.shape[1]
    return pl.pallas_call(
        _resid_matmul_kernel,
        grid=(n // tm,),
        in_specs=[pl.BlockSpec((tm, k), lambda i: (i, 0)),
                  pl.BlockSpec((k, m), lambda i: (0, 0)),
                  pl.BlockSpec((tm, m), lambda i: (i, 0))],
        out_specs=pl.BlockSpec((tm, m), lambda i: (i, 0)),
        out_shape=jax.ShapeDtypeStruct((n, m), F32),
        compiler_params=_cparams(("parallel",)),
        name="resid_matmul",
    )(a, w, res)


def _mla_proj_kernel(x_ref, g_ref, wqa_ref, qn_ref, wqup_ref, wkvt_ref, kvn_ref, wx_ref,
                     c_ref, s1_ref, s2_ref, cost_ref, sint_ref, q_out, kvt_out, x_out, *, absorbed):
    xn = _rms_rows(x_ref[0], g_ref[...]).astype(BF16)
    qa = jnp.dot(xn, wqa_ref[...], preferred_element_type=F32)
    qn = _rms_rows(qa, qn_ref[...]).astype(BF16)
    q = jnp.dot(qn, wqup_ref[...], preferred_element_type=F32)
    cq, s1, s2 = c_ref[...], s1_ref[...], s2_ref[...]
    for h in range(MLA_HEADS):
        blk = q[:, h * MLA_QBLK:(h + 1) * MLA_QBLK]
        blk = blk * cq + pltpu.roll(blk, 16, 1) * s1 + pltpu.roll(blk, MLA_QBLK - 16, 1) * s2
        blk = blk.astype(BF16)
        q_out[0, :, h * MLA_QBLK:(h + 1) * MLA_QBLK] = blk
        if absorbed:
            q_lat = jnp.dot(blk[:, MLA_ROPE:MLA_ROPE + MLA_NOPE], wx_ref[h], preferred_element_type=F32)
            x_out[0, h, :, :MLA_KV_LORA] = q_lat.astype(BF16)
            x_out[0, h, :, MLA_KV_LORA:] = blk
    kvt = lax.dot_general(wkvt_ref[...], xn, _NT, preferred_element_type=F32)
    c = kvt[:MLA_KV_LORA]
    ms = jnp.mean(c * c, axis=0, keepdims=True)
    cn = c * lax.rsqrt(ms + EPS) * kvn_ref[...]
    k1 = kvt[MLA_KV_LORA:MLA_KV_LORA + 16]
    k2 = kvt[MLA_KV_LORA + 16:MLA_ROW]
    ct, st = cost_ref[...], sint_ref[...]
    kvt_out[0, :MLA_KV_LORA] = cn
    kvt_out[0, MLA_KV_LORA:MLA_KV_LORA + 16] = k1 * ct - k2 * st
    kvt_out[0, MLA_KV_LORA + 16:] = k1 * st + k2 * ct
    if not absorbed:
        x_out[0] = jnp.dot(wx_ref[...], cn.astype(BF16), preferred_element_type=F32).astype(BF16)


def mla_proj(x3, g, w, tabs, tm, absorbed):
    nb, L, _ = x3.shape
    nl = L // tm
    hq = MLA_HEADS * MLA_QBLK
    if absorbed:
        x_shape = jax.ShapeDtypeStruct((nb, MLA_HEADS, L, MLA_KV_LORA + MLA_QBLK), BF16)
        x_spec = pl.BlockSpec((1, MLA_HEADS, tm, MLA_KV_LORA + MLA_QBLK), lambda b, i: (b, 0, i, 0))
        wx_spec = pl.BlockSpec((MLA_HEADS, MLA_NOPE, MLA_KV_LORA), lambda b, i: (0, 0, 0))
    else:
        x_shape = jax.ShapeDtypeStruct((nb, 2 * MLA_HEADS * MLA_NOPE, L), BF16)
        x_spec = pl.BlockSpec((1, 2 * MLA_HEADS * MLA_NOPE, tm), lambda b, i: (b, 0, i))
        wx_spec = pl.BlockSpec((2 * MLA_HEADS * MLA_NOPE, MLA_KV_LORA), lambda b, i: (0, 0))
    const = lambda b, i: (0, 0)
    return pl.pallas_call(
        functools.partial(_mla_proj_kernel, absorbed=absorbed),
        grid=(nb, nl),
        in_specs=[pl.BlockSpec((1, tm, D_MODEL), lambda b, i: (b, i, 0)),
                  pl.BlockSpec((1, D_MODEL), const),
                  pl.BlockSpec((D_MODEL, MLA_Q_LORA), const),
                  pl.BlockSpec((1, MLA_Q_LORA), const),
                  pl.BlockSpec((MLA_Q_LORA, hq), const),
                  pl.BlockSpec((MLA_ROW, D_MODEL), const),
                  pl.BlockSpec((MLA_KV_LORA, 1), const),
                  wx_spec,
                  pl.BlockSpec((tm, MLA_QBLK), lambda b, i: (i, 0)),
                  pl.BlockSpec((tm, MLA_QBLK), lambda b, i: (i, 0)),
                  pl.BlockSpec((tm, MLA_QBLK), lambda b, i: (i, 0)),
                  pl.BlockSpec((16, tm), lambda b, i: (0, i)),
                  pl.BlockSpec((16, tm), lambda b, i: (0, i))],
        out_specs=[pl.BlockSpec((1, tm, hq), lambda b, i: (b, i, 0)),
                   pl.BlockSpec((1, MLA_ROW, tm), lambda b, i: (b, 0, i)),
                   x_spec],
        out_shape=[jax.ShapeDtypeStruct((nb, L, hq), BF16),
                   jax.ShapeDtypeStruct((nb, MLA_ROW, L), F32),
                   x_shape],
        compiler_params=_cparams(("parallel", "parallel")),
        name="mla_proj_abs" if absorbed else "mla_proj",
    )(x3, g, w["wqa"], w["qn"], w["wqup"], w["wkvt"], w["kvn"], w["wukT"] if absorbed else w["wukvT"],
      tabs["cq"], tabs["s1"], tabs["s2"], tabs["cosT"], tabs["sinT"])


def _mla_attn_prompt_kernel(q_ref, kn_ref, v_ref, kpe_ref, o_ref, kt_scr, vt_scr, *, tq, seq):
    qi = pl.program_id(2)

    @pl.when(qi == 0)
    def _():
        kpe = kpe_ref[0].astype(BF16)
        zeros = jnp.zeros((MLA_QBLK - MLA_ROPE - MLA_NOPE, seq), BF16)
        ones_row = jnp.where(lax.broadcasted_iota(jnp.int32, (MLA_V, seq), 0) == 0, 1.0, 0.0).astype(BF16)
        for hh in range(MLA_HPS):
            kt_scr[hh, 0:MLA_ROPE] = kpe
            kt_scr[hh, MLA_ROPE:MLA_ROPE + MLA_NOPE] = kn_ref[0, hh * MLA_NOPE:(hh + 1) * MLA_NOPE]
            kt_scr[hh, MLA_ROPE + MLA_NOPE:] = zeros
            vt_scr[hh, 0:MLA_V] = v_ref[0, hh * MLA_V:(hh + 1) * MLA_V]
            vt_scr[hh, MLA_V:] = ones_row

    row = lax.broadcasted_iota(jnp.int32, (tq, tq), 0)
    col = lax.broadcasted_iota(jnp.int32, (tq, tq), 1)
    causal = col <= row
    qs = [q_ref[0, :, hh * MLA_QBLK:(hh + 1) * MLA_QBLK] for hh in range(MLA_HPS)]

    def step(j, carry, masked):
        off = pl.multiple_of(j * tq, tq)
        new = []
        for hh in range(MLA_HPS):
            m, acc = carry[hh]
            s = jnp.dot(qs[hh], kt_scr[hh, :, pl.ds(off, tq)], preferred_element_type=F32)
            if masked:
                s = jnp.where(causal, s, NEG)
            m_new = jnp.maximum(m, jnp.max(s, axis=-1, keepdims=True))
            p = jnp.exp2(s - m_new).astype(BF16)
            acc = jnp.exp2(m - m_new) * acc + lax.dot_general(p, vt_scr[hh, :, pl.ds(off, tq)], _NT,
                                                             preferred_element_type=F32)
            new.append((m_new, acc))
        return tuple(new)

    init = (jnp.full((tq, 1), NEG, F32), jnp.zeros((tq, 2 * MLA_V), F32))
    carry = lax.fori_loop(0, qi, functools.partial(step, masked=False), (init,) * MLA_HPS)
    carry = step(qi, carry, True)
    o_ref[0] = jnp.concatenate([acc[:, :MLA_V] / acc[:, MLA_V:MLA_V + 1] for _, acc in carry],
                               axis=-1).astype(BF16)


def mla_attn_prompt(q, knv, kvt, tq):
    b, s, _ = q.shape
    hp = MLA_HEADS // MLA_HPS
    return pl.pallas_call(
        functools.partial(_mla_attn_prompt_kernel, tq=tq, seq=s),
        grid=(b, hp, s // tq),
        in_specs=[pl.BlockSpec((1, tq, MLA_HPS * MLA_QBLK), lambda bi, h, i: (bi, i, h)),
                  pl.BlockSpec((1, MLA_HPS * MLA_NOPE, s), lambda bi, h, i: (bi, h, 0)),
                  pl.BlockSpec((1, MLA_HPS * MLA_V, s), lambda bi, h, i: (bi, hp + h, 0)),
                  pl.BlockSpec((1, MLA_ROPE, s), lambda bi, h, i: (bi, MLA_KV_LORA // MLA_ROPE, 0))],
        out_specs=pl.BlockSpec((1, tq, MLA_HPS * MLA_V), lambda bi, h, i: (bi, i, h)),
        out_shape=jax.ShapeDtypeStruct((b, s, MLA_HEADS * MLA_V), BF16),
        scratch_shapes=[pltpu.VMEM((MLA_HPS, MLA_QBLK, s), BF16), pltpu.VMEM((MLA_HPS, 2 * MLA_V, s), BF16)],
        compiler_params=_cparams(("parallel", "parallel", "arbitrary")),
        name="mla_attn_prompt",
    )(q, knv, knv, kvt)


def _mla_attn_sample_kernel(pt_ref, qc_ref, *refs, ppc, t_new):
    page_refs = refs[:ppc]
    kvn_ref, wuv_ref, o_ref, kt_scr, m_scr, l_scr, acc_scr = refs[ppc:]
    c = pl.program_id(1)
    rows = MLA_HEADS * t_new

    @pl.when(c == 0)
    def _():
        m_scr[...] = jnp.full((rows, 1), NEG, F32)
        l_scr[...] = jnp.zeros((rows, 1), F32)
        acc_scr[...] = jnp.zeros((rows, MLA_KV_LORA), F32)

    for i in range(ppc):
        kt_scr[:, i * PAGE:(i + 1) * PAGE] = page_refs[i][0, 0].astype(BF16)
    q = qc_ref[0][:, :MLA_ROW]
    s = jnp.dot(q, kt_scr[...], preferred_element_type=F32)
    p, alpha, m, l = _softmax_update(s, m_scr[...], l_scr[...])
    acc = alpha * acc_scr[...] + lax.dot_general(p.astype(BF16), kt_scr[:MLA_KV_LORA, :], _NT,
                                                 preferred_element_type=F32)
    m_scr[...] = m
    l_scr[...] = l
    acc_scr[...] = acc

    @pl.when(c == pl.num_programs(1) - 1)
    def _():
        kn = kvn_ref[0].astype(BF16)
        s2 = jnp.dot(q, kn, preferred_element_type=F32)
        tq = lax.broadcasted_iota(jnp.int32, s2.shape, 0) % t_new
        tk = lax.broadcasted_iota(jnp.int32, s2.shape, 1)
        s2 = jnp.where(tk <= tq, s2, NEG)
        p2, alpha2, m2, l2 = _softmax_update(s2, m, l)
        acc2 = alpha2 * acc + lax.dot_general(p2.astype(BF16), kn[:MLA_KV_LORA], _NT, preferred_element_type=F32)
        o_lat = _safe_div(acc2, l2).astype(BF16)
        outs = [jnp.dot(o_lat[h * t_new:(h + 1) * t_new], wuv_ref[h], preferred_element_type=F32)
                for h in range(MLA_HEADS)]
        o_ref[0] = jnp.concatenate(outs, axis=-1)


def mla_attn_sample(page_table, qc, cache_t, layer, kv_new, wuv, ppc):
    db, rows, _ = qc.shape
    t_new = rows // MLA_HEADS
    n_pages = page_table.shape[1]
    pt = page_table.reshape(-1)

    def page_spec(k):
        return pl.BlockSpec((1, 1, MLA_ROW, PAGE),
                            lambda b, c, pt_ref: (layer, pt_ref[b * n_pages + c * ppc + k], 0, 0))

    grid_spec = pltpu.PrefetchScalarGridSpec(
        num_scalar_prefetch=1,
        grid=(db, n_pages // ppc),
        in_specs=[pl.BlockSpec((1, rows, qc.shape[2]), lambda b, c, pt_ref: (b, 0, 0))]
                 + [page_spec(k) for k in range(ppc)]
                 + [pl.BlockSpec((1, MLA_ROW, PAGE), lambda b, c, pt_ref: (b, 0, 0)),
                    pl.BlockSpec((MLA_HEADS, MLA_KV_LORA, MLA_V), lambda b, c, pt_ref: (0, 0, 0))],
        out_specs=pl.BlockSpec((1, t_new, MLA_HEADS * MLA_V), lambda b, c, pt_ref: (b, 0, 0)),
        scratch_shapes=[pltpu.VMEM((MLA_ROW, ppc * PAGE), BF16),
                        pltpu.VMEM((rows, 1), F32), pltpu.VMEM((rows, 1), F32),
                        pltpu.VMEM((rows, MLA_KV_LORA), F32)],
    )
    return pl.pallas_call(
        functools.partial(_mla_attn_sample_kernel, ppc=ppc, t_new=t_new),
        grid_spec=grid_spec,
        out_shape=jax.ShapeDtypeStruct((db, t_new, MLA_HEADS * MLA_V), F32),
        compiler_params=_cparams(("parallel", "arbitrary")),
        name="mla_attn_sample",
    )(pt, qc, *([cache_t] * ppc), kv_new, wuv)


def _ffn_kernel(*refs, tm, tf, tiles_per_seq, sample, final):
    x_ref, g_ref, wg_ref, wu_ref, cw_ref, cb_ref, wo_ref = refs[:7]
    k = 7
    if sample:
        p1_ref, p2_ref = refs[k:k + 2]
        k += 2
    if final:
        gf_ref = refs[k]
        k += 1
    y_ref, cv_ref, gs_scr, carry_scr = refs[k:]
    i = pl.program_id(0)
    ns = D_FF // tf

    if not sample:
        @pl.when((i % tiles_per_seq) == 0)
        def _():
            carry_scr[...] = jnp.zeros((ns, 8, tf), F32)

    x = x_ref[...]
    xn = _rms_rows(x, g_ref[...]).astype(BF16)
    y = x
    for cs in range(ns):
        cols = slice(cs * tf, (cs + 1) * tf)
        g = jnp.dot(xn, wg_ref[:, cols], preferred_element_type=F32)
        u = jnp.dot(xn, wu_ref[:, cols], preferred_element_type=F32)
        gs_scr[cs, 8:8 + tm] = g
        if sample:
            gs_scr[cs, 0:8] = jnp.zeros((8, tf), F32)
            rin = lax.broadcasted_iota(jnp.int32, (tm, 1), 0) % 8
            g1 = jnp.where(rin < 1, p1_ref[:, cols], gs_scr[cs, 7:7 + tm])
            g2 = jnp.where(rin < 2, p2_ref[:, cols], gs_scr[cs, 6:6 + tm])
            cv_ref[:, cols] = g
        else:
            gs_scr[cs, 0:8] = carry_scr[cs]
            g1 = gs_scr[cs, 7:7 + tm]
            g2 = gs_scr[cs, 6:6 + tm]
            carry_scr[cs] = gs_scr[cs, tm:tm + 8]
            cv_ref[0, :, cols] = g[tm - 2:tm]
        gc = cb_ref[:, cols] + cw_ref[0:1, cols] * g2 + cw_ref[1:2, cols] * g1 + cw_ref[2:3, cols] * g
        act = (gc * jax.nn.sigmoid(gc) * u).astype(BF16)
        y = y + jnp.dot(act, wo_ref[cols, :], preferred_element_type=F32)
    if final:
        y = _rms_rows(y, gf_ref[...])
    y_ref[...] = y


def conv_ffn(x, g, w_in, cw, cb, w_out, *, tm, tf, rows_per_seq, prev=None, final_g=None):
    n = x.shape[0]
    ns = D_FF // tf
    sample = prev is not None
    final = final_g is not None
    fixed = lambda shape, idx: pl.BlockSpec(shape, lambda i: idx, pipeline_mode=pl.Buffered(1))
    in_specs = [pl.BlockSpec((tm, D_MODEL), lambda i: (i, 0)),
                fixed((1, D_MODEL), (0, 0)),
                fixed((D_MODEL, D_FF), (0, 0)),
                fixed((D_MODEL, D_FF), (0, 1)),
                fixed((3, D_FF), (0, 0)),
                fixed((1, D_FF), (0, 0)),
                fixed((D_FF, D_MODEL), (0, 0))]
    args = [x, g, w_in, w_in, cw, cb, w_out]
    if sample:
        in_specs += [pl.BlockSpec((tm, D_FF), lambda i: (i, 0))] * 2
        args += list(prev)
        cv_shape = jax.ShapeDtypeStruct((n, D_FF), F32)
        cv_spec = pl.BlockSpec((tm, D_FF), lambda i: (i, 0))
        tiles_per_seq = 1
    else:
        tiles_per_seq = rows_per_seq // tm
        cv_shape = jax.ShapeDtypeStruct((n // tm, 2, D_FF), F32)
        cv_spec = pl.BlockSpec((1, 2, D_FF), lambda i: (i, 0, 0))
    if final:
        in_specs.append(fixed((1, D_MODEL), (0, 0)))
        args.append(final_g)
    return pl.pallas_call(
        functools.partial(_ffn_kernel, tm=tm, tf=tf, tiles_per_seq=tiles_per_seq, sample=sample, final=final),
        grid=(n // tm,),
        in_specs=in_specs,
        out_specs=[pl.BlockSpec((tm, D_MODEL), lambda i: (i, 0)), cv_spec],
        out_shape=[jax.ShapeDtypeStruct((n, D_MODEL), F32), cv_shape],
        scratch_shapes=[pltpu.VMEM((ns, tm + 8, tf), F32),
                        pltpu.VMEM((ns, 8, tf), F32)],
        compiler_params=_cparams(("arbitrary",)),
        name="conv_ffn_sample" if sample else "conv_ffn",
    )(*args)


def _rope_rows(kvt, base, ct, st, out, half):
    for gg in range(NSA_GROUPS):
        r0 = base + gg * NSA_DH
        x1 = kvt[r0:r0 + half]
        x2 = kvt[r0 + half:r0 + NSA_DH]
        out[0, gg * NSA_DH:gg * NSA_DH + half] = x1 * ct - x2 * st
        out[0, gg * NSA_DH + half:(gg + 1) * NSA_DH] = x1 * st + x2 * ct


def _nsa_proj_kernel(x_ref, g_ref, wq_ref, wg_ref, bg_ref, wkvt_ref, cq_ref, s1_ref, s2_ref, cost_ref, sint_ref,
                     q_out, gate_out, kc_out, ks_out, kw_out):
    half = NSA_DH // 2
    xn = _rms_rows(x_ref[0], g_ref[...]).astype(BF16)
    q = jnp.dot(xn, wq_ref[...], preferred_element_type=F32)
    cq, s1, s2 = cq_ref[...], s1_ref[...], s2_ref[...]
    for v in range(NSA_HEADS * NSA_DH // 128):
        blk = q[:, v * 128:(v + 1) * 128]
        blk = blk * cq + pltpu.roll(blk, half, 1) * s1 + pltpu.roll(blk, 128 - half, 1) * s2
        q_out[0, :, v * 128:(v + 1) * 128] = blk.astype(BF16)
    gl = jnp.dot(xn, wg_ref[...], preferred_element_type=F32) + bg_ref[...]
    gate_out[0] = jax.nn.sigmoid(gl)
    kvt = lax.dot_general(wkvt_ref[...], xn, _NT, preferred_element_type=F32)
    ct, st = cost_ref[...], sint_ref[...]
    kw = NSA_GROUPS * NSA_DH
    for br, out in enumerate((kc_out, ks_out, kw_out)):
        _rope_rows(kvt, br * NSA_KV, ct, st, out, half)
        out[0, kw:] = kvt[br * NSA_KV + kw:(br + 1) * NSA_KV]


def nsa_proj(x3, g, w, tabs, tm):
    nb, L, _ = x3.shape
    const = lambda b, i: (0, 0)
    hq = NSA_HEADS * NSA_DH
    kv_spec = pl.BlockSpec((1, NSA_KV, tm), lambda b, i: (b, 0, i))
    kv_shape = jax.ShapeDtypeStruct((nb, NSA_KV, L), F32)
    return pl.pallas_call(
        _nsa_proj_kernel,
        grid=(nb, L // tm),
        in_specs=[pl.BlockSpec((1, tm, D_MODEL), lambda b, i: (b, i, 0)),
                  pl.BlockSpec((1, D_MODEL), const),
                  pl.BlockSpec((D_MODEL, hq), const),
                  pl.BlockSpec((D_MODEL, 128), const),
                  pl.BlockSpec((1, 128), const),
                  pl.BlockSpec((3 * NSA_KV, D_MODEL), const),
                  pl.BlockSpec((tm, 128), lambda b, i: (i, 0)),
                  pl.BlockSpec((tm, 128), lambda b, i: (i, 0)),
                  pl.BlockSpec((tm, 128), lambda b, i: (i, 0)),
                  pl.BlockSpec((NSA_DH // 2, tm), lambda b, i: (0, i)),
                  pl.BlockSpec((NSA_DH // 2, tm), lambda b, i: (0, i))],
        out_specs=[pl.BlockSpec((1, tm, hq), lambda b, i: (b, i, 0)),
                   pl.BlockSpec((1, tm, 128), lambda b, i: (b, i, 0)),
                   kv_spec, kv_spec, kv_spec],
        out_shape=[jax.ShapeDtypeStruct((nb, L, hq), BF16),
                   jax.ShapeDtypeStruct((nb, L, 128), F32),
                   kv_shape, kv_shape, kv_shape],
        compiler_params=_cparams(("parallel", "parallel")),
        name="nsa_proj",
    )(x3, g, w["wq"], w["wg"], w["bg"], w["wkvt"], tabs["cq"], tabs["s1"], tabs["s2"], tabs["cosT"], tabs["sinT"])


def _cmp_bias_kernel(pe_ref, w1_ref, o_ref):
    o_ref[0] = jnp.sum(w1_ref[0] * pe_ref[0], axis=0, keepdims=True)


def cmp_bias(pe_col, w1_flat):
    n = pe_col.shape[1]
    return pl.pallas_call(
        _cmp_bias_kernel,
        grid=(2,),
        in_specs=[pl.BlockSpec((1, n, 1), lambda c: (c, 0, 0)),
                  pl.BlockSpec((1, n, CMP_HID), lambda c: (c, 0, 0))],
        out_specs=pl.BlockSpec((1, 1, CMP_HID), lambda c: (c, 0, 0)),
        out_shape=jax.ShapeDtypeStruct((2, 1, CMP_HID), F32),
        name="cmp_bias",
    )(pe_col, w1_flat)


def _compress_kernel(*refs, npg, paged):
    if paged:
        refs = refs[1:]
    page_refs = refs[:npg]
    nv = NSA_KV // 128
    w1_ref, bias_ref, w2_ref, out_ref = refs[npg:npg + 4]
    xt_scrs = refs[npg + 4:npg + 4 + nv]
    fs_scr = refs[npg + 4 + nv]
    c = pl.program_id(1)
    nchunk = npg * PAGE // CMP_STRIDE
    cpp = PAGE // CMP_STRIDE

    @pl.when(c == 0)
    def _():
        fs_scr[:, 0:8] = jnp.zeros((2 * NSA_GROUPS, 8, CMP_HID), F32)

    for v in range(nv):
        cc = v // (nv // 2)
        xt_scr = xt_scrs[v]
        for p in range(npg):
            page = page_refs[p][0, 0, v * 128:(v + 1) * 128] if paged else page_refs[p][0, v * 128:(v + 1) * 128]
            page_t = page.T
            for j in range(cpp):
                r0 = (p * cpp + j) * CMP_PITCH
                xt_scr[r0:r0 + CMP_STRIDE, :] = page_t[j * CMP_STRIDE:(j + 1) * CMP_STRIDE]
        both = jnp.zeros((nchunk, 4 * CMP_HID), F32)
        for lp in range(CMP_STRIDE // 2):
            x2 = jnp.concatenate([xt_scr[pl.ds(2 * lp + i, nchunk, stride=CMP_PITCH), :] for i in range(2)],
                                 axis=1).astype(BF16)
            both = both + jnp.dot(x2, w1_ref[cc, lp], preferred_element_type=F32)
        for u in range(2):
            cg = 2 * v + u
            acc = both[:, u * 2 * CMP_HID:(u + 1) * 2 * CMP_HID]
            fs_scr[cg, 8:8 + nchunk] = acc[:, :CMP_HID]
            h = jax.nn.gelu(fs_scr[cg, 7:7 + nchunk] + acc[:, CMP_HID:] + bias_ref[cc])
            fs_scr[cg, 0:8] = fs_scr[cg, nchunk:nchunk + 8]
            out_ref[0, cg] = jnp.dot(h.astype(BF16), w2_ref[cc], preferred_element_type=F32).astype(BF16)


def compress(kv_src, w, *, npg=16, page_table=None, layer=None):
    paged = page_table is not None
    if paged:
        nb, n_pages = page_table.shape
        pt = page_table.reshape(-1)
        specs = [pl.BlockSpec((1, 1, NSA_KV, PAGE),
                              (lambda b, c, pt_ref, k=k: (layer, pt_ref[b * n_pages + c * npg + k], 0, 0)))
                 for k in range(npg)]
        imap = lambda f: (lambda b, c, pt_ref: f(b, c))
    else:
        nb = kv_src.shape[0]
        n_pages = kv_src.shape[2] // PAGE
        specs = [pl.BlockSpec((1, NSA_KV, PAGE), (lambda b, c, k=k: (b, 0, c * npg + k))) for k in range(npg)]
        imap = lambda f: f
    nchunk = npg * PAGE // CMP_STRIDE
    in_specs = specs + [pl.BlockSpec((2, CMP_STRIDE // 2, 4 * NSA_DH, 4 * CMP_HID), imap(lambda b, c: (0, 0, 0, 0))),
                        pl.BlockSpec((2, 1, CMP_HID), imap(lambda b, c: (0, 0, 0))),
                        pl.BlockSpec((2, CMP_HID, NSA_DH), imap(lambda b, c: (0, 0, 0)))]
    out_spec = pl.BlockSpec((1, 2 * NSA_GROUPS, nchunk, NSA_DH), imap(lambda b, c: (b, 0, c, 0)))
    scratch = [pltpu.VMEM((nchunk * CMP_PITCH, 128), F32)] * (NSA_KV // 128) \
        + [pltpu.VMEM((2 * NSA_GROUPS, nchunk + 8, CMP_HID), F32)]
    grid = (nb, n_pages // npg)
    out_shape = jax.ShapeDtypeStruct((nb, 2 * NSA_GROUPS, n_pages * PAGE // CMP_STRIDE, NSA_DH), BF16)
    kern = functools.partial(_compress_kernel, npg=npg, paged=paged)
    args = [kv_src] * npg + [w["w1big"], w["cbias"], w["w2"]]
    if paged:
        gs = pltpu.PrefetchScalarGridSpec(num_scalar_prefetch=1, grid=grid, in_specs=in_specs,
                                          out_specs=out_spec, scratch_shapes=scratch)
        return pl.pallas_call(kern, grid_spec=gs, out_shape=out_shape,
                              compiler_params=_cparams(("parallel", "arbitrary")), name="compress_paged")(pt, *args)
    return pl.pallas_call(kern, grid=grid, in_specs=in_specs, out_specs=out_spec, out_shape=out_shape,
                          scratch_shapes=scratch, compiler_params=_cparams(("parallel", "arbitrary")),
                          name="compress")(*args)


def _masked_softmax(s, valid):
    s = jnp.where(valid, s, NEG)
    m = jnp.max(s, axis=-1, keepdims=True)
    p = jnp.where(valid, jnp.exp2(s - m), 0.0)
    den = jnp.sum(p, axis=-1, keepdims=True)
    return p / jnp.where(den > 0, den, 1.0)


def _nsa_attn_prompt_kernel(q_ref, gate_ref, kc_ref, vc_ref, kst_ref, vst_ref, kwt_ref, vwt_ref, cover_ref, e_ref,
                            o_ref, ks_scr, vs_scr, kw_scr, vw_scr, *, tq, seq):
    i = pl.program_id(2)
    t0 = i * tq
    n_blk = seq // SLC_BLOCK

    @pl.when(i == 0)
    def _():
        ones_row = jnp.where(lax.broadcasted_iota(jnp.int32, (NSA_DH, seq), 0) == 0, 1.0, 0.0).astype(BF16)
        ks_scr[0:NSA_DH] = kst_ref[0].astype(BF16)
        ks_scr[NSA_DH:] = e_ref[...]
        vs_scr[0:NSA_DH] = vst_ref[0].astype(BF16)
        vs_scr[NSA_DH:] = ones_row
        kw_scr[...] = kwt_ref[0].astype(BF16)
        vw_scr[0:NSA_DH] = vwt_ref[0].astype(BF16)
        vw_scr[NSA_DH:] = ones_row

    q4 = jnp.concatenate([q_ref[0, :, h * NSA_DH:(h + 1) * NSA_DH] for h in range(NSA_HPG)], axis=0)
    rows = NSA_HPG * tq
    t_row = t0 + lax.broadcasted_iota(jnp.int32, (tq, 1), 0)
    t_row4 = t0 + lax.broadcasted_iota(jnp.int32, (rows, 1), 0) % tq

    def weighted(p, vt):
        return lax.dot_general(p.astype(BF16), vt, _NT, preferred_element_type=F32)

    ci = lax.broadcasted_iota(jnp.int32, (rows, kc_ref.shape[2]), 1)
    valid_c = (ci >= 1) & (CMP_STRIDE * ci + CMP_STRIDE - 1 <= t_row4)
    s_c = lax.dot_general(q4, kc_ref[0, 0], _NT, preferred_element_type=F32)
    p_c = _masked_softmax(s_c, valid_c)
    o_cmp = jnp.dot(p_c.astype(BF16), vc_ref[0, 0], preferred_element_type=F32)

    p_sum = p_c[0:tq]
    for h in range(1, NSA_HPG):
        p_sum = p_sum + p_c[h * tq:(h + 1) * tq]
    nq = 128 // n_blk
    rq = tq // nq
    p_wide = jnp.concatenate([p_sum[u * rq:(u + 1) * rq] for u in range(nq)], axis=1)
    imp = jnp.dot(p_wide, cover_ref[...], preferred_element_type=F32, precision=lax.Precision.HIGHEST)
    lane = lax.broadcasted_iota(jnp.int32, (rq, 128), 1)
    blk = lane % n_blk
    t_pk = t0 + (lane // n_blk) * rq + lax.broadcasted_iota(jnp.int32, (rq, 128), 0)
    cur = t_pk // SLC_BLOCK
    forced = (blk == 0) | ((blk <= cur) & (blk > cur - 2))
    score = jnp.where(blk * SLC_BLOCK <= t_pk, imp + jnp.where(forced, FORCE_BONUS, 0.0), NEG)
    cnt = jnp.zeros((rq, 128), F32)
    for k in range(1, n_blk):
        other = jnp.where(blk >= k, pltpu.roll(score, k, 1), pltpu.roll(score, 128 - n_blk + k, 1))
        beats = (other > score) | ((other == score) & (blk >= k))
        cnt = cnt + jnp.where(beats, 1.0, 0.0)
    sel_pk = jnp.where(cnt < N_SELECT, 0.0, NEG)
    sel_bias = jnp.concatenate([sel_pk[:, u * n_blk:(u + 1) * n_blk] for u in range(nq)], axis=0).astype(BF16)
    q_aug = jnp.concatenate([q4, jnp.concatenate([sel_bias] * NSA_HPG, axis=0)], axis=1)

    d_iota = lax.broadcasted_iota(jnp.int32, (tq, tq), 0) - lax.broadcasted_iota(jnp.int32, (tq, tq), 1)
    causal_bias = jnp.where(d_iota >= 0, 0.0, NEG)

    def slc_step(j, carry, diagonal):
        m, acc = carry
        off = pl.multiple_of(j * tq, tq)
        s = jnp.dot(q_aug, ks_scr[:, pl.ds(off, tq)], preferred_element_type=F32)
        if diagonal:
            s = (s.reshape(NSA_HPG, tq, tq) + causal_bias[None]).reshape(rows, tq)
        m_new = jnp.maximum(m, jnp.max(s, axis=-1, keepdims=True))
        alpha = jnp.exp2(m - m_new)
        acc = alpha * acc + weighted(jnp.exp2(s - m_new), vs_scr[:, pl.ds(off, tq)])
        return m_new, acc

    init = (jnp.full((rows, 1), NEG, F32), jnp.zeros((rows, 2 * NSA_DH), F32))
    carry = lax.fori_loop(0, i, functools.partial(slc_step, diagonal=False), init)
    _, acc_s = slc_step(i, carry, True)
    o_slc = acc_s[:, :NSA_DH] / acc_s[:, NSA_DH:NSA_DH + 1]

    span = WINDOW + tq
    w_off = pl.multiple_of(jnp.maximum(t0 - WINDOW, 0), tq)
    s_w = jnp.dot(q4, kw_scr[:, pl.ds(w_off, span)], preferred_element_type=F32)
    d = t_row - (w_off + lax.broadcasted_iota(jnp.int32, (tq, span), 1))
    bias_w = jnp.where((d >= 0) & (d < WINDOW), 0.0, NEG)
    s_w = (s_w.reshape(NSA_HPG, tq, span) + bias_w[None]).reshape(rows, span)
    acc_w = weighted(jnp.exp2(s_w - jnp.max(s_w, axis=-1, keepdims=True)), vw_scr[:, pl.ds(w_off, span)])
    o_win = acc_w[:, :NSA_DH] / acc_w[:, NSA_DH:NSA_DH + 1]

    outs = []
    gate = gate_ref[0, 0]
    for h in range(NSA_HPG):
        r = slice(h * tq, (h + 1) * tq)
        outs.append(gate[:, 3 * h:3 * h + 1] * o_cmp[r] + gate[:, 3 * h + 1:3 * h + 2] * o_slc[r]
                    + gate[:, 3 * h + 2:3 * h + 3] * o_win[r])
    o_ref[0] = jnp.concatenate(outs, axis=-1).astype(BF16)


def nsa_attn_prompt(q, gates_g, cmp_kv, kst, kwt, cover, e_mat, tq=512):
    b, s, _ = q.shape
    g = NSA_GROUPS
    gw = NSA_HPG * NSA_DH
    ncmp = cmp_kv.shape[2]
    n_blk = s // SLC_BLOCK
    kt_spec = lambda off: pl.BlockSpec((1, NSA_DH, s), lambda bi, gi, i: (bi, off + gi, 0))
    return pl.pallas_call(
        functools.partial(_nsa_attn_prompt_kernel, tq=tq, seq=s),
        grid=(b, g, s // tq),
        in_specs=[pl.BlockSpec((1, tq, gw), lambda bi, gi, i: (bi, i, gi)),
                  pl.BlockSpec((1, 1, tq, 3 * NSA_HPG), lambda bi, gi, i: (bi, gi, i, 0)),
                  pl.BlockSpec((1, 1, ncmp, NSA_DH), lambda bi, gi, i: (bi, gi, 0, 0)),
                  pl.BlockSpec((1, 1, ncmp, NSA_DH), lambda bi, gi, i: (bi, g + gi, 0, 0)),
                  kt_spec(0), kt_spec(g), kt_spec(0), kt_spec(g),
                  pl.BlockSpec(cover.shape, lambda bi, gi, i: (0, 0)),
                  pl.BlockSpec((n_blk, s), lambda bi, gi, i: (0, 0))],
        out_specs=pl.BlockSpec((1, tq, gw), lambda bi, gi, i: (bi, i, gi)),
        out_shape=jax.ShapeDtypeStruct((b, s, NSA_HEADS * NSA_DH), BF16),
        scratch_shapes=[pltpu.VMEM((NSA_DH + n_blk, s), BF16), pltpu.VMEM((2 * NSA_DH, s), BF16),
                        pltpu.VMEM((NSA_DH, s), BF16), pltpu.VMEM((2 * NSA_DH, s), BF16)],
        compiler_params=_cparams(("parallel", "parallel", "arbitrary")),
        name="nsa_attn_prompt",
    )(q, gates_g, cmp_kv, cmp_kv, kst, kst, kwt, kwt, cover, e_mat)


def _nsa_cmp_sample_kernel(q_ref, ckv_ref, cover_ref, ocmp_ref, imp_ref, *, t_new, past):
    g = NSA_GROUPS
    rg = NSA_HPG * t_new
    tt_g = lax.broadcasted_iota(jnp.int32, (rg, 1), 0) % t_new
    p_sums = []
    for gg in range(g):
        qg = q_ref[0, gg * rg:(gg + 1) * rg]
        s = lax.dot_general(qg, ckv_ref[0, gg], _NT, preferred_element_type=F32)
        ci = lax.broadcasted_iota(jnp.int32, s.shape, 1)
        valid = (ci >= 1) & (CMP_STRIDE * ci + CMP_STRIDE - 1 <= past + tt_g)
        p = _masked_softmax(s, valid)
        ocmp_ref[0, gg * rg:(gg + 1) * rg] = jnp.dot(p.astype(BF16), ckv_ref[0, g + gg], preferred_element_type=F32)
        p_sum = p[0:t_new]
        for h in range(1, NSA_HPG):
            p_sum = p_sum + p[h * t_new:(h + 1) * t_new]
        p_sums.append(p_sum)
    imp_ref[0] = jnp.dot(jnp.concatenate(p_sums, axis=0), cover_ref[...], preferred_element_type=F32,
                         precision=lax.Precision.HIGHEST)


def nsa_cmp_sample(q, cmp_kv, cover, past):
    db, rows, _ = q.shape
    t_new = rows // NSA_HEADS
    ncmp = cmp_kv.shape[2]
    n_slc_pad = cover.shape[1]
    return pl.pallas_call(
        functools.partial(_nsa_cmp_sample_kernel, t_new=t_new, past=past),
        grid=(db,),
        in_specs=[pl.BlockSpec((1, rows, NSA_DH), lambda b: (b, 0, 0)),
                  pl.BlockSpec((1, 2 * NSA_GROUPS, ncmp, NSA_DH), lambda b: (b, 0, 0, 0)),
                  pl.BlockSpec((ncmp, n_slc_pad), lambda b: (0, 0))],
        out_specs=[pl.BlockSpec((1, rows, NSA_DH), lambda b: (b, 0, 0)),
                   pl.BlockSpec((1, NSA_GROUPS * t_new, n_slc_pad), lambda b: (b, 0, 0))],
        out_shape=[jax.ShapeDtypeStruct((db, rows, NSA_DH), F32),
                   jax.ShapeDtypeStruct((db, NSA_GROUPS * t_new, n_slc_pad), F32)],
        compiler_params=_cparams(("parallel",)),
        name="nsa_cmp_sample",
    )(q, cmp_kv, cover)


def _nsa_select_sample_kernel(imp_ref, sel_ref, *, t_new, past, bpc):
    imp = imp_ref[...]
    n_rows, n_slc_pad = imp.shape
    blk = lax.broadcasted_iota(jnp.int32, imp.shape, 1)
    tpos = past + lax.broadcasted_iota(jnp.int32, (n_rows, 1), 0) % t_new
    cur = tpos // SLC_BLOCK
    forced = (blk == 0) | ((blk <= cur) & (blk > cur - 2))
    score = jnp.where(blk * SLC_BLOCK <= tpos, imp + jnp.where(forced, FORCE_BONUS, 0.0), NEG)
    n_slc = (past + t_new + SLC_BLOCK - 1) // SLC_BLOCK
    gone = jnp.float32(-3e38)
    score = jnp.where(blk < n_slc, score, gone)
    chosen = blk < 0
    for k in range(N_SELECT):
        mx = jnp.max(score, axis=-1, keepdims=True)
        pick = jnp.min(jnp.where(score == mx, blk, n_slc_pad), axis=-1, keepdims=True)
        chosen = chosen | (blk == pick)
        score = jnp.where(blk == pick, gone, score)
    sel = jnp.where(chosen, 0.0, NEG)
    for cb in range(n_slc_pad // bpc):
        sel_ref[cb] = sel[:, cb * bpc:(cb + 1) * bpc]


def nsa_select_sample(imp, t_new, past, bpc):
    n_rows, n_slc_pad = imp.shape
    slots = n_slc_pad // bpc
    return pl.pallas_call(
        functools.partial(_nsa_select_sample_kernel, t_new=t_new, past=past, bpc=bpc),
        grid=(1,),
        in_specs=[pl.BlockSpec((n_rows, n_slc_pad), lambda i: (0, 0))],
        out_specs=pl.BlockSpec((slots, n_rows, bpc), lambda i: (0, 0, 0)),
        out_shape=jax.ShapeDtypeStruct((slots, n_rows, bpc), F32),
        compiler_params=_cparams(("arbitrary",)),
        name="nsa_select_sample",
    )(imp)


def _nsa_attn_sample_kernel(pt_ref, qbd_ref, gate_ref, ocmp_ref, sel_ref, *refs, ppc, t_new, past):
    page_refs = refs[:ppc]
    ksn_ref, win_ref, kwn_ref, e_ref, o_ref, kv_scr, m_scr, l_scr, acc_scr = refs[ppc:]
    c = pl.program_id(1)
    g = NSA_GROUPS
    rg = NSA_HPG * t_new
    rows = g * rg
    kw = g * NSA_DH
    nk = ppc * PAGE
    tt8 = lax.broadcasted_iota(jnp.int32, (t_new, 1), 0)

    @pl.when(c == 0)
    def _():
        m_scr[...] = jnp.full((rows, 1), NEG, F32)
        l_scr[...] = jnp.zeros((rows, 1), F32)
        acc_scr[...] = jnp.zeros((rows, kw), F32)

    def chosen_bias(cb, lo, hi):
        return jnp.dot(sel_ref[cb, 0].astype(BF16), e_ref[:, lo:hi], preferred_element_type=F32)

    def update(s, bias4, vt):
        nk = s.shape[-1]
        s = (s.reshape(g, NSA_HPG, t_new, nk) + bias4).reshape(rows, nk)
        p, alpha, m, l = _softmax_update(s, m_scr[...], l_scr[...])
        acc_scr[...] = alpha * acc_scr[...] + lax.dot_general(p.astype(BF16), vt, _NT, preferred_element_type=F32)
        m_scr[...] = m
        l_scr[...] = l

    def own_group(acc):
        rgrp = lax.broadcasted_iota(jnp.int32, (rows, 1), 0) // rg
        out = jnp.zeros((rows, NSA_DH), F32)
        for gg in range(g):
            out = out + jnp.where(rgrp == gg, acc[:, gg * NSA_DH:(gg + 1) * NSA_DH], 0.0)
        return out

    for i in range(ppc):
        kv_scr[:, i * PAGE:(i + 1) * PAGE] = page_refs[i][0, 0].astype(BF16)
    qbd = qbd_ref[0]
    update(jnp.dot(qbd, kv_scr[0:kw, :], preferred_element_type=F32),
           chosen_bias(c, 0, nk).reshape(g, 1, t_new, nk), kv_scr[kw:, :])

    @pl.when(c == pl.num_programs(1) - 1)
    def _():
        ksn = ksn_ref[0].astype(BF16)
        kn = lax.broadcasted_iota(jnp.int32, (t_new, PAGE), 1)
        kn_g = lax.broadcasted_iota(jnp.int32, (g * t_new, PAGE), 1)
        tt_gt = lax.broadcasted_iota(jnp.int32, (g * t_new, 1), 0) % t_new
        tail_ok = (kn_g <= tt_gt) & (kn_g < t_new)
        tail_bias = jnp.where(tail_ok, chosen_bias(past // nk, 0, PAGE), NEG)
        update(jnp.dot(qbd, ksn[0:kw], preferred_element_type=F32), tail_bias.reshape(g, 1, t_new, PAGE), ksn[kw:])
        o_slc = _safe_div(own_group(acc_scr[...]), l_scr[...])

        m_scr[...] = jnp.full((rows, 1), NEG, F32)
        l_scr[...] = jnp.zeros((rows, 1), F32)
        acc_scr[...] = jnp.zeros((rows, kw), F32)
        lw = win_ref.shape[3]
        wk = win_ref[0, 0].astype(BF16)
        kwn = kwn_ref[0].astype(BF16)
        wi = lax.broadcasted_iota(jnp.int32, (t_new, lw), 1)
        d_old = tt8 + lw - wi
        bias_old = jnp.where((d_old >= 0) & (d_old < WINDOW), 0.0, NEG)
        d_new = tt8 - kn
        bias_new = jnp.where((d_new >= 0) & (d_new < WINDOW) & (kn < t_new), 0.0, NEG)
        update(jnp.dot(qbd, wk[0:kw], preferred_element_type=F32), bias_old.reshape(1, 1, t_new, lw), wk[kw:])
        update(jnp.dot(qbd, kwn[0:kw], preferred_element_type=F32), bias_new.reshape(1, 1, t_new, PAGE), kwn[kw:])
        o_win = _safe_div(own_group(acc_scr[...]), l_scr[...])
        gate = gate_ref[0]
        o_ref[0] = gate[:, 0:1] * ocmp_ref[0] + gate[:, 1:2] * o_slc + gate[:, 2:3] * o_win


def nsa_attn_sample(page_table, q, qbd, gates, cmp_kv, cover, cache_t, layer, ks_new, win_t, kw_new, ppc):
    db, rows, _ = q.shape
    t_new = rows // NSA_HEADS
    n_pages = page_table.shape[1]
    past = n_pages * PAGE
    n_slc_pad = cover.shape[1]
    lw = win_t.shape[3]
    pt = page_table.reshape(-1)
    nk = ppc * PAGE
    bpc = nk // SLC_BLOCK
    slots = n_slc_pad // bpc
    gt = NSA_GROUPS * t_new
    e_mat = (jnp.arange(bpc)[:, None] == jnp.arange(nk)[None, :] // SLC_BLOCK).astype(BF16)

    o_cmp, imp = nsa_cmp_sample(q, cmp_kv, cover, past)
    sel = nsa_select_sample(imp.reshape(db * gt, n_slc_pad), t_new, past, bpc).reshape(slots, db, gt, bpc)

    def page_spec(k):
        return pl.BlockSpec((1, 1, NSA_KV, PAGE),
                            lambda b, c, pt_ref: (layer, pt_ref[b * n_pages + c * ppc + k], 0, 0))

    per_b = lambda *blk: pl.BlockSpec((1,) + blk, lambda b, c, pt_ref: (b,) + (0,) * len(blk))
    grid_spec = pltpu.PrefetchScalarGridSpec(
        num_scalar_prefetch=1,
        grid=(db, n_pages // ppc),
        in_specs=[per_b(rows, NSA_GROUPS * NSA_DH), per_b(rows, 3), per_b(rows, NSA_DH),
                  pl.BlockSpec((slots, 1, gt, bpc), lambda b, c, pt_ref: (0, b, 0, 0))]
                 + [page_spec(k) for k in range(ppc)]
                 + [per_b(NSA_KV, PAGE),
                    pl.BlockSpec((1, 1, NSA_KV, lw), lambda b, c, pt_ref: (layer, b, 0, 0)),
                    per_b(NSA_KV, PAGE),
                    pl.BlockSpec((bpc, nk), lambda b, c, pt_ref: (0, 0))],
        out_specs=per_b(rows, NSA_DH),
        scratch_shapes=[pltpu.VMEM((NSA_KV, ppc * PAGE), BF16),
                        pltpu.VMEM((rows, 1), F32), pltpu.VMEM((rows, 1), F32),
                        pltpu.VMEM((rows, NSA_GROUPS * NSA_DH), F32)],
    )
    return pl.pallas_call(
        functools.partial(_nsa_attn_sample_kernel, ppc=ppc, t_new=t_new, past=past),
        grid_spec=grid_spec,
        out_shape=jax.ShapeDtypeStruct((db, rows, NSA_DH), F32),
        compiler_params=_cparams(("parallel", "arbitrary")),
        name="nsa_attn_sample",
    )(pt, qbd, gates, o_cmp, sel, *([cache_t] * ppc), ks_new, win_t, kw_new, e_mat)


def _rope_cos_sin(pos, d):
    inv = ROPE_THETA ** (-jnp.arange(0, d, 2, dtype=F32) / d)
    ang = pos.astype(F32)[:, None] * inv[None, :]
    return jnp.cos(ang), jnp.sin(ang)


def _mla_tables(pos):
    cos, sin = _rope_cos_sin(pos, MLA_ROPE)
    n = pos.shape[0]
    z = lambda w: jnp.zeros((n, w), F32)
    cq = jnp.concatenate([cos, cos, jnp.ones((n, MLA_NOPE), F32), z(32)], axis=1) * MLA_QSCALE
    s1 = jnp.concatenate([z(16), sin, z(96)], axis=1) * MLA_QSCALE
    s2 = jnp.concatenate([-sin, z(112)], axis=1) * MLA_QSCALE
    return dict(cq=cq, s1=s1, s2=s2, cosT=cos.T, sinT=sin.T)


def _mla_weights(w_in, q_norm, kv_norm, w_q_up, w_uk, w_uv, w_out):
    wq = w_q_up.reshape(MLA_Q_LORA, MLA_HEADS, MLA_NOPE + MLA_ROPE)
    wq = jnp.concatenate([wq[..., MLA_NOPE:], wq[..., :MLA_NOPE],
                          jnp.zeros((MLA_Q_LORA, MLA_HEADS, MLA_QBLK - MLA_NOPE - MLA_ROPE), F32)], axis=-1)
    uk_t = jnp.transpose(w_uk, (1, 2, 0))
    uv_t = jnp.transpose(w_uv, (1, 2, 0))
    return dict(
        wqa=w_in[:, :MLA_Q_LORA].astype(BF16),
        wkvt=w_in[:, MLA_Q_LORA:].T.astype(BF16),
        qn=q_norm.reshape(1, -1),
        kvn=kv_norm.reshape(-1, 1),
        wqup=wq.reshape(MLA_Q_LORA, MLA_HEADS * MLA_QBLK).astype(BF16),
        wukvT=jnp.concatenate([uk_t.reshape(-1, MLA_KV_LORA), uv_t.reshape(-1, MLA_KV_LORA)], axis=0).astype(BF16),
        wukT=uk_t.astype(BF16),
        wuv=jnp.transpose(w_uv, (1, 0, 2)).astype(BF16),
        wout=w_out.astype(BF16),
    )


def mla_layer_prompt(xp, g, w, tabs):
    b, s, _ = xp.shape
    q, kvt, knv = mla_proj(xp, g, w, tabs, tm=1024, absorbed=False)
    o = mla_attn_prompt(q, knv, kvt, tq=1024)
    x = resid_matmul(o.reshape(b * s, -1), w["wout"], xp.reshape(b * s, -1), tm=512)
    return x.reshape(b, s, -1), kvt


def mla_layer_sample(xs, g, w, tabs, cache_t, layer, page_table):
    db, t, _ = xs.shape
    n = db * t
    _, kvt, qc = mla_proj(xs.reshape(1, n, -1), g, w, tabs, tm=n, absorbed=True)
    qc = qc.reshape(MLA_HEADS, db, t, -1).transpose(1, 0, 2, 3).reshape(db, MLA_HEADS * t, -1)
    kv_new = kvt.reshape(MLA_ROW, db, t).transpose(1, 0, 2)
    kv_pad = jnp.pad(kv_new, ((0, 0), (0, 0), (0, PAGE - t)))
    o = mla_attn_sample(page_table, qc, cache_t, layer, kv_pad, w["wuv"], ppc=64)
    x = resid_matmul(o.reshape(n, -1).astype(BF16), w["wout"], xs.reshape(n, -1), tm=n)
    return x.reshape(db, t, -1), kv_new.transpose(0, 2, 1)


def ffn_layer_prompt(xp, g, w_in, cw, cb, w_out, final_g=None):
    b, s, _ = xp.shape
    y, cv = conv_ffn(xp.reshape(b * s, -1), g, w_in, cw, cb, w_out, tm=FFN_TM, tf=FFN_SLAB,
                     rows_per_seq=s, final_g=final_g)
    tiles_per_seq = s // FFN_TM
    return y.reshape(b, s, -1), cv[tiles_per_seq - 1::tiles_per_seq]


def ffn_layer_sample(xs, g, w_in, cw, cb, w_out, state, final_g=None):
    db, t, _ = xs.shape
    z = jnp.zeros((db, t - 2, D_FF), F32)
    p1 = jnp.concatenate([state[:, 1:2], jnp.zeros((db, t - 1, D_FF), F32)], axis=1).reshape(db * t, D_FF)
    p2 = jnp.concatenate([state, z], axis=1).reshape(db * t, D_FF)
    y, gate = conv_ffn(xs.reshape(db * t, -1), g, w_in, cw, cb, w_out, tm=db * t, tf=FFN_SLAB,
                       rows_per_seq=t, prev=(p1, p2), final_g=final_g)
    return y.reshape(db, t, -1), gate.reshape(db, t, D_FF)[:, t - 2:]


def _nsa_tables(pos):
    cos, sin = _rope_cos_sin(pos, NSA_DH)
    z = jnp.zeros_like(sin)
    return dict(cq=jnp.concatenate([cos] * 4, axis=1) * NSA_QSCALE,
                s1=jnp.concatenate([z, sin, z, sin], axis=1) * NSA_QSCALE,
                s2=jnp.concatenate([-sin, z, -sin, z], axis=1) * NSA_QSCALE,
                cosT=cos.T, sinT=sin.T)


def _nsa_weights(w_in, b_gate, pe, w1, w2, w_out):
    hq = NSA_HEADS * NSA_DH
    n_gate = 3 * NSA_HEADS
    half = w1.shape[1] // 2
    w1f = w1.reshape(2, -1, CMP_HID)
    w1p = jnp.concatenate([w1[:, :half], w1[:, half:]], axis=-1).reshape(2, half // 2, 2, NSA_DH, 2 * CMP_HID)
    zero = jnp.zeros_like(w1p)
    w1big = jnp.stack([jnp.concatenate([w1p, zero], axis=-1), jnp.concatenate([zero, w1p], axis=-1)], axis=3)
    return dict(
        wq=w_in[:, :hq].astype(BF16),
        wkvt=w_in[:, hq:hq + 3 * NSA_KV].T.astype(BF16),
        wg=jnp.pad(w_in[:, hq + 3 * NSA_KV:], ((0, 0), (0, 128 - n_gate))).astype(BF16),
        bg=jnp.pad(b_gate, (0, 128 - n_gate)).reshape(1, 128),
        w1big=w1big.reshape(2, half // 2, 4 * NSA_DH, 4 * CMP_HID).astype(BF16),
        cbias=cmp_bias(pe.reshape(2, -1, 1), w1f),
        w2=w2.astype(BF16),
        wout=w_out.astype(BF16),
    )


def _cover_matrix(n_entries, n_blk, n_cols, periodic):
    i = jnp.arange(n_entries)[:, None]
    col = jnp.arange(n_cols)[None, :]
    s = col % n_blk if periodic else col
    j0 = (i - 1) * CMP_STRIDE
    hit = (i >= 1) & (j0 < s * SLC_BLOCK + SLC_BLOCK) & (j0 + 2 * CMP_STRIDE > s * SLC_BLOCK)
    if not periodic:
        hit = hit & (col < n_blk)
    return hit.astype(F32)


def nsa_layer_prompt(xp, g, w, tabs):
    b, s, _ = xp.shape
    n_blk = s // SLC_BLOCK
    q, gates, kct, kst, kwt = nsa_proj(xp, g, w, tabs, tm=512)
    cmp_kv = compress(kct, w)
    gates_g = gates[..., :3 * NSA_HEADS].reshape(b, s, NSA_GROUPS, 3 * NSA_HPG).transpose(0, 2, 1, 3)
    cover = jnp.kron(jnp.eye(128 // n_blk, dtype=F32), _cover_matrix(s // CMP_STRIDE, n_blk, n_blk, False))
    e_mat = (jnp.arange(n_blk)[:, None] == jnp.arange(s)[None, :] // SLC_BLOCK).astype(BF16)
    o = nsa_attn_prompt(q, gates_g, cmp_kv, kst, kwt, cover, e_mat)
    x = resid_matmul(o.reshape(b * s, -1), w["wout"], xp.reshape(b * s, -1), tm=512)
    return x.reshape(b, s, -1), kct, kst, kwt


def nsa_layer_sample(xs, g, w, tabs, cache_cmp_t, cache_slc_t, win_t, layer, page_table):
    db, t, _ = xs.shape
    n = db * t
    past = page_table.shape[1] * PAGE
    q, gates, kct, kst, kwt = nsa_proj(xs.reshape(1, n, -1), g, w, tabs, tm=n)
    per_seq = lambda a: a.reshape(NSA_KV, db, t).transpose(1, 0, 2)
    kc_new, ks_new, kw_new = per_seq(kct), per_seq(kst), per_seq(kwt)
    pad = lambda a: jnp.pad(a, ((0, 0), (0, 0), (0, PAGE - t)))
    cmp_kv = compress(cache_cmp_t, w, npg=32, page_table=page_table, layer=layer)
    rows = lambda a, k: a.reshape(db, t, NSA_GROUPS, NSA_HPG, k).transpose(0, 2, 3, 1, 4).reshape(db, NSA_HEADS * t, k)
    q_r = rows(q.reshape(n, -1), NSA_DH)
    gates_r = rows(gates.reshape(n, -1)[:, :3 * NSA_HEADS], 3)
    n_slc = -(-(past + t) // SLC_BLOCK)
    n_slc_pad = -(-n_slc // 128) * 128
    cover = _cover_matrix(past // CMP_STRIDE, n_slc, n_slc_pad, False)
    own = jnp.arange(NSA_HEADS * t) // (NSA_HPG * t)
    q_bd = q_r[:, :, None, :] * (own[:, None] == jnp.arange(NSA_GROUPS)[None, :]).astype(BF16)[None, :, :, None]
    q_bd = q_bd.reshape(db, NSA_HEADS * t, NSA_GROUPS * NSA_DH)
    o = nsa_attn_sample(page_table, q_r, q_bd, gates_r, cmp_kv, cover, cache_slc_t, layer, pad(ks_new), win_t,
                        pad(kw_new), ppc=32)
    o = o.reshape(db, NSA_GROUPS, NSA_HPG, t, NSA_DH).transpose(0, 3, 1, 2, 4).reshape(n, -1)
    x = resid_matmul(o.astype(BF16), w["wout"], xs.reshape(n, -1), tm=n)
    return x.reshape(db, t, -1), kc_new, ks_new, kw_new


def kernel(x_prompt, x_sample, cache_mla_kv, cache_nsa_cmp_kv, cache_nsa_slc_kv, state_nsa_win_kv, state_ffn_conv,
           page_table, attn_norm, ffn_norm, final_norm, mla_w_in, mla_q_norm, mla_kv_norm, mla_w_q_up, mla_w_uk,
           mla_w_uv, mla_w_out, nsa_w_in, nsa_b_gate, nsa_cmp_pe, nsa_cmp_w1, nsa_cmp_w2, nsa_w_out, ffn_w_in,
           ffn_conv_w, ffn_conv_b, ffn_w_out):
    b, s, _ = x_prompt.shape
    db, t, _ = x_sample.shape
    depth = attn_norm.shape[0]
    past = page_table.shape[1] * PAGE
    pos_p = jnp.arange(s)
    pos_s = past + jnp.arange(db * t) % t
    mla_t = jnp.transpose(cache_mla_kv, (0, 1, 3, 2))
    to_t = lambda c: jnp.transpose(c, (0, 1, 3, 4, 5, 2)).reshape(c.shape[0], c.shape[1], NSA_KV, c.shape[2])
    cmp_t, slc_t, win_t = to_t(cache_nsa_cmp_kv), to_t(cache_nsa_slc_kv), to_t(state_nsa_win_kv)
    tabs_mla_p, tabs_mla_s = _mla_tables(pos_p), _mla_tables(pos_s)
    tabs_nsa_p, tabs_nsa_s = _nsa_tables(pos_p), _nsa_tables(pos_s)

    xp, xs = x_prompt, x_sample
    outs = {k: [] for k in ("mla_p", "mla_s", "cmp_p", "cmp_s", "slc_p", "slc_s", "win_p", "win_s", "conv_p", "conv_s")}
    rows_t = lambda a: a.transpose(0, 2, 1)
    for i in range(depth):
        l = i // 2
        g_attn = attn_norm[i].reshape(1, -1)
        if i % 2 == 0:
            w = _mla_weights(mla_w_in[l], mla_q_norm[l], mla_kv_norm[l], mla_w_q_up[l], mla_w_uk[l], mla_w_uv[l],
                             mla_w_out[l])
            xp, kvt_p = mla_layer_prompt(xp, g_attn, w, tabs_mla_p)
            xs, kv_s = mla_layer_sample(xs, g_attn, w, tabs_mla_s, mla_t, l, page_table)
            outs["mla_p"].append(kvt_p)
            outs["mla_s"].append(kv_s)
        else:
            w = _nsa_weights(nsa_w_in[l], nsa_b_gate[l], nsa_cmp_pe[l], nsa_cmp_w1[l], nsa_cmp_w2[l], nsa_w_out[l])
            xp, kct, kst, kwt = nsa_layer_prompt(xp, g_attn, w, tabs_nsa_p)
            xs, kc_n, ks_n, kw_n = nsa_layer_sample(xs, g_attn, w, tabs_nsa_s, cmp_t, slc_t, win_t, l, page_table)
            outs["cmp_p"].append(kct)
            outs["slc_p"].append(kst)
            outs["win_p"].append(kwt[:, :, s - min(WINDOW, s):])
            outs["cmp_s"].append(rows_t(kc_n))
            outs["slc_s"].append(rows_t(ks_n))
            outs["win_s"].append(jnp.concatenate([win_t[l], kw_n], axis=2)[:, :, -WINDOW:])
        fin = final_norm.reshape(1, -1) if i == depth - 1 else None
        g_ffn = ffn_norm[i].reshape(1, -1)
        f = (ffn_w_in[i].astype(BF16), ffn_conv_w[i], ffn_conv_b[i].reshape(1, -1), ffn_w_out[i].astype(BF16))
        xp, cv_p = ffn_layer_prompt(xp, g_ffn, *f, final_g=fin)
        xs, cv_s = ffn_layer_sample(xs, g_ffn, *f, state_ffn_conv[i], final_g=fin)
        outs["conv_p"].append(cv_p)
        outs["conv_s"].append(cv_s)

    kv6 = lambda a: a.reshape(a.shape[:3] + (2, NSA_GROUPS, NSA_DH))
    feat_major = lambda lst: kv6(jnp.stack(lst).transpose(0, 1, 3, 2))
    return (xp, xs,
            jnp.stack(outs["mla_p"]).transpose(0, 1, 3, 2), jnp.stack(outs["mla_s"]),
            feat_major(outs["cmp_p"]), kv6(jnp.stack(outs["cmp_s"])),
            feat_major(outs["slc_p"]), kv6(jnp.stack(outs["slc_s"])),
            feat_major(outs["win_p"]), feat_major(outs["win_s"]),
            jnp.stack(outs["conv_p"]), jnp.stack(outs["conv_s"]))
```

```python
import functools

import jax
import jax.numpy as jnp
from jax import lax
from jax.experimental import pallas as pl
from jax.experimental.pallas import tpu as pltpu

F32 = jnp.float32
BF16 = jnp.bfloat16

D_MODEL = 1024
EPS = 1e-6
ROPE_THETA = 10000.0
NEG = -1e30
PAGE = 128

MLA_HEADS = 16
MLA_Q_LORA = 384
MLA_KV_LORA = 256
MLA_NOPE = 64
MLA_ROPE = 32
MLA_V = 64
MLA_ROW = MLA_KV_LORA + MLA_ROPE
MLA_SCALE = (MLA_NOPE + MLA_ROPE) ** -0.5
MLA_QBLK = 128
MLA_HPS = 4

NSA_HEADS = 16
NSA_GROUPS = 4
NSA_HPG = 4
NSA_DH = 64
NSA_KV = 2 * NSA_GROUPS * NSA_DH
CMP_STRIDE = 16
CMP_PITCH = 24
CMP_HID = 128
SLC_BLOCK = 64
N_SELECT = 16
WINDOW = 512
FORCE_BONUS = 1e4
NSA_SCALE = NSA_DH ** -0.5

D_FF = 2816
FFN_SLAB = 2816
FFN_TM = 512
VMEM_LIMIT = 56 * 1024 * 1024

LOG2E = 1.4426950408889634
MLA_QSCALE = MLA_SCALE * LOG2E
NSA_QSCALE = NSA_SCALE * LOG2E

_NT = (((1,), (1,)), ((), ()))


def _cparams(sem):
    return pltpu.CompilerParams(dimension_semantics=sem, vmem_limit_bytes=VMEM_LIMIT)


def _rms_rows(x, g):
    ms = jnp.mean(x * x, axis=-1, keepdims=True)
    return x * lax.rsqrt(ms + EPS) * g


def _softmax_update(s, m, l):
    m_new = jnp.maximum(m, jnp.max(s, axis=-1, keepdims=True))
    alpha = jnp.exp2(m - m_new)
    p = jnp.exp2(s - m_new)
    l_new = alpha * l + jnp.sum(p, axis=-1, keepdims=True)
    return p, alpha, m_new, l_new


def _safe_div(acc, l):
    return acc / jnp.where(l > 0, l, 1.0)


def _mla_proj_kernel(x_ref, g_ref, wqa_ref, qn_ref, wqup_ref, wkvt_ref, kvn_ref, wx_ref,
                     c_ref, s1_ref, s2_ref, cost_ref, sint_ref, q_out, kvt_out, x_out, *, absorbed):
    xn = _rms_rows(x_ref[0], g_ref[...]).astype(BF16)
    qa = jnp.dot(xn, wqa_ref[...], preferred_element_type=F32)
    qn = _rms_rows(qa, qn_ref[...]).astype(BF16)
    q = jnp.dot(qn, wqup_ref[...], preferred_element_type=F32)
    cq, s1, s2 = c_ref[...], s1_ref[...], s2_ref[...]
    for h in range(MLA_HEADS):
        blk = q[:, h * MLA_QBLK:(h + 1) * MLA_QBLK]
        blk = blk * cq + pltpu.roll(blk, 16, 1) * s1 + pltpu.roll(blk, MLA_QBLK - 16, 1) * s2
        blk = blk.astype(BF16)
        q_out[0, :, h * MLA_QBLK:(h + 1) * MLA_QBLK] = blk
        if absorbed:
            q_lat = jnp.dot(blk[:, MLA_ROPE:MLA_ROPE + MLA_NOPE], wx_ref[h], preferred_element_type=F32)
            x_out[0, h, :, :MLA_KV_LORA] = q_lat.astype(BF16)
            x_out[0, h, :, MLA_KV_LORA:] = blk
    kvt = lax.dot_general(wkvt_ref[...], xn, _NT, preferred_element_type=F32)
    c = kvt[:MLA_KV_LORA]
    ms = jnp.mean(c * c, axis=0, keepdims=True)
    cn = c * lax.rsqrt(ms + EPS) * kvn_ref[...]
    k1 = kvt[MLA_KV_LORA:MLA_KV_LORA + 16]
    k2 = kvt[MLA_KV_LORA + 16:MLA_ROW]
    ct, st = cost_ref[...], sint_ref[...]
    kvt_out[0, :MLA_KV_LORA] = cn
    kvt_out[0, MLA_KV_LORA:MLA_KV_LORA + 16] = k1 * ct - k2 * st
    kvt_out[0, MLA_KV_LORA + 16:] = k1 * st + k2 * ct
    if not absorbed:
        x_out[0] = jnp.dot(wx_ref[...], cn.astype(BF16), preferred_element_type=F32).astype(BF16)


def mla_proj(x3, g, w, tabs, tm, absorbed):
    nb, L, _ = x3.shape
    nl = L // tm
    hq = MLA_HEADS * MLA_QBLK
    if absorbed:
        x_shape = jax.ShapeDtypeStruct((nb, MLA_HEADS, L, MLA_KV_LORA + MLA_QBLK), BF16)
        x_spec = pl.BlockSpec((1, MLA_HEADS, tm, MLA_KV_LORA + MLA_QBLK), lambda b, i: (b, 0, i, 0))
        wx_spec = pl.BlockSpec((MLA_HEADS, MLA_NOPE, MLA_KV_LORA), lambda b, i: (0, 0, 0))
    else:
        x_shape = jax.ShapeDtypeStruct((nb, 2 * MLA_HEADS * MLA_NOPE, L), BF16)
        x_spec = pl.BlockSpec((1, 2 * MLA_HEADS * MLA_NOPE, tm), lambda b, i: (b, 0, i))
        wx_spec = pl.BlockSpec((2 * MLA_HEADS * MLA_NOPE, MLA_KV_LORA), lambda b, i: (0, 0))
    const = lambda b, i: (0, 0)
    return pl.pallas_call(
        functools.partial(_mla_proj_kernel, absorbed=absorbed),
        grid=(nb, nl),
        in_specs=[pl.BlockSpec((1, tm, D_MODEL), lambda b, i: (b, i, 0)),
                  pl.BlockSpec((1, D_MODEL), const),
                  pl.BlockSpec((D_MODEL, MLA_Q_LORA), const),
                  pl.BlockSpec((1, MLA_Q_LORA), const),
                  pl.BlockSpec((MLA_Q_LORA, hq), const),
                  pl.BlockSpec((MLA_ROW, D_MODEL), const),
                  pl.BlockSpec((MLA_KV_LORA, 1), const),
                  wx_spec,
                  pl.BlockSpec((tm, MLA_QBLK), lambda b, i: (i, 0)),
                  pl.BlockSpec((tm, MLA_QBLK), lambda b, i: (i, 0)),
                  pl.BlockSpec((tm, MLA_QBLK), lambda b, i: (i, 0)),
                  pl.BlockSpec((16, tm), lambda b, i: (0, i)),
                  pl.BlockSpec((16, tm), lambda b, i: (0, i))],
        out_specs=[pl.BlockSpec((1, tm, hq), lambda b, i: (b, i, 0)),
                   pl.BlockSpec((1, MLA_ROW, tm), lambda b, i: (b, 0, i)),
                   x_spec],
        out_shape=[jax.ShapeDtypeStruct((nb, L, hq), BF16),
                   jax.ShapeDtypeStruct((nb, MLA_ROW, L), F32),
                   x_shape],
        compiler_params=_cparams(("parallel", "parallel")),
        name="mla_proj_abs" if absorbed else "mla_proj",
    )(x3, g, w["wqa"], w["qn"], w["wqup"], w["wkvt"], w["kvn"], w["wukT"] if absorbed else w["wukvT"],
      tabs["cq"], tabs["s1"], tabs["s2"], tabs["cosT"], tabs["sinT"])


def _mla_attn_prompt_kernel(q_ref, kn_ref, v_ref, kpe_ref, o_ref, kt_scr, vt_scr, *, tq, seq):
    qi = pl.program_id(2)

    @pl.when(qi == 0)
    def _():
        kpe = kpe_ref[0].astype(BF16)
        zeros = jnp.zeros((MLA_QBLK - MLA_ROPE - MLA_NOPE, seq), BF16)
        ones_row = jnp.where(lax.broadcasted_iota(jnp.int32, (MLA_V, seq), 0) == 0, 1.0, 0.0).astype(BF16)
        for hh in range(MLA_HPS):
            kt_scr[hh, 0:MLA_ROPE] = kpe
            kt_scr[hh, MLA_ROPE:MLA_ROPE + MLA_NOPE] = kn_ref[0, hh * MLA_NOPE:(hh + 1) * MLA_NOPE]
            kt_scr[hh, MLA_ROPE + MLA_NOPE:] = zeros
            vt_scr[hh, 0:MLA_V] = v_ref[0, hh * MLA_V:(hh + 1) * MLA_V]
            vt_scr[hh, MLA_V:] = ones_row

    row = lax.broadcasted_iota(jnp.int32, (tq, tq), 0)
    col = lax.broadcasted_iota(jnp.int32, (tq, tq), 1)
    causal = col <= row
    qs = [q_ref[0, :, hh * MLA_QBLK:(hh + 1) * MLA_QBLK] for hh in range(MLA_HPS)]

    def step(j, carry, masked):
        off = pl.multiple_of(j * tq, tq)
        new = []
        for hh in range(MLA_HPS):
            m, acc = carry[hh]
            s = jnp.dot(qs[hh], kt_scr[hh, :, pl.ds(off, tq)], preferred_element_type=F32)
            if masked:
                s = jnp.where(causal, s, NEG)
            m_new = jnp.maximum(m, jnp.max(s, axis=-1, keepdims=True))
            p = jnp.exp2(s - m_new).astype(BF16)
            acc = jnp.exp2(m - m_new) * acc + lax.dot_general(p, vt_scr[hh, :, pl.ds(off, tq)], _NT,
                                                             preferred_element_type=F32)
            new.append((m_new, acc))
        return tuple(new)

    init = (jnp.full((tq, 1), NEG, F32), jnp.zeros((tq, 2 * MLA_V), F32))
    carry = lax.fori_loop(0, qi, functools.partial(step, masked=False), (init,) * MLA_HPS)
    carry = step(qi, carry, True)
    o_ref[0] = jnp.concatenate([acc[:, :MLA_V] / acc[:, MLA_V:MLA_V + 1] for _, acc in carry],
                               axis=-1).astype(BF16)


def mla_attn_prompt(q, knv, kvt, tq):
    b, s, _ = q.shape
    hp = MLA_HEADS // MLA_HPS
    return pl.pallas_call(
        functools.partial(_mla_attn_prompt_kernel, tq=tq, seq=s),
        grid=(b, hp, s // tq),
        in_specs=[pl.BlockSpec((1, tq, MLA_HPS * MLA_QBLK), lambda bi, h, i: (bi, i, h)),
                  pl.BlockSpec((1, MLA_HPS * MLA_NOPE, s), lambda bi, h, i: (bi, h, 0)),
                  pl.BlockSpec((1, MLA_HPS * MLA_V, s), lambda bi, h, i: (bi, hp + h, 0)),
                  pl.BlockSpec((1, MLA_ROPE, s), lambda bi, h, i: (bi, MLA_KV_LORA // MLA_ROPE, 0))],
        out_specs=pl.BlockSpec((1, tq, MLA_HPS * MLA_V), lambda bi, h, i: (bi, i, h)),
        out_shape=jax.ShapeDtypeStruct((b, s, MLA_HEADS * MLA_V), BF16),
        scratch_shapes=[pltpu.VMEM((MLA_HPS, MLA_QBLK, s), BF16), pltpu.VMEM((MLA_HPS, 2 * MLA_V, s), BF16)],
        compiler_params=_cparams(("parallel", "parallel", "arbitrary")),
        name="mla_attn_prompt",
    )(q, knv, knv, kvt)


def _mla_attn_sample_kernel(pt_ref, qc_ref, *refs, ppc, t_new):
    page_refs = refs[:ppc]
    kvn_ref, wuv_ref, o_ref, kt_scr, m_scr, l_scr, acc_scr = refs[ppc:]
    c = pl.program_id(1)
    rows = MLA_HEADS * t_new

    @pl.when(c == 0)
    def _():
        m_scr[...] = jnp.full((rows, 1), NEG, F32)
        l_scr[...] = jnp.zeros((rows, 1), F32)
        acc_scr[...] = jnp.zeros((rows, MLA_KV_LORA), F32)

    for i in range(ppc):
        kt_scr[:, i * PAGE:(i + 1) * PAGE] = page_refs[i][0, 0].astype(BF16)
    q = qc_ref[0][:, :MLA_ROW]
    s = jnp.dot(q, kt_scr[...], preferred_element_type=F32)
    p, alpha, m, l = _softmax_update(s, m_scr[...], l_scr[...])
    acc = alpha * acc_scr[...] + lax.dot_general(p.astype(BF16), kt_scr[:MLA_KV_LORA, :], _NT,
                                                 preferred_element_type=F32)
    m_scr[...] = m
    l_scr[...] = l
    acc_scr[...] = acc

    @pl.when(c == pl.num_programs(1) - 1)
    def _():
        kn = kvn_ref[0].astype(BF16)
        s2 = jnp.dot(q, kn, preferred_element_type=F32)
        tq = lax.broadcasted_iota(jnp.int32, s2.shape, 0) % t_new
        tk = lax.broadcasted_iota(jnp.int32, s2.shape, 1)
        s2 = jnp.where(tk <= tq, s2, NEG)
        p2, alpha2, m2, l2 = _softmax_update(s2, m, l)
        acc2 = alpha2 * acc + lax.dot_general(p2.astype(BF16), kn[:MLA_KV_LORA], _NT, preferred_element_type=F32)
        o_lat = _safe_div(acc2, l2).astype(BF16)
        outs = [jnp.dot(o_lat[h * t_new:(h + 1) * t_new], wuv_ref[h], preferred_element_type=F32)
                for h in range(MLA_HEADS)]
        o_ref[0] = jnp.concatenate(outs, axis=-1)


def mla_attn_sample(page_table, qc, cache_t, layer, kv_new, wuv, ppc):
    db, rows, _ = qc.shape
    t_new = rows // MLA_HEADS
    n_pages = page_table.shape[1]
    pt = page_table.reshape(-1)

    def page_spec(k):
        return pl.BlockSpec((1, 1, MLA_ROW, PAGE),
                            lambda b, c, pt_ref: (layer, pt_ref[b * n_pages + c * ppc + k], 0, 0))

    grid_spec = pltpu.PrefetchScalarGridSpec(
        num_scalar_prefetch=1,
        grid=(db, n_pages // ppc),
        in_specs=[pl.BlockSpec((1, rows, qc.shape[2]), lambda b, c, pt_ref: (b, 0, 0))]
                 + [page_spec(k) for k in range(ppc)]
                 + [pl.BlockSpec((1, MLA_ROW, PAGE), lambda b, c, pt_ref: (b, 0, 0)),
                    pl.BlockSpec((MLA_HEADS, MLA_KV_LORA, MLA_V), lambda b, c, pt_ref: (0, 0, 0))],
        out_specs=pl.BlockSpec((1, t_new, MLA_HEADS * MLA_V), lambda b, c, pt_ref: (b, 0, 0)),
        scratch_shapes=[pltpu.VMEM((MLA_ROW, ppc * PAGE), BF16),
                        pltpu.VMEM((rows, 1), F32), pltpu.VMEM((rows, 1), F32),
                        pltpu.VMEM((rows, MLA_KV_LORA), F32)],
    )
    return pl.pallas_call(
        functools.partial(_mla_attn_sample_kernel, ppc=ppc, t_new=t_new),
        grid_spec=grid_spec,
        out_shape=jax.ShapeDtypeStruct((db, t_new, MLA_HEADS * MLA_V), F32),
        compiler_params=_cparams(("parallel", "arbitrary")),
        name="mla_attn_sample",
    )(pt, qc, *([cache_t] * ppc), kv_new, wuv)


def _ffn_kernel(*refs, tm, tf, tiles_per_seq, sample, final, attn):
    x_ref, g_ref, wg_ref, wu_ref, cw_ref, cb_ref, wo_ref = refs[:7]
    k = 7
    if sample:
        p1_ref, p2_ref = refs[k:k + 2]
        k += 2
    if final:
        gf_ref = refs[k]
        k += 1
    if attn:
        o_ref, wa_ref = refs[k:k + 2]
        k += 2
    y_ref, cv_ref, gs_scr, carry_scr = refs[k:]
    i = pl.program_id(0)
    ns = D_FF // tf

    if not sample:
        @pl.when((i % tiles_per_seq) == 0)
        def _():
            carry_scr[...] = jnp.zeros((ns, 8, tf), F32)

    x = x_ref[...]
    if attn:
        x = x + jnp.dot(o_ref[...], wa_ref[...], preferred_element_type=F32)
    xn = _rms_rows(x, g_ref[...]).astype(BF16)
    y = x
    for cs in range(ns):
        cols = slice(cs * tf, (cs + 1) * tf)
        g = jnp.dot(xn, wg_ref[:, cols], preferred_element_type=F32)
        u = jnp.dot(xn, wu_ref[:, cols], preferred_element_type=F32)
        gs_scr[cs, 8:8 + tm] = g
        if sample:
            gs_scr[cs, 0:8] = jnp.zeros((8, tf), F32)
            rin = lax.broadcasted_iota(jnp.int32, (tm, 1), 0) % 8
            g1 = jnp.where(rin < 1, p1_ref[:, cols], gs_scr[cs, 7:7 + tm])
            g2 = jnp.where(rin < 2, p2_ref[:, cols], gs_scr[cs, 6:6 + tm])
            cv_ref[:, cols] = g
        else:
            gs_scr[cs, 0:8] = carry_scr[cs]
            g1 = gs_scr[cs, 7:7 + tm]
            g2 = gs_scr[cs, 6:6 + tm]
            carry_scr[cs] = gs_scr[cs, tm:tm + 8]
            cv_ref[0, :, cols] = g[tm - 2:tm]
        gc = cb_ref[:, cols] + cw_ref[0:1, cols] * g2 + cw_ref[1:2, cols] * g1 + cw_ref[2:3, cols] * g
        act = (gc * jax.nn.sigmoid(gc) * u).astype(BF16)
        y = y + jnp.dot(act, wo_ref[cols, :], preferred_element_type=F32)
    if final:
        y = _rms_rows(y, gf_ref[...])
    y_ref[...] = y


def conv_ffn(x, g, w_in, cw, cb, w_out, *, tm, tf, rows_per_seq, prev=None, final_g=None, attn=None):
    n = x.shape[0]
    ns = D_FF // tf
    sample = prev is not None
    final = final_g is not None
    fixed = lambda shape, idx: pl.BlockSpec(shape, lambda i: idx, pipeline_mode=pl.Buffered(1))
    in_specs = [pl.BlockSpec((tm, D_MODEL), lambda i: (i, 0)),
                fixed((1, D_MODEL), (0, 0)),
                fixed((D_MODEL, D_FF), (0, 0)),
                fixed((D_MODEL, D_FF), (0, 1)),
                fixed((3, D_FF), (0, 0)),
                fixed((1, D_FF), (0, 0)),
                fixed((D_FF, D_MODEL), (0, 0))]
    args = [x, g, w_in, w_in, cw, cb, w_out]
    if sample:
        in_specs += [pl.BlockSpec((tm, D_FF), lambda i: (i, 0))] * 2
        args += list(prev)
        cv_shape = jax.ShapeDtypeStruct((n, D_FF), F32)
        cv_spec = pl.BlockSpec((tm, D_FF), lambda i: (i, 0))
        tiles_per_seq = 1
    else:
        tiles_per_seq = rows_per_seq // tm
        cv_shape = jax.ShapeDtypeStruct((n // tm, 2, D_FF), F32)
        cv_spec = pl.BlockSpec((1, 2, D_FF), lambda i: (i, 0, 0))
    if final:
        in_specs.append(fixed((1, D_MODEL), (0, 0)))
        args.append(final_g)
    if attn is not None:
        in_specs += [pl.BlockSpec((tm, attn[0].shape[1]), lambda i: (i, 0)), fixed(attn[1].shape, (0, 0))]
        args += list(attn)
    return pl.pallas_call(
        functools.partial(_ffn_kernel, tm=tm, tf=tf, tiles_per_seq=tiles_per_seq, sample=sample, final=final,
                          attn=attn is not None),
        grid=(n // tm,),
        in_specs=in_specs,
        out_specs=[pl.BlockSpec((tm, D_MODEL), lambda i: (i, 0)), cv_spec],
        out_shape=[jax.ShapeDtypeStruct((n, D_MODEL), F32), cv_shape],
        scratch_shapes=[pltpu.VMEM((ns, tm + 8, tf), F32),
                        pltpu.VMEM((ns, 8, tf), F32)],
        compiler_params=_cparams(("arbitrary",)),
        name="conv_ffn_sample" if sample else "conv_ffn",
    )(*args)


def _rope_rows(kvt, base, ct, st, out, half):
    for gg in range(NSA_GROUPS):
        r0 = base + gg * NSA_DH
        x1 = kvt[r0:r0 + half]
        x2 = kvt[r0 + half:r0 + NSA_DH]
        out[0, gg * NSA_DH:gg * NSA_DH + half] = x1 * ct - x2 * st
        out[0, gg * NSA_DH + half:(gg + 1) * NSA_DH] = x1 * st + x2 * ct


def _nsa_proj_kernel(x_ref, g_ref, wq_ref, wg_ref, bg_ref, wkvt_ref, cq_ref, s1_ref, s2_ref, cost_ref, sint_ref,
                     q_out, gate_out, kc_out, ks_out, kw_out):
    half = NSA_DH // 2
    xn = _rms_rows(x_ref[0], g_ref[...]).astype(BF16)
    q = jnp.dot(xn, wq_ref[...], preferred_element_type=F32)
    cq, s1, s2 = cq_ref[...], s1_ref[...], s2_ref[...]
    for v in range(NSA_HEADS * NSA_DH // 128):
        blk = q[:, v * 128:(v + 1) * 128]
        blk = blk * cq + pltpu.roll(blk, half, 1) * s1 + pltpu.roll(blk, 128 - half, 1) * s2
        q_out[0, :, v * 128:(v + 1) * 128] = blk.astype(BF16)
    gl = jnp.dot(xn, wg_ref[...], preferred_element_type=F32) + bg_ref[...]
    gate_out[0] = jax.nn.sigmoid(gl)
    kvt = lax.dot_general(wkvt_ref[...], xn, _NT, preferred_element_type=F32)
    ct, st = cost_ref[...], sint_ref[...]
    kw = NSA_GROUPS * NSA_DH
    for br, out in enumerate((kc_out, ks_out, kw_out)):
        _rope_rows(kvt, br * NSA_KV, ct, st, out, half)
        out[0, kw:] = kvt[br * NSA_KV + kw:(br + 1) * NSA_KV]


def nsa_proj(x3, g, w, tabs, tm):
    nb, L, _ = x3.shape
    const = lambda b, i: (0, 0)
    hq = NSA_HEADS * NSA_DH
    kv_spec = pl.BlockSpec((1, NSA_KV, tm), lambda b, i: (b, 0, i))
    kv_shape = jax.ShapeDtypeStruct((nb, NSA_KV, L), F32)
    return pl.pallas_call(
        _nsa_proj_kernel,
        grid=(nb, L // tm),
        in_specs=[pl.BlockSpec((1, tm, D_MODEL), lambda b, i: (b, i, 0)),
                  pl.BlockSpec((1, D_MODEL), const),
                  pl.BlockSpec((D_MODEL, hq), const),
                  pl.BlockSpec((D_MODEL, 128), const),
                  pl.BlockSpec((1, 128), const),
                  pl.BlockSpec((3 * NSA_KV, D_MODEL), const),
                  pl.BlockSpec((tm, 128), lambda b, i: (i, 0)),
                  pl.BlockSpec((tm, 128), lambda b, i: (i, 0)),
                  pl.BlockSpec((tm, 128), lambda b, i: (i, 0)),
                  pl.BlockSpec((NSA_DH // 2, tm), lambda b, i: (0, i)),
                  pl.BlockSpec((NSA_DH // 2, tm), lambda b, i: (0, i))],
        out_specs=[pl.BlockSpec((1, tm, hq), lambda b, i: (b, i, 0)),
                   pl.BlockSpec((1, tm, 128), lambda b, i: (b, i, 0)),
                   kv_spec, kv_spec, kv_spec],
        out_shape=[jax.ShapeDtypeStruct((nb, L, hq), BF16),
                   jax.ShapeDtypeStruct((nb, L, 128), F32),
                   kv_shape, kv_shape, kv_shape],
        compiler_params=_cparams(("parallel", "parallel")),
        name="nsa_proj",
    )(x3, g, w["wq"], w["wg"], w["bg"], w["wkvt"], tabs["cq"], tabs["s1"], tabs["s2"], tabs["cosT"], tabs["sinT"])


def _cmp_bias_kernel(pe_ref, w1_ref, o_ref):
    o_ref[0] = jnp.sum(w1_ref[0] * pe_ref[0], axis=0, keepdims=True)


def cmp_bias(pe_col, w1_flat):
    n = pe_col.shape[1]
    return pl.pallas_call(
        _cmp_bias_kernel,
        grid=(2,),
        in_specs=[pl.BlockSpec((1, n, 1), lambda c: (c, 0, 0)),
                  pl.BlockSpec((1, n, CMP_HID), lambda c: (c, 0, 0))],
        out_specs=pl.BlockSpec((1, 1, CMP_HID), lambda c: (c, 0, 0)),
        out_shape=jax.ShapeDtypeStruct((2, 1, CMP_HID), F32),
        name="cmp_bias",
    )(pe_col, w1_flat)


def _compress_kernel(*refs, npg, paged):
    if paged:
        refs = refs[1:]
    page_refs = refs[:npg]
    nv = NSA_KV // 128
    w1_ref, bias_ref, w2_ref, out_ref = refs[npg:npg + 4]
    xt_scrs = refs[npg + 4:npg + 4 + nv]
    fs_scr = refs[npg + 4 + nv]
    c = pl.program_id(1)
    nchunk = npg * PAGE // CMP_STRIDE
    cpp = PAGE // CMP_STRIDE

    @pl.when(c == 0)
    def _():
        fs_scr[:, 0:8] = jnp.zeros((2 * NSA_GROUPS, 8, CMP_HID), F32)

    for v in range(nv):
        cc = v // (nv // 2)
        xt_scr = xt_scrs[v]
        for p in range(npg):
            page = page_refs[p][0, 0, v * 128:(v + 1) * 128] if paged else page_refs[p][0, v * 128:(v + 1) * 128]
            page_t = page.T
            for j in range(cpp):
                r0 = (p * cpp + j) * CMP_PITCH
                xt_scr[r0:r0 + CMP_STRIDE, :] = page_t[j * CMP_STRIDE:(j + 1) * CMP_STRIDE]
        both = jnp.zeros((nchunk, 4 * CMP_HID), F32)
        for lp in range(CMP_STRIDE // 2):
            x2 = jnp.concatenate([xt_scr[pl.ds(2 * lp + i, nchunk, stride=CMP_PITCH), :] for i in range(2)],
                                 axis=1).astype(BF16)
            both = both + jnp.dot(x2, w1_ref[cc, lp], preferred_element_type=F32)
        for u in range(2):
            cg = 2 * v + u
            acc = both[:, u * 2 * CMP_HID:(u + 1) * 2 * CMP_HID]
            fs_scr[cg, 8:8 + nchunk] = acc[:, :CMP_HID]
            h = jax.nn.gelu(fs_scr[cg, 7:7 + nchunk] + acc[:, CMP_HID:] + bias_ref[cc])
            fs_scr[cg, 0:8] = fs_scr[cg, nchunk:nchunk + 8]
            out_ref[0, cg] = jnp.dot(h.astype(BF16), w2_ref[cc], preferred_element_type=F32).astype(BF16)


def compress(kv_src, w, *, npg=16, page_table=None, layer=None):
    paged = page_table is not None
    if paged:
        nb, n_pages = page_table.shape
        pt = page_table.reshape(-1)
        specs = [pl.BlockSpec((1, 1, NSA_KV, PAGE),
                              (lambda b, c, pt_ref, k=k: (layer, pt_ref[b * n_pages + c * npg + k], 0, 0)))
                 for k in range(npg)]
        imap = lambda f: (lambda b, c, pt_ref: f(b, c))
    else:
        nb = kv_src.shape[0]
        n_pages = kv_src.shape[2] // PAGE
        specs = [pl.BlockSpec((1, NSA_KV, PAGE), (lambda b, c, k=k: (b, 0, c * npg + k))) for k in range(npg)]
        imap = lambda f: f
    nchunk = npg * PAGE // CMP_STRIDE
    in_specs = specs + [pl.BlockSpec((2, CMP_STRIDE // 2, 4 * NSA_DH, 4 * CMP_HID), imap(lambda b, c: (0, 0, 0, 0))),
                        pl.BlockSpec((2, 1, CMP_HID), imap(lambda b, c: (0, 0, 0))),
                        pl.BlockSpec((2, CMP_HID, NSA_DH), imap(lambda b, c: (0, 0, 0)))]
    out_spec = pl.BlockSpec((1, 2 * NSA_GROUPS, nchunk, NSA_DH), imap(lambda b, c: (b, 0, c, 0)))
    scratch = [pltpu.VMEM((nchunk * CMP_PITCH, 128), F32)] * (NSA_KV // 128) \
        + [pltpu.VMEM((2 * NSA_GROUPS, nchunk + 8, CMP_HID), F32)]
    grid = (nb, n_pages // npg)
    out_shape = jax.ShapeDtypeStruct((nb, 2 * NSA_GROUPS, n_pages * PAGE // CMP_STRIDE, NSA_DH), BF16)
    kern = functools.partial(_compress_kernel, npg=npg, paged=paged)
    args = [kv_src] * npg + [w["w1big"], w["cbias"], w["w2"]]
    if paged:
        gs = pltpu.PrefetchScalarGridSpec(num_scalar_prefetch=1, grid=grid, in_specs=in_specs,
                                          out_specs=out_spec, scratch_shapes=scratch)
        return pl.pallas_call(kern, grid_spec=gs, out_shape=out_shape,
                              compiler_params=_cparams(("parallel", "arbitrary")), name="compress_paged")(pt, *args)
    return pl.pallas_call(kern, grid=grid, in_specs=in_specs, out_specs=out_spec, out_shape=out_shape,
                          scratch_shapes=scratch, compiler_params=_cparams(("parallel", "arbitrary")),
                          name="compress")(*args)


def _masked_softmax(s, valid):
    s = jnp.where(valid, s, NEG)
    m = jnp.max(s, axis=-1, keepdims=True)
    p = jnp.where(valid, jnp.exp2(s - m), 0.0)
    den = jnp.sum(p, axis=-1, keepdims=True)
    return p / jnp.where(den > 0, den, 1.0)


def _nsa_attn_prompt_kernel(q_ref, gate_ref, kc_ref, vc_ref, kst_ref, vst_ref, kwt_ref, vwt_ref, cover_ref, e_ref,
                            o_ref, ks_scr, vs_scr, kw_scr, vw_scr, *, tq, seq):
    i = pl.program_id(2)
    t0 = i * tq
    n_blk = seq // SLC_BLOCK

    @pl.when(i == 0)
    def _():
        ones_row = jnp.where(lax.broadcasted_iota(jnp.int32, (NSA_DH, seq), 0) == 0, 1.0, 0.0).astype(BF16)
        ks_scr[0:NSA_DH] = kst_ref[0].astype(BF16)
        ks_scr[NSA_DH:] = e_ref[...]
        vs_scr[0:NSA_DH] = vst_ref[0].astype(BF16)
        vs_scr[NSA_DH:] = ones_row
        kw_scr[...] = kwt_ref[0].astype(BF16)
        vw_scr[0:NSA_DH] = vwt_ref[0].astype(BF16)
        vw_scr[NSA_DH:] = ones_row

    q4 = jnp.concatenate([q_ref[0, :, h * NSA_DH:(h + 1) * NSA_DH] for h in range(NSA_HPG)], axis=0)
    rows = NSA_HPG * tq
    t_row = t0 + lax.broadcasted_iota(jnp.int32, (tq, 1), 0)
    t_row4 = t0 + lax.broadcasted_iota(jnp.int32, (rows, 1), 0) % tq

    def weighted(p, vt):
        return lax.dot_general(p.astype(BF16), vt, _NT, preferred_element_type=F32)

    ci = lax.broadcasted_iota(jnp.int32, (rows, kc_ref.shape[2]), 1)
    valid_c = (ci >= 1) & (CMP_STRIDE * ci + CMP_STRIDE - 1 <= t_row4)
    s_c = lax.dot_general(q4, kc_ref[0, 0], _NT, preferred_element_type=F32)
    p_c = _masked_softmax(s_c, valid_c)
    o_cmp = jnp.dot(p_c.astype(BF16), vc_ref[0, 0], preferred_element_type=F32)

    p_sum = p_c[0:tq]
    for h in range(1, NSA_HPG):
        p_sum = p_sum + p_c[h * tq:(h + 1) * tq]
    nq = 128 // n_blk
    rq = tq // nq
    p_wide = jnp.concatenate([p_sum[u * rq:(u + 1) * rq] for u in range(nq)], axis=1)
    imp = jnp.dot(p_wide, cover_ref[...], preferred_element_type=F32, precision=lax.Precision.HIGHEST)
    lane = lax.broadcasted_iota(jnp.int32, (rq, 128), 1)
    blk = lane % n_blk
    t_pk = t0 + (lane // n_blk) * rq + lax.broadcasted_iota(jnp.int32, (rq, 128), 0)
    cur = t_pk // SLC_BLOCK
    forced = (blk == 0) | ((blk <= cur) & (blk > cur - 2))
    score = jnp.where(blk * SLC_BLOCK <= t_pk, imp + jnp.where(forced, FORCE_BONUS, 0.0), NEG)
    cnt = jnp.zeros((rq, 128), F32)
    for k in range(1, n_blk):
        other = jnp.where(blk >= k, pltpu.roll(score, k, 1), pltpu.roll(score, 128 - n_blk + k, 1))
        beats = (other > score) | ((other == score) & (blk >= k))
        cnt = cnt + jnp.where(beats, 1.0, 0.0)
    sel_pk = jnp.where(cnt < N_SELECT, 0.0, NEG)
    sel_bias = jnp.concatenate([sel_pk[:, u * n_blk:(u + 1) * n_blk] for u in range(nq)], axis=0).astype(BF16)
    q_aug = jnp.concatenate([q4, jnp.concatenate([sel_bias] * NSA_HPG, axis=0)], axis=1)

    d_iota = lax.broadcasted_iota(jnp.int32, (tq, tq), 0) - lax.broadcasted_iota(jnp.int32, (tq, tq), 1)
    causal_bias = jnp.where(d_iota >= 0, 0.0, NEG)

    def slc_step(j, carry, diagonal):
        m, acc = carry
        off = pl.multiple_of(j * tq, tq)
        s = jnp.dot(q_aug, ks_scr[:, pl.ds(off, tq)], preferred_element_type=F32)
        if diagonal:
            s = (s.reshape(NSA_HPG, tq, tq) + causal_bias[None]).reshape(rows, tq)
        m_new = jnp.maximum(m, jnp.max(s, axis=-1, keepdims=True))
        alpha = jnp.exp2(m - m_new)
        acc = alpha * acc + weighted(jnp.exp2(s - m_new), vs_scr[:, pl.ds(off, tq)])
        return m_new, acc

    init = (jnp.full((rows, 1), NEG, F32), jnp.zeros((rows, 2 * NSA_DH), F32))
    carry = lax.fori_loop(0, i, functools.partial(slc_step, diagonal=False), init)
    _, acc_s = slc_step(i, carry, True)
    o_slc = acc_s[:, :NSA_DH] / acc_s[:, NSA_DH:NSA_DH + 1]

    span = WINDOW + tq
    w_off = pl.multiple_of(jnp.maximum(t0 - WINDOW, 0), tq)
    s_w = jnp.dot(q4, kw_scr[:, pl.ds(w_off, span)], preferred_element_type=F32)
    d = t_row - (w_off + lax.broadcasted_iota(jnp.int32, (tq, span), 1))
    bias_w = jnp.where((d >= 0) & (d < WINDOW), 0.0, NEG)
    s_w = (s_w.reshape(NSA_HPG, tq, span) + bias_w[None]).reshape(rows, span)
    acc_w = weighted(jnp.exp2(s_w - jnp.max(s_w, axis=-1, keepdims=True)), vw_scr[:, pl.ds(w_off, span)])
    o_win = acc_w[:, :NSA_DH] / acc_w[:, NSA_DH:NSA_DH + 1]

    outs = []
    gate = gate_ref[0, 0]
    for h in range(NSA_HPG):
        r = slice(h * tq, (h + 1) * tq)
        outs.append(gate[:, 3 * h:3 * h + 1] * o_cmp[r] + gate[:, 3 * h + 1:3 * h + 2] * o_slc[r]
                    + gate[:, 3 * h + 2:3 * h + 3] * o_win[r])
    o_ref[0] = jnp.concatenate(outs, axis=-1).astype(BF16)


def nsa_attn_prompt(q, gates_g, cmp_kv, kst, kwt, cover, e_mat, tq=512):
    b, s, _ = q.shape
    g = NSA_GROUPS
    gw = NSA_HPG * NSA_DH
    ncmp = cmp_kv.shape[2]
    n_blk = s // SLC_BLOCK
    kt_spec = lambda off: pl.BlockSpec((1, NSA_DH, s), lambda bi, gi, i: (bi, off + gi, 0))
    return pl.pallas_call(
        functools.partial(_nsa_attn_prompt_kernel, tq=tq, seq=s),
        grid=(b, g, s // tq),
        in_specs=[pl.BlockSpec((1, tq, gw), lambda bi, gi, i: (bi, i, gi)),
                  pl.BlockSpec((1, 1, tq, 3 * NSA_HPG), lambda bi, gi, i: (bi, gi, i, 0)),
                  pl.BlockSpec((1, 1, ncmp, NSA_DH), lambda bi, gi, i: (bi, gi, 0, 0)),
                  pl.BlockSpec((1, 1, ncmp, NSA_DH), lambda bi, gi, i: (bi, g + gi, 0, 0)),
                  kt_spec(0), kt_spec(g), kt_spec(0), kt_spec(g),
                  pl.BlockSpec(cover.shape, lambda bi, gi, i: (0, 0)),
                  pl.BlockSpec((n_blk, s), lambda bi, gi, i: (0, 0))],
        out_specs=pl.BlockSpec((1, tq, gw), lambda bi, gi, i: (bi, i, gi)),
        out_shape=jax.ShapeDtypeStruct((b, s, NSA_HEADS * NSA_DH), BF16),
        scratch_shapes=[pltpu.VMEM((NSA_DH + n_blk, s), BF16), pltpu.VMEM((2 * NSA_DH, s), BF16),
                        pltpu.VMEM((NSA_DH, s), BF16), pltpu.VMEM((2 * NSA_DH, s), BF16)],
        compiler_params=_cparams(("parallel", "parallel", "arbitrary")),
        name="nsa_attn_prompt",
    )(q, gates_g, cmp_kv, cmp_kv, kst, kst, kwt, kwt, cover, e_mat)


def _nsa_cmp_sample_kernel(q_ref, ckv_ref, cover_ref, ocmp_ref, imp_ref, *, t_new, past):
    g = NSA_GROUPS
    rg = NSA_HPG * t_new
    tt_g = lax.broadcasted_iota(jnp.int32, (rg, 1), 0) % t_new
    p_sums = []
    for gg in range(g):
        qg = q_ref[0, gg * rg:(gg + 1) * rg]
        s = lax.dot_general(qg, ckv_ref[0, gg], _NT, preferred_element_type=F32)
        ci = lax.broadcasted_iota(jnp.int32, s.shape, 1)
        valid = (ci >= 1) & (CMP_STRIDE * ci + CMP_STRIDE - 1 <= past + tt_g)
        p = _masked_softmax(s, valid)
        ocmp_ref[0, gg * rg:(gg + 1) * rg] = jnp.dot(p.astype(BF16), ckv_ref[0, g + gg], preferred_element_type=F32)
        p_sum = p[0:t_new]
        for h in range(1, NSA_HPG):
            p_sum = p_sum + p[h * t_new:(h + 1) * t_new]
        p_sums.append(p_sum)
    imp_ref[0] = jnp.dot(jnp.concatenate(p_sums, axis=0), cover_ref[...], preferred_element_type=F32,
                         precision=lax.Precision.HIGHEST)


def nsa_cmp_sample(q, cmp_kv, cover, past):
    db, rows, _ = q.shape
    t_new = rows // NSA_HEADS
    ncmp = cmp_kv.shape[2]
    n_slc_pad = cover.shape[1]
    return pl.pallas_call(
        functools.partial(_nsa_cmp_sample_kernel, t_new=t_new, past=past),
        grid=(db,),
        in_specs=[pl.BlockSpec((1, rows, NSA_DH), lambda b: (b, 0, 0)),
                  pl.BlockSpec((1, 2 * NSA_GROUPS, ncmp, NSA_DH), lambda b: (b, 0, 0, 0)),
                  pl.BlockSpec((ncmp, n_slc_pad), lambda b: (0, 0))],
        out_specs=[pl.BlockSpec((1, rows, NSA_DH), lambda b: (b, 0, 0)),
                   pl.BlockSpec((1, NSA_GROUPS * t_new, n_slc_pad), lambda b: (b, 0, 0))],
        out_shape=[jax.ShapeDtypeStruct((db, rows, NSA_DH), F32),
                   jax.ShapeDtypeStruct((db, NSA_GROUPS * t_new, n_slc_pad), F32)],
        compiler_params=_cparams(("parallel",)),
        name="nsa_cmp_sample",
    )(q, cmp_kv, cover)


def _nsa_select_sample_kernel(imp_ref, sel_ref, *, t_new, past, bpc):
    imp = imp_ref[...]
    n_rows, n_slc_pad = imp.shape
    blk = lax.broadcasted_iota(jnp.int32, imp.shape, 1)
    tpos = past + lax.broadcasted_iota(jnp.int32, (n_rows, 1), 0) % t_new
    cur = tpos // SLC_BLOCK
    forced = (blk == 0) | ((blk <= cur) & (blk > cur - 2))
    score = jnp.where(blk * SLC_BLOCK <= tpos, imp + jnp.where(forced, FORCE_BONUS, 0.0), NEG)
    n_slc = (past + t_new + SLC_BLOCK - 1) // SLC_BLOCK
    gone = jnp.float32(-3e38)
    score = jnp.where(blk < n_slc, score, gone)
    chosen = blk < 0
    for k in range(N_SELECT):
        mx = jnp.max(score, axis=-1, keepdims=True)
        pick = jnp.min(jnp.where(score == mx, blk, n_slc_pad), axis=-1, keepdims=True)
        chosen = chosen | (blk == pick)
        score = jnp.where(blk == pick, gone, score)
    sel = jnp.where(chosen, 0.0, NEG)
    for cb in range(n_slc_pad // bpc):
        sel_ref[cb] = sel[:, cb * bpc:(cb + 1) * bpc]


def nsa_select_sample(imp, t_new, past, bpc):
    n_rows, n_slc_pad = imp.shape
    slots = n_slc_pad // bpc
    return pl.pallas_call(
        functools.partial(_nsa_select_sample_kernel, t_new=t_new, past=past, bpc=bpc),
        grid=(1,),
        in_specs=[pl.BlockSpec((n_rows, n_slc_pad), lambda i: (0, 0))],
        out_specs=pl.BlockSpec((slots, n_rows, bpc), lambda i: (0, 0, 0)),
        out_shape=jax.ShapeDtypeStruct((slots, n_rows, bpc), F32),
        compiler_params=_cparams(("arbitrary",)),
        name="nsa_select_sample",
    )(imp)


def _nsa_attn_sample_kernel(pt_ref, qbd_ref, gate_ref, ocmp_ref, sel_ref, *refs, ppc, t_new, past):
    page_refs = refs[:ppc]
    ksn_ref, win_ref, kwn_ref, e_ref, o_ref, kv_scr, m_scr, l_scr, acc_scr = refs[ppc:]
    c = pl.program_id(1)
    g = NSA_GROUPS
    rg = NSA_HPG * t_new
    rows = g * rg
    kw = g * NSA_DH
    nk = ppc * PAGE
    tt8 = lax.broadcasted_iota(jnp.int32, (t_new, 1), 0)

    @pl.when(c == 0)
    def _():
        m_scr[...] = jnp.full((rows, 1), NEG, F32)
        l_scr[...] = jnp.zeros((rows, 1), F32)
        acc_scr[...] = jnp.zeros((rows, kw), F32)

    def chosen_bias(cb, lo, hi):
        return jnp.dot(sel_ref[cb, 0].astype(BF16), e_ref[:, lo:hi], preferred_element_type=F32)

    def update(s, bias4, vt):
        nk = s.shape[-1]
        s = (s.reshape(g, NSA_HPG, t_new, nk) + bias4).reshape(rows, nk)
        p, alpha, m, l = _softmax_update(s, m_scr[...], l_scr[...])
        acc_scr[...] = alpha * acc_scr[...] + lax.dot_general(p.astype(BF16), vt, _NT, preferred_element_type=F32)
        m_scr[...] = m
        l_scr[...] = l

    def own_group(acc):
        rgrp = lax.broadcasted_iota(jnp.int32, (rows, 1), 0) // rg
        out = jnp.zeros((rows, NSA_DH), F32)
        for gg in range(g):
            out = out + jnp.where(rgrp == gg, acc[:, gg * NSA_DH:(gg + 1) * NSA_DH], 0.0)
        return out

    for i in range(ppc):
        kv_scr[:, i * PAGE:(i + 1) * PAGE] = page_refs[i][0, 0].astype(BF16)
    qbd = qbd_ref[0]
    update(jnp.dot(qbd, kv_scr[0:kw, :], preferred_element_type=F32),
           chosen_bias(c, 0, nk).reshape(g, 1, t_new, nk), kv_scr[kw:, :])

    @pl.when(c == pl.num_programs(1) - 1)
    def _():
        ksn = ksn_ref[0].astype(BF16)
        kn = lax.broadcasted_iota(jnp.int32, (t_new, PAGE), 1)
        kn_g = lax.broadcasted_iota(jnp.int32, (g * t_new, PAGE), 1)
        tt_gt = lax.broadcasted_iota(jnp.int32, (g * t_new, 1), 0) % t_new
        tail_ok = (kn_g <= tt_gt) & (kn_g < t_new)
        tail_bias = jnp.where(tail_ok, chosen_bias(past // nk, 0, PAGE), NEG)
        update(jnp.dot(qbd, ksn[0:kw], preferred_element_type=F32), tail_bias.reshape(g, 1, t_new, PAGE), ksn[kw:])
        o_slc = _safe_div(own_group(acc_scr[...]), l_scr[...])

        m_scr[...] = jnp.full((rows, 1), NEG, F32)
        l_scr[...] = jnp.zeros((rows, 1), F32)
        acc_scr[...] = jnp.zeros((rows, kw), F32)
        lw = win_ref.shape[3]
        wk = win_ref[0, 0].astype(BF16)
        kwn = kwn_ref[0].astype(BF16)
        wi = lax.broadcasted_iota(jnp.int32, (t_new, lw), 1)
        d_old = tt8 + lw - wi
        bias_old = jnp.where((d_old >= 0) & (d_old < WINDOW), 0.0, NEG)
        d_new = tt8 - kn
        bias_new = jnp.where((d_new >= 0) & (d_new < WINDOW) & (kn < t_new), 0.0, NEG)
        update(jnp.dot(qbd, wk[0:kw], preferred_element_type=F32), bias_old.reshape(1, 1, t_new, lw), wk[kw:])
        update(jnp.dot(qbd, kwn[0:kw], preferred_element_type=F32), bias_new.reshape(1, 1, t_new, PAGE), kwn[kw:])
        o_win = _safe_div(own_group(acc_scr[...]), l_scr[...])
        gate = gate_ref[0]
        o_ref[0] = gate[:, 0:1] * ocmp_ref[0] + gate[:, 1:2] * o_slc + gate[:, 2:3] * o_win


def nsa_attn_sample(page_table, q, qbd, gates, cmp_kv, cover, cache_t, layer, ks_new, win_t, kw_new, ppc):
    db, rows, _ = q.shape
    t_new = rows // NSA_HEADS
    n_pages = page_table.shape[1]
    past = n_pages * PAGE
    n_slc_pad = cover.shape[1]
    lw = win_t.shape[3]
    pt = page_table.reshape(-1)
    nk = ppc * PAGE
    bpc = nk // SLC_BLOCK
    slots = n_slc_pad // bpc
    gt = NSA_GROUPS * t_new
    e_mat = (jnp.arange(bpc)[:, None] == jnp.arange(nk)[None, :] // SLC_BLOCK).astype(BF16)

    o_cmp, imp = nsa_cmp_sample(q, cmp_kv, cover, past)
    sel = nsa_select_sample(imp.reshape(db * gt, n_slc_pad), t_new, past, bpc).reshape(slots, db, gt, bpc)

    def page_spec(k):
        return pl.BlockSpec((1, 1, NSA_KV, PAGE),
                            lambda b, c, pt_ref: (layer, pt_ref[b * n_pages + c * ppc + k], 0, 0))

    per_b = lambda *blk: pl.BlockSpec((1,) + blk, lambda b, c, pt_ref: (b,) + (0,) * len(blk))
    grid_spec = pltpu.PrefetchScalarGridSpec(
        num_scalar_prefetch=1,
        grid=(db, n_pages // ppc),
        in_specs=[per_b(rows, NSA_GROUPS * NSA_DH), per_b(rows, 3), per_b(rows, NSA_DH),
                  pl.BlockSpec((slots, 1, gt, bpc), lambda b, c, pt_ref: (0, b, 0, 0))]
                 + [page_spec(k) for k in range(ppc)]
                 + [per_b(NSA_KV, PAGE),
                    pl.BlockSpec((1, 1, NSA_KV, lw), lambda b, c, pt_ref: (layer, b, 0, 0)),
                    per_b(NSA_KV, PAGE),
                    pl.BlockSpec((bpc, nk), lambda b, c, pt_ref: (0, 0))],
        out_specs=per_b(rows, NSA_DH),
        scratch_shapes=[pltpu.VMEM((NSA_KV, ppc * PAGE), BF16),
                        pltpu.VMEM((rows, 1), F32), pltpu.VMEM((rows, 1), F32),
                        pltpu.VMEM((rows, NSA_GROUPS * NSA_DH), F32)],
    )
    return pl.pallas_call(
        functools.partial(_nsa_attn_sample_kernel, ppc=ppc, t_new=t_new, past=past),
        grid_spec=grid_spec,
        out_shape=jax.ShapeDtypeStruct((db, rows, NSA_DH), F32),
        compiler_params=_cparams(("parallel", "arbitrary")),
        name="nsa_attn_sample",
    )(pt, qbd, gates, o_cmp, sel, *([cache_t] * ppc), ks_new, win_t, kw_new, e_mat)


def _rope_cos_sin(pos, d):
    inv = ROPE_THETA ** (-jnp.arange(0, d, 2, dtype=F32) / d)
    ang = pos.astype(F32)[:, None] * inv[None, :]
    return jnp.cos(ang), jnp.sin(ang)


def _mla_tables(pos):
    cos, sin = _rope_cos_sin(pos, MLA_ROPE)
    n = pos.shape[0]
    z = lambda w: jnp.zeros((n, w), F32)
    cq = jnp.concatenate([cos, cos, jnp.ones((n, MLA_NOPE), F32), z(32)], axis=1) * MLA_QSCALE
    s1 = jnp.concatenate([z(16), sin, z(96)], axis=1) * MLA_QSCALE
    s2 = jnp.concatenate([-sin, z(112)], axis=1) * MLA_QSCALE
    return dict(cq=cq, s1=s1, s2=s2, cosT=cos.T, sinT=sin.T)


def _mla_weights(w_in, q_norm, kv_norm, w_q_up, w_uk, w_uv, w_out):
    wq = w_q_up.reshape(MLA_Q_LORA, MLA_HEADS, MLA_NOPE + MLA_ROPE)
    wq = jnp.concatenate([wq[..., MLA_NOPE:], wq[..., :MLA_NOPE],
                          jnp.zeros((MLA_Q_LORA, MLA_HEADS, MLA_QBLK - MLA_NOPE - MLA_ROPE), F32)], axis=-1)
    uk_t = jnp.transpose(w_uk, (1, 2, 0))
    uv_t = jnp.transpose(w_uv, (1, 2, 0))
    return dict(
        wqa=w_in[:, :MLA_Q_LORA].astype(BF16),
        wkvt=w_in[:, MLA_Q_LORA:].T.astype(BF16),
        qn=q_norm.reshape(1, -1),
        kvn=kv_norm.reshape(-1, 1),
        wqup=wq.reshape(MLA_Q_LORA, MLA_HEADS * MLA_QBLK).astype(BF16),
        wukvT=jnp.concatenate([uk_t.reshape(-1, MLA_KV_LORA), uv_t.reshape(-1, MLA_KV_LORA)], axis=0).astype(BF16),
        wukT=uk_t.astype(BF16),
        wuv=jnp.transpose(w_uv, (1, 0, 2)).astype(BF16),
        wout=w_out.astype(BF16),
    )


def mla_layer_prompt(xp, g, w, tabs):
    b, s, _ = xp.shape
    q, kvt, knv = mla_proj(xp, g, w, tabs, tm=1024, absorbed=False)
    o = mla_attn_prompt(q, knv, kvt, tq=1024)
    return o.reshape(b * s, -1), kvt


def mla_layer_sample(xs, g, w, tabs, cache_t, layer, page_table):
    db, t, _ = xs.shape
    n = db * t
    _, kvt, qc = mla_proj(xs.reshape(1, n, -1), g, w, tabs, tm=n, absorbed=True)
    qc = qc.reshape(MLA_HEADS, db, t, -1).transpose(1, 0, 2, 3).reshape(db, MLA_HEADS * t, -1)
    kv_new = kvt.reshape(MLA_ROW, db, t).transpose(1, 0, 2)
    kv_pad = jnp.pad(kv_new, ((0, 0), (0, 0), (0, PAGE - t)))
    o = mla_attn_sample(page_table, qc, cache_t, layer, kv_pad, w["wuv"], ppc=64)
    return o.reshape(n, -1).astype(BF16), kv_new.transpose(0, 2, 1)


def ffn_layer_prompt(xp, g, w_in, cw, cb, w_out, final_g=None, attn=None):
    b, s, _ = xp.shape
    y, cv = conv_ffn(xp.reshape(b * s, -1), g, w_in, cw, cb, w_out, tm=FFN_TM, tf=FFN_SLAB,
                     rows_per_seq=s, final_g=final_g, attn=attn)
    tiles_per_seq = s // FFN_TM
    return y.reshape(b, s, -1), cv[tiles_per_seq - 1::tiles_per_seq]


def ffn_layer_sample(xs, g, w_in, cw, cb, w_out, state, final_g=None, attn=None):
    db, t, _ = xs.shape
    z = jnp.zeros((db, t - 2, D_FF), F32)
    p1 = jnp.concatenate([state[:, 1:2], jnp.zeros((db, t - 1, D_FF), F32)], axis=1).reshape(db * t, D_FF)
    p2 = jnp.concatenate([state, z], axis=1).reshape(db * t, D_FF)
    y, gate = conv_ffn(xs.reshape(db * t, -1), g, w_in, cw, cb, w_out, tm=db * t, tf=FFN_SLAB,
                       rows_per_seq=t, prev=(p1, p2), final_g=final_g, attn=attn)
    return y.reshape(db, t, -1), gate.reshape(db, t, D_FF)[:, t - 2:]


def _nsa_tables(pos):
    cos, sin = _rope_cos_sin(pos, NSA_DH)
    z = jnp.zeros_like(sin)
    return dict(cq=jnp.concatenate([cos] * 4, axis=1) * NSA_QSCALE,
                s1=jnp.concatenate([z, sin, z, sin], axis=1) * NSA_QSCALE,
                s2=jnp.concatenate([-sin, z, -sin, z], axis=1) * NSA_QSCALE,
                cosT=cos.T, sinT=sin.T)


def _nsa_weights(w_in, b_gate, pe, w1, w2, w_out):
    hq = NSA_HEADS * NSA_DH
    n_gate = 3 * NSA_HEADS
    half = w1.shape[1] // 2
    w1f = w1.reshape(2, -1, CMP_HID)
    w1p = jnp.concatenate([w1[:, :half], w1[:, half:]], axis=-1).reshape(2, half // 2, 2, NSA_DH, 2 * CMP_HID)
    zero = jnp.zeros_like(w1p)
    w1big = jnp.stack([jnp.concatenate([w1p, zero], axis=-1), jnp.concatenate([zero, w1p], axis=-1)], axis=3)
    return dict(
        wq=w_in[:, :hq].astype(BF16),
        wkvt=w_in[:, hq:hq + 3 * NSA_KV].T.astype(BF16),
        wg=jnp.pad(w_in[:, hq + 3 * NSA_KV:], ((0, 0), (0, 128 - n_gate))).astype(BF16),
        bg=jnp.pad(b_gate, (0, 128 - n_gate)).reshape(1, 128),
        w1big=w1big.reshape(2, half // 2, 4 * NSA_DH, 4 * CMP_HID).astype(BF16),
        cbias=cmp_bias(pe.reshape(2, -1, 1), w1f),
        w2=w2.astype(BF16),
        wout=w_out.astype(BF16),
    )


def _cover_matrix(n_entries, n_blk, n_cols, periodic):
    i = jnp.arange(n_entries)[:, None]
    col = jnp.arange(n_cols)[None, :]
    s = col % n_blk if periodic else col
    j0 = (i - 1) * CMP_STRIDE
    hit = (i >= 1) & (j0 < s * SLC_BLOCK + SLC_BLOCK) & (j0 + 2 * CMP_STRIDE > s * SLC_BLOCK)
    if not periodic:
        hit = hit & (col < n_blk)
    return hit.astype(F32)


def nsa_layer_prompt(xp, g, w, tabs):
    b, s, _ = xp.shape
    n_blk = s // SLC_BLOCK
    q, gates, kct, kst, kwt = nsa_proj(xp, g, w, tabs, tm=512)
    cmp_kv = compress(kct, w)
    gates_g = gates[..., :3 * NSA_HEADS].reshape(b, s, NSA_GROUPS, 3 * NSA_HPG).transpose(0, 2, 1, 3)
    cover = jnp.kron(jnp.eye(128 // n_blk, dtype=F32), _cover_matrix(s // CMP_STRIDE, n_blk, n_blk, False))
    e_mat = (jnp.arange(n_blk)[:, None] == jnp.arange(s)[None, :] // SLC_BLOCK).astype(BF16)
    o = nsa_attn_prompt(q, gates_g, cmp_kv, kst, kwt, cover, e_mat)
    return o.reshape(b * s, -1), kct, kst, kwt


def nsa_layer_sample(xs, g, w, tabs, cache_cmp_t, cache_slc_t, win_t, layer, page_table):
    db, t, _ = xs.shape
    n = db * t
    past = page_table.shape[1] * PAGE
    q, gates, kct, kst, kwt = nsa_proj(xs.reshape(1, n, -1), g, w, tabs, tm=n)
    per_seq = lambda a: a.reshape(NSA_KV, db, t).transpose(1, 0, 2)
    kc_new, ks_new, kw_new = per_seq(kct), per_seq(kst), per_seq(kwt)
    pad = lambda a: jnp.pad(a, ((0, 0), (0, 0), (0, PAGE - t)))
    cmp_kv = compress(cache_cmp_t, w, npg=32, page_table=page_table, layer=layer)
    rows = lambda a, k: a.reshape(db, t, NSA_GROUPS, NSA_HPG, k).transpose(0, 2, 3, 1, 4).reshape(db, NSA_HEADS * t, k)
    q_r = rows(q.reshape(n, -1), NSA_DH)
    gates_r = rows(gates.reshape(n, -1)[:, :3 * NSA_HEADS], 3)
    n_slc = -(-(past + t) // SLC_BLOCK)
    n_slc_pad = -(-n_slc // 128) * 128
    cover = _cover_matrix(past // CMP_STRIDE, n_slc, n_slc_pad, False)
    own = jnp.arange(NSA_HEADS * t) // (NSA_HPG * t)
    q_bd = q_r[:, :, None, :] * (own[:, None] == jnp.arange(NSA_GROUPS)[None, :]).astype(BF16)[None, :, :, None]
    q_bd = q_bd.reshape(db, NSA_HEADS * t, NSA_GROUPS * NSA_DH)
    o = nsa_attn_sample(page_table, q_r, q_bd, gates_r, cmp_kv, cover, cache_slc_t, layer, pad(ks_new), win_t,
                        pad(kw_new), ppc=32)
    o = o.reshape(db, NSA_GROUPS, NSA_HPG, t, NSA_DH).transpose(0, 3, 1, 2, 4).reshape(n, -1)
    return o.astype(BF16), kc_new, ks_new, kw_new


def kernel(x_prompt, x_sample, cache_mla_kv, cache_nsa_cmp_kv, cache_nsa_slc_kv, state_nsa_win_kv, state_ffn_conv,
           page_table, attn_norm, ffn_norm, final_norm, mla_w_in, mla_q_norm, mla_kv_norm, mla_w_q_up, mla_w_uk,
           mla_w_uv, mla_w_out, nsa_w_in, nsa_b_gate, nsa_cmp_pe, nsa_cmp_w1, nsa_cmp_w2, nsa_w_out, ffn_w_in,
           ffn_conv_w, ffn_conv_b, ffn_w_out):
    b, s, _ = x_prompt.shape
    db, t, _ = x_sample.shape
    depth = attn_norm.shape[0]
    past = page_table.shape[1] * PAGE
    pos_p = jnp.arange(s)
    pos_s = past + jnp.arange(db * t) % t
    mla_t = jnp.transpose(cache_mla_kv, (0, 1, 3, 2))
    to_t = lambda c: jnp.transpose(c, (0, 1, 3, 4, 5, 2)).reshape(c.shape[0], c.shape[1], NSA_KV, c.shape[2])
    cmp_t, slc_t, win_t = to_t(cache_nsa_cmp_kv), to_t(cache_nsa_slc_kv), to_t(state_nsa_win_kv)
    tabs_mla_p, tabs_mla_s = _mla_tables(pos_p), _mla_tables(pos_s)
    tabs_nsa_p, tabs_nsa_s = _nsa_tables(pos_p), _nsa_tables(pos_s)

    xp, xs = x_prompt, x_sample
    outs = {k: [] for k in ("mla_p", "mla_s", "cmp_p", "cmp_s", "slc_p", "slc_s", "win_p", "win_s", "conv_p", "conv_s")}
    rows_t = lambda a: a.transpose(0, 2, 1)
    for i in range(depth):
        l = i // 2
        g_attn = attn_norm[i].reshape(1, -1)
        if i % 2 == 0:
            w = _mla_weights(mla_w_in[l], mla_q_norm[l], mla_kv_norm[l], mla_w_q_up[l], mla_w_uk[l], mla_w_uv[l],
                             mla_w_out[l])
            op, kvt_p = mla_layer_prompt(xp, g_attn, w, tabs_mla_p)
            osm, kv_s = mla_layer_sample(xs, g_attn, w, tabs_mla_s, mla_t, l, page_table)
            outs["mla_p"].append(kvt_p)
            outs["mla_s"].append(kv_s)
        else:
            w = _nsa_weights(nsa_w_in[l], nsa_b_gate[l], nsa_cmp_pe[l], nsa_cmp_w1[l], nsa_cmp_w2[l], nsa_w_out[l])
            op, kct, kst, kwt = nsa_layer_prompt(xp, g_attn, w, tabs_nsa_p)
            osm, kc_n, ks_n, kw_n = nsa_layer_sample(xs, g_attn, w, tabs_nsa_s, cmp_t, slc_t, win_t, l, page_table)
            outs["cmp_p"].append(kct)
            outs["slc_p"].append(kst)
            outs["win_p"].append(kwt[:, :, s - min(WINDOW, s):])
            outs["cmp_s"].append(rows_t(kc_n))
            outs["slc_s"].append(rows_t(ks_n))
            outs["win_s"].append(jnp.concatenate([win_t[l], kw_n], axis=2)[:, :, -WINDOW:])
        fin = final_norm.reshape(1, -1) if i == depth - 1 else None
        g_ffn = ffn_norm[i].reshape(1, -1)
        f = (ffn_w_in[i].astype(BF16), ffn_conv_w[i], ffn_conv_b[i].reshape(1, -1), ffn_w_out[i].astype(BF16))
        xp, cv_p = ffn_layer_prompt(xp, g_ffn, *f, final_g=fin, attn=(op, w["wout"]))
        xs, cv_s = ffn_layer_sample(xs, g_ffn, *f, state_ffn_conv[i], final_g=fin, attn=(osm, w["wout"]))
        outs["conv_p"].append(cv_p)
        outs["conv_s"].append(cv_s)

    kv6 = lambda a: a.reshape(a.shape[:3] + (2, NSA_GROUPS, NSA_DH))
    feat_major = lambda lst: kv6(jnp.stack(lst).transpose(0, 1, 3, 2))
    return (xp, xs,
            jnp.stack(outs["mla_p"]).transpose(0, 1, 3, 2), jnp.stack(outs["mla_s"]),
            feat_major(outs["cmp_p"]), kv6(jnp.stack(outs["cmp_s"])),
            feat_major(outs["slc_p"]), kv6(jnp.stack(outs["slc_s"])),
            feat_major(outs["win_p"]), feat_major(outs["win_s"]),
            jnp.stack(outs["conv_p"]), jnp.stack(outs["conv_s"]))
```
